```python
import math
import jax, jax.numpy as jnp
from jax import lax
import numpy as np

D_MODEL = 1024
BATCH = 2
SEQ = 8192
DEPTH = 1

GRID_W = 64
HEAD_DIM = 64
A_Q_HEADS = 8
A_KV_HEADS = 2
A_GROUP = A_Q_HEADS // A_KV_HEADS
A_WIDTH = A_Q_HEADS * HEAD_DIM
B_HEADS = 4
B_V_DIM = 2 * HEAD_DIM
B_WIDTH = B_HEADS * B_V_DIM
ROPE_THETA = 10000.0
AXIAL_THETA = 10000.0
Q_BLOCK = 128
NORM_EPS = 1e-6
D_FF = -(-8 * D_MODEL // (3 * 256)) * 256

SPLIT_SIZES = (
    A_Q_HEADS * HEAD_DIM,
    A_KV_HEADS * HEAD_DIM,
    A_KV_HEADS * HEAD_DIM,
    B_HEADS * 2 * HEAD_DIM,
    B_HEADS * 2 * HEAD_DIM,
    B_HEADS * B_V_DIM,
    D_MODEL,
    D_MODEL,
)
IN_WIDTH = sum(SPLIT_SIZES)

kernel_name = "hybrid_gqa_axial_diffattn_gated_encoder"


def rms_norm(x, g):
    xf = x.astype(jnp.float32)
    y = xf * lax.rsqrt(jnp.mean(xf * xf, axis=-1, keepdims=True) + NORM_EPS)
    return (y * g.astype(jnp.float32)).astype(x.dtype)


def rope_angles(pos, dim, theta):
    inv_freq = theta ** (-jnp.arange(0, dim, 2, dtype=jnp.float32) / dim)
    return pos[:, None] * inv_freq[None, :]


def axial_angles(seq_len):
    rows = seq_len // GRID_W
    row = jnp.broadcast_to(jnp.arange(rows, dtype=jnp.float32)[:, None], (rows, GRID_W)).reshape(-1)
    col = jnp.broadcast_to(jnp.arange(GRID_W, dtype=jnp.float32)[None, :], (rows, GRID_W)).reshape(-1)
    axis_dim = HEAD_DIM // 2
    return jnp.concatenate([rope_angles(row, axis_dim, AXIAL_THETA),
                            rope_angles(col, axis_dim, AXIAL_THETA)], axis=-1)


def apply_rope(x, ang):
    cos = jnp.cos(ang).astype(x.dtype)
    sin = jnp.sin(ang).astype(x.dtype)
    half = x.shape[-1] // 2
    x1, x2 = x[..., :half], x[..., half:]
    return jnp.concatenate([x1 * cos - x2 * sin, x2 * cos + x1 * sin], axis=-1)


def split_cols(z):
    idx, acc = [], 0
    for s in SPLIT_SIZES[:-1]:
        acc += s
        idx.append(acc)
    return jnp.split(z, idx, axis=-1)


def gqa_attention(q, k, v):
    b, kvh, g, s, d = q.shape
    nb = s // Q_BLOCK
    scale = 1.0 / math.sqrt(d)
    qb = jnp.moveaxis(q.reshape(b, kvh, g, nb, Q_BLOCK, d), 3, 0)

    def block(qi):
        sc = jnp.einsum('bkgqd,bksd->bkgqs', qi, k).astype(jnp.float32) * scale
        p = jax.nn.softmax(sc, axis=-1).astype(v.dtype)
        return jnp.einsum('bkgqs,bksd->bkgqd', p, v)

    out = lax.map(block, qb)
    out = jnp.moveaxis(out, 0, 3).reshape(b, kvh * g, s, d)
    return out


def diff_attention(q, k, v, lam):
    b, h, _, s, d = q.shape
    nb = s // Q_BLOCK
    scale = 1.0 / math.sqrt(d)
    qb = jnp.moveaxis(q.reshape(b, h, 2, nb, Q_BLOCK, d), 3, 0)

    def block(qi):
        sc = jnp.einsum('bhcqd,bhcsd->bhcqs', qi, k).astype(jnp.float32) * scale
        p = jax.nn.softmax(sc, axis=-1)
        diff = (p[:, :, 0] - lam * p[:, :, 1]).astype(v.dtype)
        return jnp.einsum('bhqs,bhse->bhqe', diff, v)

    out = lax.map(block, qb)
    return jnp.moveaxis(out, 0, 2).reshape(b, h, s, v.shape[-1])


def setup_inputs(seed: int = 0) -> dict:
    key = jax.random.key(seed)
    ks = jax.random.split(key, 20)

    def nrm(k, shape, scale):
        return jax.random.normal(k, shape, jnp.float32) * scale

    def gain(k, shape):
        return 1.0 + 0.02 * jax.random.normal(k, shape, jnp.float32)

    return {
        "x": nrm(ks[0], (BATCH, SEQ, D_MODEL), 1.0),
        "norm_mix": gain(ks[1], (DEPTH, D_MODEL)),
        "w_in": nrm(ks[2], (DEPTH, D_MODEL, IN_WIDTH), D_MODEL ** -0.5),
        "q_norm_a": gain(ks[3], (DEPTH, HEAD_DIM)),
        "k_norm_a": gain(ks[4], (DEPTH, HEAD_DIM)),
        "lambda_q1": nrm(ks[5], (DEPTH, HEAD_DIM), 0.1),
        "lambda_k1": nrm(ks[6], (DEPTH, HEAD_DIM), 0.1),
        "lambda_q2": nrm(ks[7], (DEPTH, HEAD_DIM), 0.1),
        "lambda_k2": nrm(ks[8], (DEPTH, HEAD_DIM), 0.1),
        "subln_b": gain(ks[9], (DEPTH, B_V_DIM)),
        "w_proj_a": nrm(ks[10], (DEPTH, A_WIDTH, D_MODEL), A_WIDTH ** -0.5),
        "w_proj_b": nrm(ks[11], (DEPTH, B_WIDTH, D_MODEL), B_WIDTH ** -0.5),
        "w_out": nrm(ks[12], (DEPTH, D_MODEL, D_MODEL), D_MODEL ** -0.5),
        "norm_ffn": gain(ks[13], (DEPTH, D_MODEL)),
        "w_gate_ffn": nrm(ks[14], (DEPTH, D_MODEL, D_FF), D_MODEL ** -0.5),
        "w_up_ffn": nrm(ks[15], (DEPTH, D_MODEL, D_FF), D_MODEL ** -0.5),
        "w_down_ffn": nrm(ks[16], (DEPTH, D_FF, D_MODEL), D_FF ** -0.5),
        "norm_final": gain(ks[17], (D_MODEL,)),
    }


def reference(x, norm_mix, w_in, q_norm_a, k_norm_a, lambda_q1, lambda_k1, lambda_q2, lambda_k2,
              subln_b, w_proj_a, w_proj_b, w_out, norm_ffn, w_gate_ffn, w_up_ffn, w_down_ffn,
              norm_final):
    b, s, _ = x.shape
    ang_axial = axial_angles(s)
    ang_1d = rope_angles(jnp.arange(s, dtype=jnp.float32), HEAD_DIM, ROPE_THETA)

    for l in range(DEPTH):
        lambda_init = 0.8 - 0.6 * math.exp(-0.3 * l)
        h = rms_norm(x, norm_mix[l])
        z = h @ w_in[l]
        aq, ak, av, bq, bk, bv, ga, gb = split_cols(z)

        aq = aq.reshape(b, s, A_Q_HEADS, HEAD_DIM).transpose(0, 2, 1, 3)
        ak = ak.reshape(b, s, A_KV_HEADS, HEAD_DIM).transpose(0, 2, 1, 3)
        av = av.reshape(b, s, A_KV_HEADS, HEAD_DIM).transpose(0, 2, 1, 3)
        aq = apply_rope(rms_norm(aq, q_norm_a[l]), ang_axial)
        ak = apply_rope(rms_norm(ak, k_norm_a[l]), ang_axial)
        aq = aq.reshape(b, A_KV_HEADS, A_GROUP, s, HEAD_DIM)
        oa = gqa_attention(aq, ak, av)
        oa = oa.transpose(0, 2, 1, 3).reshape(b, s, A_WIDTH)

        bq = apply_rope(bq.reshape(b, s, B_HEADS, 2, HEAD_DIM).transpose(0, 2, 3, 1, 4), ang_1d)
        bk = apply_rope(bk.reshape(b, s, B_HEADS, 2, HEAD_DIM).transpose(0, 2, 3, 1, 4), ang_1d)
        bv = bv.reshape(b, s, B_HEADS, B_V_DIM).transpose(0, 2, 1, 3)
        lam = (jnp.exp(jnp.sum(lambda_q1[l].astype(jnp.float32) * lambda_k1[l].astype(jnp.float32)))
               - jnp.exp(jnp.sum(lambda_q2[l].astype(jnp.float32) * lambda_k2[l].astype(jnp.float32)))
               + lambda_init)
        ob = diff_attention(bq, bk, bv, lam)
        ob = rms_norm(ob, subln_b[l]) * (1.0 - lambda_init)
        ob = ob.transpose(0, 2, 1, 3).reshape(b, s, B_WIDTH)

        y = jax.nn.sigmoid(ga) * (oa @ w_proj_a[l]) + jax.nn.sigmoid(gb) * (ob @ w_proj_b[l])
        x = x + y @ w_out[l]

        h = rms_norm(x, norm_ffn[l])
        x = x + (jax.nn.silu(h @ w_gate_ffn[l]) * (h @ w_up_ffn[l])) @ w_down_ffn[l]

    return rms_norm(x, norm_final)
```

```python
import functools
import math

import jax
import jax.numpy as jnp
from jax import lax
from jax.experimental import pallas as pl
from jax.experimental.pallas import tpu as pltpu

F32 = jnp.float32
BF16 = jnp.bfloat16

GRID_W = 64
HEAD_DIM = 64
HALF = HEAD_DIM // 2
A_Q_HEADS = 8
A_KV_HEADS = 2
A_GROUP = A_Q_HEADS // A_KV_HEADS
A_WIDTH = A_Q_HEADS * HEAD_DIM
B_HEADS = 4
B_V_DIM = 2 * HEAD_DIM
B_WIDTH = B_HEADS * B_V_DIM
ROPE_THETA = 10000.0
AXIAL_THETA = 10000.0
NORM_EPS = 1e-6
QK_SCALE = math.log2(math.e) / math.sqrt(HEAD_DIM)
A_STREAMS = 2

OFF_AQ = 0
OFF_AK = OFF_AQ + A_WIDTH
OFF_AV = OFF_AK + A_KV_HEADS * HEAD_DIM
OFF_BQ = OFF_AV + A_KV_HEADS * HEAD_DIM
OFF_BK = OFF_BQ + B_HEADS * 2 * HEAD_DIM
OFF_BV = OFF_BK + B_HEADS * 2 * HEAD_DIM
OFF_GATES = OFF_BV + B_WIDTH
PAIR = 2 * HEAD_DIM

PREP_TM = 512
ATTN_TQ = 256
DENSE_TM = 512
VMEM_LIMIT = 56 * 1024 * 1024


def _rms(x, axis):
    return x * lax.rsqrt(jnp.mean(x * x, axis=axis, keepdims=True) + NORM_EPS)


def _rope_t(xt, cos, sin):
    x1, x2 = xt[:HALF], xt[HALF:]
    return jnp.concatenate([x1 * cos - x2 * sin, x2 * cos + x1 * sin], axis=0)


def _prep_kernel(x_ref, nw_ref, w_ref, gq_ref, gk_ref, ca_ref, sa_ref, cb_ref, sb_ref,
                 qta_ref, ka_ref, vta_ref, qtb_ref, kb_ref, vtb_ref):
    x = x_ref[0]
    h = _rms(x, -1) * nw_ref[...]
    z = jnp.dot(h.astype(BF16), w_ref[...], preferred_element_type=F32)
    tm = z.shape[0]
    ca, sa, cb, sb = ca_ref[...], sa_ref[...], cb_ref[...], sb_ref[...]
    gq, gk = gq_ref[...], gk_ref[...]
    zeros = jnp.zeros((HEAD_DIM, tm), BF16)

    aqt = z[:, OFF_AQ:OFF_AK].T
    for hd in range(A_Q_HEADS):
        q = aqt[hd * HEAD_DIM:(hd + 1) * HEAD_DIM]
        q = _rope_t(_rms(q, 0) * gq, ca, sa) * QK_SCALE
        g = hd // A_GROUP
        qta_ref[0, hd, g * HEAD_DIM:(g + 1) * HEAD_DIM, :] = q.astype(BF16)
        qta_ref[0, hd, (1 - g) * HEAD_DIM:(2 - g) * HEAD_DIM, :] = zeros

    akt = z[:, OFF_AK:OFF_AV].T
    ks = [_rope_t(_rms(akt[g * HEAD_DIM:(g + 1) * HEAD_DIM], 0) * gk, ca, sa)
          for g in range(A_KV_HEADS)]
    ka_ref[0] = jnp.concatenate(ks, axis=0).T.astype(BF16)
    vta_ref[0, 0] = z[:, OFF_AV:OFF_BQ].T.astype(BF16)

    bqt = z[:, OFF_BQ:OFF_BK].T
    for hd in range(B_HEADS):
        for c in range(2):
            r0 = (hd * 2 + c) * HEAD_DIM
            q = _rope_t(bqt[r0:r0 + HEAD_DIM], cb, sb) * QK_SCALE
            qtb_ref[0, hd, c, c * HEAD_DIM:(c + 1) * HEAD_DIM, :] = q.astype(BF16)
            qtb_ref[0, hd, c, (1 - c) * HEAD_DIM:(2 - c) * HEAD_DIM, :] = zeros
    bkt = z[:, OFF_BK:OFF_BV].T
    ks = [_rope_t(bkt[r * HEAD_DIM:(r + 1) * HEAD_DIM], cb, sb) for r in range(2 * B_HEADS)]
    kb_ref[0] = jnp.concatenate(ks, axis=0).T.astype(BF16)
    vtb_ref[0, 0] = z[:, OFF_BV:OFF_GATES].T.astype(BF16)


def _flash_streams(q_list, k_ref, vt_ref, s_buf, mx_buf, m_ref, l_ref, acc_ref):
    n = len(q_list)
    n_chunks, kc = vt_ref.shape[1], vt_ref.shape[3]
    m_ref[...] = jnp.full(m_ref.shape, -jnp.inf, F32)
    l_ref[...] = jnp.zeros(l_ref.shape, F32)
    acc_ref[...] = jnp.zeros(acc_ref.shape, F32)

    def scores(c, slot):
        off = pl.multiple_of(c * kc, kc)
        kblk = k_ref[0, pl.ds(off, kc), :]
        for j in range(n):
            s = jnp.dot(kblk, q_list[j], preferred_element_type=F32)
            s_buf[slot, j] = s
            mx_buf[slot, j] = jnp.max(s, axis=0, keepdims=True)

    def softmax_pv(c, slot):
        vblk = vt_ref[0, c]
        for j in range(n):
            m_old = m_ref[j]
            m_new = jnp.maximum(m_old, mx_buf[slot, j])
            alpha = jnp.exp2(m_old - m_new)
            p = jnp.exp2(s_buf[slot, j] - m_new)
            l_ref[j] = alpha * l_ref[j] + jnp.sum(p, axis=0, keepdims=True)
            acc_ref[j] = alpha * acc_ref[j] + jnp.dot(vblk, p.astype(BF16),
                                                      preferred_element_type=F32)
            m_ref[j] = m_new

    scores(0, 0)

    def pair(i, carry):
        c = 2 * i
        scores(c + 1, 1)
        softmax_pv(c, 0)
        scores(c + 2, 0)
        softmax_pv(c + 1, 1)
        return carry

    lax.fori_loop(0, n_chunks // 2 - 1, pair, 0)
    scores(n_chunks - 1, 1)
    softmax_pv(n_chunks - 2, 0)
    softmax_pv(n_chunks - 1, 1)


def _attn_a_kernel(qt_ref, k_ref, vt_ref, o_ref, s_buf, mx_buf, m_ref, l_ref, acc_ref):
    q_list = [qt_ref[0, j] for j in range(A_STREAMS)]
    _flash_streams(q_list, k_ref, vt_ref, s_buf, mx_buf, m_ref, l_ref, acc_ref)
    outs = [acc_ref[j] * (1.0 / l_ref[j]) for j in range(A_STREAMS)]
    o_ref[0] = jnp.concatenate(outs, axis=0).T.astype(BF16)


def _attn_b_kernel(lambda_init, qt_ref, k_ref, vt_ref, lq1_ref, lk1_ref, lq2_ref, lk2_ref,
                   sub_ref, o_ref, s_buf, mx_buf, m_ref, l_ref, acc_ref):
    q_list = [qt_ref[0, 0, c] for c in range(2)]
    _flash_streams(q_list, k_ref, vt_ref, s_buf, mx_buf, m_ref, l_ref, acc_ref)
    lam = (jnp.exp(jnp.sum(lq1_ref[...] * lk1_ref[...], axis=-1, keepdims=True))
           - jnp.exp(jnp.sum(lq2_ref[...] * lk2_ref[...], axis=-1, keepdims=True))
           + lambda_init)
    o = acc_ref[0] * (1.0 / l_ref[0]) - lam * (acc_ref[1] * (1.0 / l_ref[1]))
    o = _rms(o, 0) * sub_ref[...] * (1.0 - lambda_init)
    o_ref[0] = o.T.astype(BF16)


def _merge_kernel(x_ref, oa_ref, ob_ref, nw_ref, wg_ref, wa_ref, wb_ref, wo_ref, x1_ref):
    x = x_ref[...]
    h = (_rms(x, -1) * nw_ref[...]).astype(BF16)
    gates = jnp.dot(h, wg_ref[...], preferred_element_type=F32)
    d = x.shape[-1]
    ya = jnp.dot(oa_ref[...], wa_ref[...], preferred_element_type=F32)
    yb = jnp.dot(ob_ref[...], wb_ref[...], preferred_element_type=F32)
    y = jax.nn.sigmoid(gates[:, :d]) * ya + jax.nn.sigmoid(gates[:, d:]) * yb
    x1_ref[...] = x + jnp.dot(y.astype(BF16), wo_ref[...], preferred_element_type=F32)


def _ffn_kernel(final, x_ref, nw_ref, wg_ref, wu_ref, wd_ref, nf_ref, o_ref):
    x = x_ref[...]
    h = (_rms(x, -1) * nw_ref[...]).astype(BF16)
    gate = jnp.dot(h, wg_ref[...], preferred_element_type=F32)
    up = jnp.dot(h, wu_ref[...], preferred_element_type=F32)
    act = (jax.nn.silu(gate) * up).astype(BF16)
    x2 = x + jnp.dot(act, wd_ref[...], preferred_element_type=F32)
    if final:
        x2 = _rms(x2, -1) * nf_ref[...]
    o_ref[...] = x2


def _const_spec(shape):
    nd = len(shape)
    return pl.BlockSpec(shape, lambda *_: (0,) * nd, pipeline_mode=pl.Buffered(1))


def _rope_angles_t(pos, dim, theta):
    inv_freq = theta ** (-jnp.arange(0, dim, 2, dtype=F32) / dim)
    return (pos[:, None] * inv_freq[None, :]).T


def _angle_tables(seq_len):
    rows = seq_len // GRID_W
    row = jnp.broadcast_to(jnp.arange(rows, dtype=F32)[:, None], (rows, GRID_W)).reshape(-1)
    col = jnp.broadcast_to(jnp.arange(GRID_W, dtype=F32)[None, :], (rows, GRID_W)).reshape(-1)
    ang_a = jnp.concatenate([_rope_angles_t(row, HALF, AXIAL_THETA),
                             _rope_angles_t(col, HALF, AXIAL_THETA)], axis=0)
    ang_b = _rope_angles_t(jnp.arange(seq_len, dtype=F32), HEAD_DIM, ROPE_THETA)
    return jnp.cos(ang_a), jnp.sin(ang_a), jnp.cos(ang_b), jnp.sin(ang_b)


def _params(*sem):
    return pltpu.CompilerParams(dimension_semantics=sem, vmem_limit_bytes=VMEM_LIMIT)


def _layer(x, lambda_init, final, norm_mix, w_in, q_norm_a, k_norm_a, lq1, lk1, lq2, lk2, subln,
           w_proj_a, w_proj_b, w_out, norm_ffn, w_gate, w_up, w_down, norm_final, tables):
    b, s, d = x.shape
    d_ff = w_gate.shape[-1]
    n_tok = b * s
    n_chunks = s // PREP_TM
    ca, sa, cb, sb = tables
    row = lambda v: v.reshape(1, -1).astype(F32)
    col = lambda v: v.reshape(-1, 1).astype(F32)

    tab_spec = pl.BlockSpec((HALF, PREP_TM), lambda bi, ti: (0, ti))
    qta, ka, vta, qtb, kb, vtb = pl.pallas_call(
        _prep_kernel,
        grid=(b, n_chunks),
        in_specs=[
            pl.BlockSpec((1, PREP_TM, d), lambda bi, ti: (bi, ti, 0)),
            _const_spec((1, d)),
            _const_spec((d, OFF_GATES)),
            _const_spec((HEAD_DIM, 1)),
            _const_spec((HEAD_DIM, 1)),
            tab_spec, tab_spec, tab_spec, tab_spec,
        ],
        out_specs=[
            pl.BlockSpec((1, A_Q_HEADS, PAIR, PREP_TM), lambda bi, ti: (bi, 0, 0, ti)),
            pl.BlockSpec((1, PREP_TM, PAIR), lambda bi, ti: (bi, ti, 0)),
            pl.BlockSpec((1, 1, PAIR, PREP_TM), lambda bi, ti: (bi, ti, 0, 0)),
            pl.BlockSpec((1, B_HEADS, 2, PAIR, PREP_TM), lambda bi, ti: (bi, 0, 0, 0, ti)),
            pl.BlockSpec((1, PREP_TM, B_HEADS * PAIR), lambda bi, ti: (bi, ti, 0)),
            pl.BlockSpec((1, 1, B_WIDTH, PREP_TM), lambda bi, ti: (bi, ti, 0, 0)),
        ],
        out_shape=[
            jax.ShapeDtypeStruct((b, A_Q_HEADS, PAIR, s), BF16),
            jax.ShapeDtypeStruct((b, s, PAIR), BF16),
            jax.ShapeDtypeStruct((b, n_chunks, PAIR, PREP_TM), BF16),
            jax.ShapeDtypeStruct((b, B_HEADS, 2, PAIR, s), BF16),
            jax.ShapeDtypeStruct((b, s, B_HEADS * PAIR), BF16),
            jax.ShapeDtypeStruct((b, n_chunks, B_WIDTH, PREP_TM), BF16),
        ],
        compiler_params=_params("parallel", "parallel"),
        name="prep",
    )(x, row(norm_mix), w_in[:, :OFF_GATES].astype(BF16), col(q_norm_a), col(k_norm_a),
      ca, sa, cb, sb)

    nq = s // ATTN_TQ
    steps_per_group = A_GROUP // A_STREAMS

    def flash_scratch(n, dv):
        return [
            pltpu.VMEM((2, n, PREP_TM, ATTN_TQ), F32),
            pltpu.VMEM((2, n, 1, ATTN_TQ), F32),
            pltpu.VMEM((n, 1, ATTN_TQ), F32),
            pltpu.VMEM((n, 1, ATTN_TQ), F32),
            pltpu.VMEM((n, dv, ATTN_TQ), F32),
        ]

    oa = pl.pallas_call(
        _attn_a_kernel,
        grid=(b, A_Q_HEADS // A_STREAMS, nq),
        in_specs=[
            pl.BlockSpec((1, A_STREAMS, PAIR, ATTN_TQ), lambda bi, hp, qi: (bi, hp, 0, qi)),
            pl.BlockSpec((1, s, PAIR), lambda bi, hp, qi: (bi, 0, 0)),
            pl.BlockSpec((1, n_chunks, HEAD_DIM, PREP_TM),
                         lambda bi, hp, qi: (bi, 0, hp // steps_per_group, 0)),
        ],
        out_specs=pl.BlockSpec((1, ATTN_TQ, A_STREAMS * HEAD_DIM), lambda bi, hp, qi: (bi, qi, hp)),
        out_shape=jax.ShapeDtypeStruct((b, s, A_WIDTH), BF16),
        scratch_shapes=flash_scratch(A_STREAMS, HEAD_DIM),
        compiler_params=_params("parallel", "parallel", "arbitrary"),
        name="attn_a",
    )(qta, ka, vta)

    lam_spec = _const_spec((1, HEAD_DIM))
    ob = pl.pallas_call(
        functools.partial(_attn_b_kernel, lambda_init),
        grid=(b, B_HEADS, nq),
        in_specs=[
            pl.BlockSpec((1, 1, 2, PAIR, ATTN_TQ), lambda bi, hd, qi: (bi, hd, 0, 0, qi)),
            pl.BlockSpec((1, s, PAIR), lambda bi, hd, qi: (bi, 0, hd)),
            pl.BlockSpec((1, n_chunks, B_V_DIM, PREP_TM), lambda bi, hd, qi: (bi, 0, hd, 0)),
            lam_spec, lam_spec, lam_spec, lam_spec,
            _const_spec((B_V_DIM, 1)),
        ],
        out_specs=pl.BlockSpec((1, ATTN_TQ, B_V_DIM), lambda bi, hd, qi: (bi, qi, hd)),
        out_shape=jax.ShapeDtypeStruct((b, s, B_WIDTH), BF16),
        scratch_shapes=flash_scratch(2, B_V_DIM),
        compiler_params=_params("parallel", "parallel", "arbitrary"),
        name="attn_b",
    )(qtb, kb, vtb, row(lq1), row(lk1), row(lq2), row(lk2), col(subln))

    tok_spec = lambda width: pl.BlockSpec((DENSE_TM, width), lambda ti: (ti, 0))
    x_flat = x.reshape(n_tok, d)
    x1 = pl.pallas_call(
        _merge_kernel,
        grid=(n_tok // DENSE_TM,),
        in_specs=[
            tok_spec(d), tok_spec(A_WIDTH), tok_spec(B_WIDTH),
            _const_spec((1, d)),
            _const_spec((d, 2 * d)),
            _const_spec((A_WIDTH, d)),
            _const_spec((B_WIDTH, d)),
            _const_spec((d, d)),
        ],
        out_specs=tok_spec(d),
        out_shape=jax.ShapeDtypeStruct((n_tok, d), F32),
        compiler_params=_params("parallel"),
        name="merge",
    )(x_flat, oa.reshape(n_tok, A_WIDTH), ob.reshape(n_tok, B_WIDTH), row(norm_mix),
      w_in[:, OFF_GATES:].astype(BF16), w_proj_a.astype(BF16), w_proj_b.astype(BF16),
      w_out.astype(BF16))

    x2 = pl.pallas_call(
        functools.partial(_ffn_kernel, final),
        grid=(n_tok // DENSE_TM,),
        in_specs=[
            tok_spec(d),
            _const_spec((1, d)),
            _const_spec((d, d_ff)),
            _const_spec((d, d_ff)),
            _const_spec((d_ff, d)),
            _const_spec((1, d)),
        ],
        out_specs=tok_spec(d),
        out_shape=jax.ShapeDtypeStruct((n_tok, d), F32),
        compiler_params=_params("parallel"),
        name="ffn",
    )(x1, row(norm_ffn), w_gate.astype(BF16), w_up.astype(BF16), w_down.astype(BF16),
      row(norm_final))
    return x2.reshape(b, s, d)


def kernel(x, norm_mix, w_in, q_norm_a, k_norm_a, lambda_q1, lambda_k1, lambda_q2, lambda_k2,
           subln_b, w_proj_a, w_proj_b, w_out, norm_ffn, w_gate_ffn, w_up_ffn, w_down_ffn,
           norm_final):
    depth = norm_mix.shape[0]
    tables = _angle_tables(x.shape[1])
    for l in range(depth):
        lambda_init = 0.8 - 0.6 * math.exp(-0.3 * l)
        x = _layer(x, lambda_init, l == depth - 1, norm_mix[l], w_in[l], q_norm_a[l], k_norm_a[l],
                   lambda_q1[l], lambda_k1[l], lambda_q2[l], lambda_k2[l], subln_b[l],
                   w_proj_a[l], w_proj_b[l], w_out[l], norm_ffn[l], w_gate_ffn[l], w_up_ffn[l],
                   w_down_ffn[l], norm_final, tables)
    return x
```

```python
import functools
import math

import jax
import jax.numpy as jnp
from jax import lax
from jax.experimental import pallas as pl
from jax.experimental.pallas import tpu as pltpu

F32 = jnp.float32
BF16 = jnp.bfloat16

GRID_W = 64
HEAD_DIM = 64
HALF = HEAD_DIM // 2
A_Q_HEADS = 8
A_KV_HEADS = 2
A_GROUP = A_Q_HEADS // A_KV_HEADS
A_WIDTH = A_Q_HEADS * HEAD_DIM
B_HEADS = 4
B_V_DIM = 2 * HEAD_DIM
B_WIDTH = B_HEADS * B_V_DIM
ROPE_THETA = 10000.0
AXIAL_THETA = 10000.0
NORM_EPS = 1e-6
QK_SCALE = math.log2(math.e) / math.sqrt(HEAD_DIM)
A_STREAMS = 2
SCORE_LAG = 2
SCORE_SLOTS = 3

OFF_AQ = 0
OFF_AK = OFF_AQ + A_WIDTH
OFF_AV = OFF_AK + A_KV_HEADS * HEAD_DIM
OFF_BQ = OFF_AV + A_KV_HEADS * HEAD_DIM
OFF_BK = OFF_BQ + B_HEADS * 2 * HEAD_DIM
OFF_BV = OFF_BK + B_HEADS * 2 * HEAD_DIM
OFF_GATES = OFF_BV + B_WIDTH
PAIR = 2 * HEAD_DIM

PREP_TM = 512
ATTN_TQ = 512
DENSE_TM = 512
VMEM_LIMIT = 56 * 1024 * 1024


def _rms(x, axis):
    return x * lax.rsqrt(jnp.mean(x * x, axis=axis, keepdims=True) + NORM_EPS)


def _rope_t(xt, cos, sin):
    x1, x2 = xt[:HALF], xt[HALF:]
    return jnp.concatenate([x1 * cos - x2 * sin, x2 * cos + x1 * sin], axis=0)


def _prep_kernel(x_ref, nw_ref, w_ref, gq_ref, gk_ref, ca_ref, sa_ref, cb_ref, sb_ref,
                 qta_ref, ka_ref, vta_ref, qtb_ref, kb_ref, vtb_ref):
    x = x_ref[0]
    h = _rms(x, -1) * nw_ref[...]
    z = jnp.dot(h.astype(BF16), w_ref[...], preferred_element_type=F32)
    tm = z.shape[0]
    ca, sa, cb, sb = ca_ref[...], sa_ref[...], cb_ref[...], sb_ref[...]
    gq, gk = gq_ref[...], gk_ref[...]
    zeros = jnp.zeros((HEAD_DIM, tm), BF16)

    aqt = z[:, OFF_AQ:OFF_AK].T
    for hd in range(A_Q_HEADS):
        q = aqt[hd * HEAD_DIM:(hd + 1) * HEAD_DIM]
        q = _rope_t(_rms(q, 0) * gq, ca, sa) * QK_SCALE
        g = hd // A_GROUP
        qta_ref[0, hd, g * HEAD_DIM:(g + 1) * HEAD_DIM, :] = q.astype(BF16)
        qta_ref[0, hd, (1 - g) * HEAD_DIM:(2 - g) * HEAD_DIM, :] = zeros

    akt = z[:, OFF_AK:OFF_AV].T
    ks = [_rope_t(_rms(akt[g * HEAD_DIM:(g + 1) * HEAD_DIM], 0) * gk, ca, sa)
          for g in range(A_KV_HEADS)]
    ka_ref[0] = jnp.concatenate(ks, axis=0).T.astype(BF16)
    vta_ref[0, 0] = z[:, OFF_AV:OFF_BQ].T.astype(BF16)

    bqt = z[:, OFF_BQ:OFF_BK].T
    for hd in range(B_HEADS):
        for c in range(2):
            r0 = (hd * 2 + c) * HEAD_DIM
            q = _rope_t(bqt[r0:r0 + HEAD_DIM], cb, sb) * QK_SCALE
            qtb_ref[0, hd, c, c * HEAD_DIM:(c + 1) * HEAD_DIM, :] = q.astype(BF16)
            qtb_ref[0, hd, c, (1 - c) * HEAD_DIM:(2 - c) * HEAD_DIM, :] = zeros
    bkt = z[:, OFF_BK:OFF_BV].T
    ks = [_rope_t(bkt[r * HEAD_DIM:(r + 1) * HEAD_DIM], cb, sb) for r in range(2 * B_HEADS)]
    kb_ref[0] = jnp.concatenate(ks, axis=0).T.astype(BF16)
    vtb_ref[0, 0] = z[:, OFF_BV:OFF_GATES].T.astype(BF16)


def _flash_streams(q_list, k_ref, vt_ref, s_buf, mx_buf, m_ref, l_ref, acc_ref):
    n = len(q_list)
    n_chunks, kc = vt_ref.shape[1], vt_ref.shape[3]
    m_ref[...] = jnp.full(m_ref.shape, -jnp.inf, F32)
    l_ref[...] = jnp.zeros(l_ref.shape, F32)
    acc_ref[...] = jnp.zeros(acc_ref.shape, F32)

    def scores(c, slot):
        off = c * kc if isinstance(c, int) else pl.multiple_of(c * kc, kc)
        kblk = k_ref[0, pl.ds(off, kc), :]
        for j in range(n):
            s = jnp.dot(kblk, q_list[j], preferred_element_type=F32)
            s_buf[slot, j] = s
            mx_buf[slot, j] = jnp.max(s, axis=0, keepdims=True)

    def softmax_pv(c, slot):
        vblk = vt_ref[0, c]
        for j in range(n):
            m_old = m_ref[j]
            m_new = jnp.maximum(m_old, mx_buf[slot, j])
            alpha = jnp.exp2(m_old - m_new)
            p = jnp.exp2(s_buf[slot, j] - m_new)
            l_ref[j] = alpha * l_ref[j] + jnp.sum(p, axis=0, keepdims=True)
            acc_ref[j] = alpha * acc_ref[j] + jnp.dot(vblk, p.astype(BF16),
                                                      preferred_element_type=F32)
            m_ref[j] = m_new

    n_slots = s_buf.shape[0]

    def stage(c, u):
        if not isinstance(c, int) or c + SCORE_LAG < n_chunks:
            scores(c + SCORE_LAG, (u + SCORE_LAG) % n_slots)
        softmax_pv(c, u)

    for c in range(SCORE_LAG):
        scores(c, c % n_slots)
    n_loop = (n_chunks - SCORE_LAG) // n_slots

    def body(i, carry):
        for u in range(n_slots):
            stage(i * n_slots + u, u)
        return carry

    lax.fori_loop(0, n_loop, body, 0)
    for c in range(n_loop * n_slots, n_chunks):
        stage(c, c % n_slots)


def _attn_a_kernel(qt_ref, k_ref, vt_ref, o_ref, s_buf, mx_buf, m_ref, l_ref, acc_ref):
    q_list = [qt_ref[0, j] for j in range(A_STREAMS)]
    _flash_streams(q_list, k_ref, vt_ref, s_buf, mx_buf, m_ref, l_ref, acc_ref)
    outs = [acc_ref[j] * (1.0 / l_ref[j]) for j in range(A_STREAMS)]
    o_ref[0] = jnp.concatenate(outs, axis=0).T.astype(BF16)


def _attn_b_kernel(lambda_init, qt_ref, k_ref, vt_ref, lq1_ref, lk1_ref, lq2_ref, lk2_ref,
                   sub_ref, o_ref, s_buf, mx_buf, m_ref, l_ref, acc_ref):
    q_list = [qt_ref[0, 0, c] for c in range(2)]
    _flash_streams(q_list, k_ref, vt_ref, s_buf, mx_buf, m_ref, l_ref, acc_ref)
    lam = (jnp.exp(jnp.sum(lq1_ref[...] * lk1_ref[...], axis=-1, keepdims=True))
           - jnp.exp(jnp.sum(lq2_ref[...] * lk2_ref[...], axis=-1, keepdims=True))
           + lambda_init)
    o = acc_ref[0] * (1.0 / l_ref[0]) - lam * (acc_ref[1] * (1.0 / l_ref[1]))
    o = _rms(o, 0) * sub_ref[...] * (1.0 - lambda_init)
    o_ref[0] = o.T.astype(BF16)


def _merge_kernel(x_ref, oa_ref, ob_ref, nw_ref, wg_ref, wa_ref, wb_ref, wo_ref, x1_ref):
    x = x_ref[...]
    h = (_rms(x, -1) * nw_ref[...]).astype(BF16)
    gates = jnp.dot(h, wg_ref[...], preferred_element_type=F32)
    d = x.shape[-1]
    ya = jnp.dot(oa_ref[...], wa_ref[...], preferred_element_type=F32)
    yb = jnp.dot(ob_ref[...], wb_ref[...], preferred_element_type=F32)
    y = jax.nn.sigmoid(gates[:, :d]) * ya + jax.nn.sigmoid(gates[:, d:]) * yb
    x1_ref[...] = x + jnp.dot(y.astype(BF16), wo_ref[...], preferred_element_type=F32)


def _ffn_kernel(final, x_ref, nw_ref, wg_ref, wu_ref, wd_ref, nf_ref, o_ref):
    x = x_ref[...]
    h = (_rms(x, -1) * nw_ref[...]).astype(BF16)
    gate = jnp.dot(h, wg_ref[...], preferred_element_type=F32)
    up = jnp.dot(h, wu_ref[...], preferred_element_type=F32)
    act = (jax.nn.silu(gate) * up).astype(BF16)
    x2 = x + jnp.dot(act, wd_ref[...], preferred_element_type=F32)
    if final:
        x2 = _rms(x2, -1) * nf_ref[...]
    o_ref[...] = x2


def _const_spec(shape):
    nd = len(shape)
    return pl.BlockSpec(shape, lambda *_: (0,) * nd, pipeline_mode=pl.Buffered(1))


def _rope_angles_t(pos, dim, theta):
    inv_freq = theta ** (-jnp.arange(0, dim, 2, dtype=F32) / dim)
    return (pos[:, None] * inv_freq[None, :]).T


def _angle_tables(seq_len):
    rows = seq_len // GRID_W
    row = jnp.broadcast_to(jnp.arange(rows, dtype=F32)[:, None], (rows, GRID_W)).reshape(-1)
    col = jnp.broadcast_to(jnp.arange(GRID_W, dtype=F32)[None, :], (rows, GRID_W)).reshape(-1)
    ang_a = jnp.concatenate([_rope_angles_t(row, HALF, AXIAL_THETA),
                             _rope_angles_t(col, HALF, AXIAL_THETA)], axis=0)
    ang_b = _rope_angles_t(jnp.arange(seq_len, dtype=F32), HEAD_DIM, ROPE_THETA)
    return jnp.cos(ang_a), jnp.sin(ang_a), jnp.cos(ang_b), jnp.sin(ang_b)


def _params(*sem):
    return pltpu.CompilerParams(dimension_semantics=sem, vmem_limit_bytes=VMEM_LIMIT)


def _layer(x, lambda_init, final, norm_mix, w_in, q_norm_a, k_norm_a, lq1, lk1, lq2, lk2, subln,
           w_proj_a, w_proj_b, w_out, norm_ffn, w_gate, w_up, w_down, norm_final, tables):
    b, s, d = x.shape
    d_ff = w_gate.shape[-1]
    n_tok = b * s
    n_chunks = s // PREP_TM
    ca, sa, cb, sb = tables
    row = lambda v: v.reshape(1, -1).astype(F32)
    col = lambda v: v.reshape(-1, 1).astype(F32)

    tab_spec = pl.BlockSpec((HALF, PREP_TM), lambda bi, ti: (0, ti))
    qta, ka, vta, qtb, kb, vtb = pl.pallas_call(
        _prep_kernel,
        grid=(b, n_chunks),
        in_specs=[
            pl.BlockSpec((1, PREP_TM, d), lambda bi, ti: (bi, ti, 0)),
            _const_spec((1, d)),
            _const_spec((d, OFF_GATES)),
            _const_spec((HEAD_DIM, 1)),
            _const_spec((HEAD_DIM, 1)),
            tab_spec, tab_spec, tab_spec, tab_spec,
        ],
        out_specs=[
            pl.BlockSpec((1, A_Q_HEADS, PAIR, PREP_TM), lambda bi, ti: (bi, 0, 0, ti)),
            pl.BlockSpec((1, PREP_TM, PAIR), lambda bi, ti: (bi, ti, 0)),
            pl.BlockSpec((1, 1, PAIR, PREP_TM), lambda bi, ti: (bi, ti, 0, 0)),
            pl.BlockSpec((1, B_HEADS, 2, PAIR, PREP_TM), lambda bi, ti: (bi, 0, 0, 0, ti)),
            pl.BlockSpec((1, PREP_TM, B_HEADS * PAIR), lambda bi, ti: (bi, ti, 0)),
            pl.BlockSpec((1, 1, B_WIDTH, PREP_TM), lambda bi, ti: (bi, ti, 0, 0)),
        ],
        out_shape=[
            jax.ShapeDtypeStruct((b, A_Q_HEADS, PAIR, s), BF16),
            jax.ShapeDtypeStruct((b, s, PAIR), BF16),
            jax.ShapeDtypeStruct((b, n_chunks, PAIR, PREP_TM), BF16),
            jax.ShapeDtypeStruct((b, B_HEADS, 2, PAIR, s), BF16),
            jax.ShapeDtypeStruct((b, s, B_HEADS * PAIR), BF16),
            jax.ShapeDtypeStruct((b, n_chunks, B_WIDTH, PREP_TM), BF16),
        ],
        compiler_params=_params("parallel", "parallel"),
        name="prep",
    )(x, row(norm_mix), w_in[:, :OFF_GATES].astype(BF16), col(q_norm_a), col(k_norm_a),
      ca, sa, cb, sb)

    nq = s // ATTN_TQ
    steps_per_group = A_GROUP // A_STREAMS

    def flash_scratch(n, dv):
        return [
            pltpu.VMEM((SCORE_SLOTS, n, PREP_TM, ATTN_TQ), F32),
            pltpu.VMEM((SCORE_SLOTS, n, 1, ATTN_TQ), F32),
            pltpu.VMEM((n, 1, ATTN_TQ), F32),
            pltpu.VMEM((n, 1, ATTN_TQ), F32),
            pltpu.VMEM((n, dv, ATTN_TQ), F32),
        ]

    oa = pl.pallas_call(
        _attn_a_kernel,
        grid=(b, A_Q_HEADS // A_STREAMS, nq),
        in_specs=[
            pl.BlockSpec((1, A_STREAMS, PAIR, ATTN_TQ), lambda bi, hp, qi: (bi, hp, 0, qi)),
            pl.BlockSpec((1, s, PAIR), lambda bi, hp, qi: (bi, 0, 0)),
            pl.BlockSpec((1, n_chunks, HEAD_DIM, PREP_TM),
                         lambda bi, hp, qi: (bi, 0, hp // steps_per_group, 0)),
        ],
        out_specs=pl.BlockSpec((1, ATTN_TQ, A_STREAMS * HEAD_DIM), lambda bi, hp, qi: (bi, qi, hp)),
        out_shape=jax.ShapeDtypeStruct((b, s, A_WIDTH), BF16),
        scratch_shapes=flash_scratch(A_STREAMS, HEAD_DIM),
        compiler_params=_params("parallel", "parallel", "arbitrary"),
        name="attn_a",
    )(qta, ka, vta)

    lam_spec = _const_spec((1, HEAD_DIM))
    ob = pl.pallas_call(
        functools.partial(_attn_b_kernel, lambda_init),
        grid=(b, B_HEADS, nq),
        in_specs=[
            pl.BlockSpec((1, 1, 2, PAIR, ATTN_TQ), lambda bi, hd, qi: (bi, hd, 0, 0, qi)),
            pl.BlockSpec((1, s, PAIR), lambda bi, hd, qi: (bi, 0, hd)),
            pl.BlockSpec((1, n_chunks, B_V_DIM, PREP_TM), lambda bi, hd, qi: (bi, 0, hd, 0)),
            lam_spec, lam_spec, lam_spec, lam_spec,
            _const_spec((B_V_DIM, 1)),
        ],
        out_specs=pl.BlockSpec((1, ATTN_TQ, B_V_DIM), lambda bi, hd, qi: (bi, qi, hd)),
        out_shape=jax.ShapeDtypeStruct((b, s, B_WIDTH), BF16),
        scratch_shapes=flash_scratch(2, B_V_DIM),
        compiler_params=_params("parallel", "parallel", "arbitrary"),
        name="attn_b",
    )(qtb, kb, vtb, row(lq1), row(lk1), row(lq2), row(lk2), col(subln))

    tok_spec = lambda width: pl.BlockSpec((DENSE_TM, width), lambda ti: (ti, 0))
    x_flat = x.reshape(n_tok, d)
    x1 = pl.pallas_call(
        _merge_kernel,
        grid=(n_tok // DENSE_TM,),
        in_specs=[
            tok_spec(d), tok_spec(A_WIDTH), tok_spec(B_WIDTH),
            _const_spec((1, d)),
            _const_spec((d, 2 * d)),
            _const_spec((A_WIDTH, d)),
            _const_spec((B_WIDTH, d)),
            _const_spec((d, d)),
        ],
        out_specs=tok_spec(d),
        out_shape=jax.ShapeDtypeStruct((n_tok, d), F32),
        compiler_params=_params("parallel"),
        name="merge",
    )(x_flat, oa.reshape(n_tok, A_WIDTH), ob.reshape(n_tok, B_WIDTH), row(norm_mix),
      w_in[:, OFF_GATES:].astype(BF16), w_proj_a.astype(BF16), w_proj_b.astype(BF16),
      w_out.astype(BF16))

    x2 = pl.pallas_call(
        functools.partial(_ffn_kernel, final),
        grid=(n_tok // DENSE_TM,),
        in_specs=[
            tok_spec(d),
            _const_spec((1, d)),
            _const_spec((d, d_ff)),
            _const_spec((d, d_ff)),
            _const_spec((d_ff, d)),
            _const_spec((1, d)),
        ],
        out_specs=tok_spec(d),
        out_shape=jax.ShapeDtypeStruct((n_tok, d), F32),
        compiler_params=_params("parallel"),
        name="ffn",
    )(x1, row(norm_ffn), w_gate.astype(BF16), w_up.astype(BF16), w_down.astype(BF16),
      row(norm_final))
    return x2.reshape(b, s, d)


def kernel(x, norm_mix, w_in, q_norm_a, k_norm_a, lambda_q1, lambda_k1, lambda_q2, lambda_k2,
           subln_b, w_proj_a, w_proj_b, w_out, norm_ffn, w_gate_ffn, w_up_ffn, w_down_ffn,
           norm_final):
    depth = norm_mix.shape[0]
    tables = _angle_tables(x.shape[1])
    for l in range(depth):
        lambda_init = 0.8 - 0.6 * math.exp(-0.3 * l)
        x = _layer(x, lambda_init, l == depth - 1, norm_mix[l], w_in[l], q_norm_a[l], k_norm_a[l],
                   lambda_q1[l], lambda_k1[l], lambda_q2[l], lambda_k2[l], subln_b[l],
                   w_proj_a[l], w_proj_b[l], w_out[l], norm_ffn[l], w_gate_ffn[l], w_up_ffn[l],
                   w_down_ffn[l], norm_final, tables)
    return x
```

```python
import functools
import math

import jax
import jax.numpy as jnp
from jax import lax
from jax.experimental import pallas as pl
from jax.experimental.pallas import tpu as pltpu

F32 = jnp.float32
BF16 = jnp.bfloat16

GRID_W = 64
HEAD_DIM = 64
HALF = HEAD_DIM // 2
A_Q_HEADS = 8
A_KV_HEADS = 2
A_GROUP = A_Q_HEADS // A_KV_HEADS
A_WIDTH = A_Q_HEADS * HEAD_DIM
B_HEADS = 4
B_V_DIM = 2 * HEAD_DIM
B_WIDTH = B_HEADS * B_V_DIM
ROPE_THETA = 10000.0
AXIAL_THETA = 10000.0
NORM_EPS = 1e-6
QK_SCALE = math.log2(math.e) / math.sqrt(HEAD_DIM)
A_STREAMS = 2
SCORE_LAG = 2
SCORE_SLOTS = 3

OFF_AQ = 0
OFF_AK = OFF_AQ + A_WIDTH
OFF_AV = OFF_AK + A_KV_HEADS * HEAD_DIM
OFF_BQ = OFF_AV + A_KV_HEADS * HEAD_DIM
OFF_BK = OFF_BQ + B_HEADS * 2 * HEAD_DIM
OFF_BV = OFF_BK + B_HEADS * 2 * HEAD_DIM
OFF_GATES = OFF_BV + B_WIDTH
SLAB = 2 * HEAD_DIM
FAST_BOUND = 40.0

PREP_TM = 512
ATTN_TQ = 512
FAST_KC = 4096
DENSE_TM = 512
VMEM_LIMIT = 56 * 1024 * 1024


def _rms(x, axis):
    return x * lax.rsqrt(jnp.mean(x * x, axis=axis, keepdims=True) + NORM_EPS)


def _rope_t(xt, cos, sin):
    x1, x2 = xt[:HALF], xt[HALF:]
    return jnp.concatenate([x1 * cos - x2 * sin, x2 * cos + x1 * sin], axis=0)


def _prep_kernel(x_ref, nw_ref, w_ref, gq_ref, gk_ref, ca_ref, sa_ref, cb_ref, sb_ref,
                 qta_ref, ka_ref, vta_ref, qtb_ref, kb_ref, vtb_ref, kn_ref):
    x = x_ref[0]
    h = _rms(x, -1) * nw_ref[...]
    z = jnp.dot(h.astype(BF16), w_ref[...], preferred_element_type=F32)
    tm = z.shape[0]
    ca, sa, cb, sb = ca_ref[...], sa_ref[...], cb_ref[...], sb_ref[...]
    gq, gk = gq_ref[...], gk_ref[...]
    pad = jnp.zeros((SLAB - HEAD_DIM - 8, tm), F32)
    one_row = (lax.broadcasted_iota(jnp.int32, (8, tm), 0) == 0).astype(F32)

    def norm(v):
        return jnp.sqrt(jnp.sum(v * v, axis=0, keepdims=True))

    def q_block(q):
        return jnp.concatenate([q, jnp.broadcast_to(norm(q), (8, tm)), pad], axis=0).astype(BF16)

    def k_slab(k):
        return jnp.concatenate([k, one_row, pad], axis=0).T.astype(BF16)

    aqt = z[:, OFF_AQ:OFF_AK].T
    for hd in range(A_Q_HEADS):
        q = aqt[hd * HEAD_DIM:(hd + 1) * HEAD_DIM]
        qta_ref[0, hd] = q_block(_rope_t(_rms(q, 0) * gq, ca, sa) * QK_SCALE)
    akt = z[:, OFF_AK:OFF_AV].T
    k_norms = []
    for g in range(A_KV_HEADS):
        k = _rope_t(_rms(akt[g * HEAD_DIM:(g + 1) * HEAD_DIM], 0) * gk, ca, sa)
        ka_ref[0, g] = k_slab(k)
        k_norms.append(norm(k))
    k_norms.append(jnp.zeros((8 - A_KV_HEADS, tm), F32))
    vta_ref[0, 0] = z[:, OFF_AV:OFF_BQ].T.astype(BF16)

    bqt = z[:, OFF_BQ:OFF_BK].T
    bkt = z[:, OFF_BK:OFF_BV].T
    for hd in range(B_HEADS):
        for c in range(2):
            r0 = (hd * 2 + c) * HEAD_DIM
            qtb_ref[0, hd, c] = q_block(_rope_t(bqt[r0:r0 + HEAD_DIM], cb, sb) * QK_SCALE)
            k = _rope_t(bkt[r0:r0 + HEAD_DIM], cb, sb)
            kb_ref[0, hd, c] = k_slab(k)
            k_norms.append(norm(k))
    vtb_ref[0, 0] = z[:, OFF_BV:OFF_GATES].T.astype(BF16)
    kn_ref[0] = jnp.concatenate(k_norms, axis=0)


def _plain_query(qt):
    qf = qt.astype(F32)
    rows = lax.broadcasted_iota(jnp.int32, qf.shape, 0)
    return jnp.where(rows < HEAD_DIM, qf, 0.0).astype(BF16)


def _shifted_query(qt, kmax):
    qf = qt.astype(F32)
    rows = lax.broadcasted_iota(jnp.int32, qf.shape, 0)
    shifted = jnp.where(rows == HEAD_DIM, -(qf * kmax), jnp.where(rows < HEAD_DIM, qf, 0.0))
    return shifted.astype(BF16)


def _flash_fast(q_list, k_of, vt_ref, l_ref, acc_ref):
    n = len(q_list)
    vb = vt_ref.shape[3]
    n_sub = FAST_KC // vb
    l_ref[...] = jnp.zeros(l_ref.shape, F32)
    acc_ref[...] = jnp.zeros(acc_ref.shape, F32)

    def body(i, carry):
        off = pl.multiple_of(i * FAST_KC, FAST_KC)
        for j in range(n):
            p = jnp.exp2(jnp.dot(k_of(j, off, FAST_KC), q_list[j], preferred_element_type=F32))
            l_ref[j] += jnp.sum(p, axis=0, keepdims=True)
            pb = p.astype(BF16)
            acc = acc_ref[j]
            for g in range(n_sub):
                acc = acc + jnp.dot(vt_ref[0, i * n_sub + g], pb[g * vb:(g + 1) * vb],
                                    preferred_element_type=F32)
            acc_ref[j] = acc
        return carry

    lax.fori_loop(0, vt_ref.shape[1] // n_sub, body, 0)


def _flash_streams(q_list, k_of, vt_ref, s_buf, mx_buf, m_ref, l_ref, acc_ref):
    n = len(q_list)
    n_chunks, kc = vt_ref.shape[1], vt_ref.shape[3]
    m_ref[...] = jnp.full(m_ref.shape, -jnp.inf, F32)
    l_ref[...] = jnp.zeros(l_ref.shape, F32)
    acc_ref[...] = jnp.zeros(acc_ref.shape, F32)

    def scores(c, slot):
        off = c * kc if isinstance(c, int) else pl.multiple_of(c * kc, kc)
        for j in range(n):
            s = jnp.dot(k_of(j, off, kc), q_list[j], preferred_element_type=F32)
            s_buf[slot, j] = s
            mx_buf[slot, j] = jnp.max(s, axis=0, keepdims=True)

    def softmax_pv(c, slot):
        vblk = vt_ref[0, c]
        for j in range(n):
            m_old = m_ref[j]
            m_new = jnp.maximum(m_old, mx_buf[slot, j])
            alpha = jnp.exp2(m_old - m_new)
            p = jnp.exp2(s_buf[slot, j] - m_new)
            l_ref[j] = alpha * l_ref[j] + jnp.sum(p, axis=0, keepdims=True)
            acc_ref[j] = alpha * acc_ref[j] + jnp.dot(vblk, p.astype(BF16),
                                                      preferred_element_type=F32)
            m_ref[j] = m_new

    n_slots = s_buf.shape[0]

    def stage(c, u):
        if not isinstance(c, int) or c + SCORE_LAG < n_chunks:
            scores(c + SCORE_LAG, (u + SCORE_LAG) % n_slots)
        softmax_pv(c, u)

    for c in range(SCORE_LAG):
        scores(c, c % n_slots)
    n_loop = (n_chunks - SCORE_LAG) // n_slots

    def body(i, carry):
        for u in range(n_slots):
            stage(i * n_slots + u, u)
        return carry

    lax.fori_loop(0, n_loop, body, 0)
    for c in range(n_loop * n_slots, n_chunks):
        stage(c, c % n_slots)


def _finish_a(o_ref, l_ref, acc_ref):
    outs = [acc_ref[j] * (1.0 / l_ref[j]) for j in range(A_STREAMS)]
    o_ref[0] = jnp.concatenate(outs, axis=0).T.astype(BF16)


def _finish_b(lambda_init, lam_refs, sub_ref, o_ref, l_ref, acc_ref):
    lq1_ref, lk1_ref, lq2_ref, lk2_ref = lam_refs
    lam = (jnp.exp(jnp.sum(lq1_ref[...] * lk1_ref[...], axis=-1, keepdims=True))
           - jnp.exp(jnp.sum(lq2_ref[...] * lk2_ref[...], axis=-1, keepdims=True))
           + lambda_init)
    o = acc_ref[0] * (1.0 / l_ref[0]) - lam * (acc_ref[1] * (1.0 / l_ref[1]))
    o = _rms(o, 0) * sub_ref[...] * (1.0 - lambda_init)
    o_ref[0] = o.T.astype(BF16)


def _k_of_a(k_ref):
    return lambda j, off, size: k_ref[0, 0, pl.ds(off, size), :]


def _k_of_b(k_ref):
    return lambda j, off, size: k_ref[0, 0, j, pl.ds(off, size), :]


def _attn_a_general(qt_ref, k_ref, vt_ref, o_ref, s_buf, mx_buf, m_ref, l_ref, acc_ref):
    q_list = [_plain_query(qt_ref[0, j]) for j in range(A_STREAMS)]
    _flash_streams(q_list, _k_of_a(k_ref), vt_ref, s_buf, mx_buf, m_ref, l_ref, acc_ref)
    _finish_a(o_ref, l_ref, acc_ref)


def _attn_a_fast(qt_ref, k_ref, vt_ref, kmax_ref, o_ref, l_ref, acc_ref):
    q_list = [_shifted_query(qt_ref[0, j], kmax_ref[0, 0]) for j in range(A_STREAMS)]
    _flash_fast(q_list, _k_of_a(k_ref), vt_ref, l_ref, acc_ref)
    _finish_a(o_ref, l_ref, acc_ref)


def _attn_b_general(lambda_init, qt_ref, k_ref, vt_ref, lq1_ref, lk1_ref, lq2_ref, lk2_ref,
                    sub_ref, o_ref, s_buf, mx_buf, m_ref, l_ref, acc_ref):
    q_list = [_plain_query(qt_ref[0, 0, c]) for c in range(2)]
    _flash_streams(q_list, _k_of_b(k_ref), vt_ref, s_buf, mx_buf, m_ref, l_ref, acc_ref)
    _finish_b(lambda_init, (lq1_ref, lk1_ref, lq2_ref, lk2_ref), sub_ref, o_ref, l_ref, acc_ref)


def _attn_b_fast(lambda_init, qt_ref, k_ref, vt_ref, kmax_ref, lq1_ref, lk1_ref, lq2_ref, lk2_ref,
                 sub_ref, o_ref, l_ref, acc_ref):
    q_list = [_shifted_query(qt_ref[0, 0, c], kmax_ref[0, 0, c]) for c in range(2)]
    _flash_fast(q_list, _k_of_b(k_ref), vt_ref, l_ref, acc_ref)
    _finish_b(lambda_init, (lq1_ref, lk1_ref, lq2_ref, lk2_ref), sub_ref, o_ref, l_ref, acc_ref)


def _merge_kernel(x_ref, oa_ref, ob_ref, nw_ref, wg_ref, wa_ref, wb_ref, wo_ref, x1_ref):
    x = x_ref[...]
    h = (_rms(x, -1) * nw_ref[...]).astype(BF16)
    gates = jnp.dot(h, wg_ref[...], preferred_element_type=F32)
    d = x.shape[-1]
    ya = jnp.dot(oa_ref[...], wa_ref[...], preferred_element_type=F32)
    yb = jnp.dot(ob_ref[...], wb_ref[...], preferred_element_type=F32)
    y = jax.nn.sigmoid(gates[:, :d]) * ya + jax.nn.sigmoid(gates[:, d:]) * yb
    x1_ref[...] = x + jnp.dot(y.astype(BF16), wo_ref[...], preferred_element_type=F32)


def _ffn_kernel(final, x_ref, nw_ref, wg_ref, wu_ref, wd_ref, nf_ref, o_ref):
    x = x_ref[...]
    h = (_rms(x, -1) * nw_ref[...]).astype(BF16)
    gate = jnp.dot(h, wg_ref[...], preferred_element_type=F32)
    up = jnp.dot(h, wu_ref[...], preferred_element_type=F32)
    act = (jax.nn.silu(gate) * up).astype(BF16)
    x2 = x + jnp.dot(act, wd_ref[...], preferred_element_type=F32)
    if final:
        x2 = _rms(x2, -1) * nf_ref[...]
    o_ref[...] = x2


def _const_spec(shape):
    nd = len(shape)
    return pl.BlockSpec(shape, lambda *_: (0,) * nd, pipeline_mode=pl.Buffered(1))


def _rope_angles_t(pos, dim, theta):
    inv_freq = theta ** (-jnp.arange(0, dim, 2, dtype=F32) / dim)
    return (pos[:, None] * inv_freq[None, :]).T


def _angle_tables(seq_len):
    rows = seq_len // GRID_W
    row = jnp.broadcast_to(jnp.arange(rows, dtype=F32)[:, None], (rows, GRID_W)).reshape(-1)
    col = jnp.broadcast_to(jnp.arange(GRID_W, dtype=F32)[None, :], (rows, GRID_W)).reshape(-1)
    ang_a = jnp.concatenate([_rope_angles_t(row, HALF, AXIAL_THETA),
                             _rope_angles_t(col, HALF, AXIAL_THETA)], axis=0)
    ang_b = _rope_angles_t(jnp.arange(seq_len, dtype=F32), HEAD_DIM, ROPE_THETA)
    return jnp.cos(ang_a), jnp.sin(ang_a), jnp.cos(ang_b), jnp.sin(ang_b)


def _params(*sem):
    return pltpu.CompilerParams(dimension_semantics=sem, vmem_limit_bytes=VMEM_LIMIT)


def _layer(x, lambda_init, final, norm_mix, w_in, q_norm_a, k_norm_a, lq1, lk1, lq2, lk2, subln,
           w_proj_a, w_proj_b, w_out, norm_ffn, w_gate, w_up, w_down, norm_final, tables):
    b, s, d = x.shape
    d_ff = w_gate.shape[-1]
    n_tok = b * s
    n_chunks = s // PREP_TM
    ca, sa, cb, sb = tables
    row = lambda v: v.reshape(1, -1).astype(F32)
    col = lambda v: v.reshape(-1, 1).astype(F32)

    tab_spec = pl.BlockSpec((HALF, PREP_TM), lambda bi, ti: (0, ti))
    qta, ka, vta, qtb, kb, vtb, kn = pl.pallas_call(
        _prep_kernel,
        grid=(b, n_chunks),
        in_specs=[
            pl.BlockSpec((1, PREP_TM, d), lambda bi, ti: (bi, ti, 0)),
            _const_spec((1, d)),
            _const_spec((d, OFF_GATES)),
            _const_spec((HEAD_DIM, 1)),
            _const_spec((HEAD_DIM, 1)),
            tab_spec, tab_spec, tab_spec, tab_spec,
        ],
        out_specs=[
            pl.BlockSpec((1, A_Q_HEADS, SLAB, PREP_TM), lambda bi, ti: (bi, 0, 0, ti)),
            pl.BlockSpec((1, A_KV_HEADS, PREP_TM, SLAB), lambda bi, ti: (bi, 0, ti, 0)),
            pl.BlockSpec((1, 1, A_KV_HEADS * HEAD_DIM, PREP_TM), lambda bi, ti: (bi, ti, 0, 0)),
            pl.BlockSpec((1, B_HEADS, 2, SLAB, PREP_TM), lambda bi, ti: (bi, 0, 0, 0, ti)),
            pl.BlockSpec((1, B_HEADS, 2, PREP_TM, SLAB), lambda bi, ti: (bi, 0, 0, ti, 0)),
            pl.BlockSpec((1, 1, B_WIDTH, PREP_TM), lambda bi, ti: (bi, ti, 0, 0)),
            pl.BlockSpec((1, 16, PREP_TM), lambda bi, ti: (bi, 0, ti)),
        ],
        out_shape=[
            jax.ShapeDtypeStruct((b, A_Q_HEADS, SLAB, s), BF16),
            jax.ShapeDtypeStruct((b, A_KV_HEADS, s, SLAB), BF16),
            jax.ShapeDtypeStruct((b, n_chunks, A_KV_HEADS * HEAD_DIM, PREP_TM), BF16),
            jax.ShapeDtypeStruct((b, B_HEADS, 2, SLAB, s), BF16),
            jax.ShapeDtypeStruct((b, B_HEADS, 2, s, SLAB), BF16),
            jax.ShapeDtypeStruct((b, n_chunks, B_WIDTH, PREP_TM), BF16),
            jax.ShapeDtypeStruct((b, 16, s), F32),
        ],
        compiler_params=_params("parallel", "parallel"),
        name="prep",
    )(x, row(norm_mix), w_in[:, :OFF_GATES].astype(BF16), col(q_norm_a), col(k_norm_a),
      ca, sa, cb, sb)

    nq = s // ATTN_TQ
    steps_per_group = A_GROUP // A_STREAMS
    kmax_a = jnp.max(kn[:, :A_KV_HEADS], axis=-1)
    kmax_b = jnp.max(kn[:, 8:], axis=-1).reshape(b, B_HEADS, 2)
    qmax_a = jnp.max(qta[:, :, HEAD_DIM, :].astype(F32), axis=-1)
    qmax_b = jnp.max(qtb[:, :, :, HEAD_DIM, :].astype(F32), axis=-1)
    fast_a = jnp.max(qmax_a * jnp.repeat(kmax_a, A_GROUP, axis=1)) < FAST_BOUND
    fast_b = jnp.max(qmax_b * kmax_b) < FAST_BOUND

    def general_scratch(n, dv):
        return [
            pltpu.VMEM((SCORE_SLOTS, n, PREP_TM, ATTN_TQ), F32),
            pltpu.VMEM((SCORE_SLOTS, n, 1, ATTN_TQ), F32),
            pltpu.VMEM((n, 1, ATTN_TQ), F32),
            pltpu.VMEM((n, 1, ATTN_TQ), F32),
            pltpu.VMEM((n, dv, ATTN_TQ), F32),
        ]

    def fast_scratch(n, dv):
        return [pltpu.VMEM((n, 1, ATTN_TQ), F32), pltpu.VMEM((n, dv, ATTN_TQ), F32)]

    a_specs = [
        pl.BlockSpec((1, A_STREAMS, SLAB, ATTN_TQ), lambda bi, hp, qi: (bi, hp, 0, qi)),
        pl.BlockSpec((1, 1, s, SLAB), lambda bi, hp, qi: (bi, hp // steps_per_group, 0, 0)),
        pl.BlockSpec((1, n_chunks, HEAD_DIM, PREP_TM),
                     lambda bi, hp, qi: (bi, 0, hp // steps_per_group, 0)),
    ]
    a_kmax_spec = pl.BlockSpec((1, 1, 1, ATTN_TQ), lambda bi, hp, qi: (bi, hp // steps_per_group, 0, 0))
    a_common = dict(
        grid=(b, A_Q_HEADS // A_STREAMS, nq),
        out_specs=pl.BlockSpec((1, ATTN_TQ, A_STREAMS * HEAD_DIM), lambda bi, hp, qi: (bi, qi, hp)),
        out_shape=jax.ShapeDtypeStruct((b, s, A_WIDTH), BF16),
        compiler_params=_params("parallel", "parallel", "arbitrary"),
    )

    def attn_a_fast():
        kmax = jnp.broadcast_to(kmax_a[:, :, None, None], (b, A_KV_HEADS, 1, ATTN_TQ))
        return pl.pallas_call(_attn_a_fast, in_specs=a_specs + [a_kmax_spec],
                              scratch_shapes=fast_scratch(A_STREAMS, HEAD_DIM), name="attn_a_fast",
                              **a_common)(qta, ka, vta, kmax)

    def attn_a_general():
        return pl.pallas_call(_attn_a_general, in_specs=a_specs,
                              scratch_shapes=general_scratch(A_STREAMS, HEAD_DIM), name="attn_a",
                              **a_common)(qta, ka, vta)

    oa = lax.cond(fast_a, attn_a_fast, attn_a_general)

    lam_spec = _const_spec((1, HEAD_DIM))
    b_specs = [
        pl.BlockSpec((1, 1, 2, SLAB, ATTN_TQ), lambda bi, hd, qi: (bi, hd, 0, 0, qi)),
        pl.BlockSpec((1, 1, 2, s, SLAB), lambda bi, hd, qi: (bi, hd, 0, 0, 0)),
        pl.BlockSpec((1, n_chunks, B_V_DIM, PREP_TM), lambda bi, hd, qi: (bi, 0, hd, 0)),
    ]
    b_kmax_spec = pl.BlockSpec((1, 1, 2, 1, ATTN_TQ), lambda bi, hd, qi: (bi, hd, 0, 0, 0))
    b_tail_specs = [lam_spec, lam_spec, lam_spec, lam_spec, _const_spec((B_V_DIM, 1))]
    b_tail = (row(lq1), row(lk1), row(lq2), row(lk2), col(subln))
    b_common = dict(
        grid=(b, B_HEADS, nq),
        out_specs=pl.BlockSpec((1, ATTN_TQ, B_V_DIM), lambda bi, hd, qi: (bi, qi, hd)),
        out_shape=jax.ShapeDtypeStruct((b, s, B_WIDTH), BF16),
        compiler_params=_params("parallel", "parallel", "arbitrary"),
    )

    def attn_b_fast():
        kmax = jnp.broadcast_to(kmax_b[:, :, :, None, None], (b, B_HEADS, 2, 1, ATTN_TQ))
        return pl.pallas_call(functools.partial(_attn_b_fast, lambda_init),
                              in_specs=b_specs + [b_kmax_spec] + b_tail_specs,
                              scratch_shapes=fast_scratch(2, B_V_DIM), name="attn_b_fast",
                              **b_common)(qtb, kb, vtb, kmax, *b_tail)

    def attn_b_general():
        return pl.pallas_call(functools.partial(_attn_b_general, lambda_init),
                              in_specs=b_specs + b_tail_specs,
                              scratch_shapes=general_scratch(2, B_V_DIM), name="attn_b",
                              **b_common)(qtb, kb, vtb, *b_tail)

    ob = lax.cond(fast_b, attn_b_fast, attn_b_general)

    tok_spec = lambda width: pl.BlockSpec((DENSE_TM, width), lambda ti: (ti, 0))
    x_flat = x.reshape(n_tok, d)
    x1 = pl.pallas_call(
        _merge_kernel,
        grid=(n_tok // DENSE_TM,),
        in_specs=[
            tok_spec(d), tok_spec(A_WIDTH), tok_spec(B_WIDTH),
            _const_spec((1, d)),
            _const_spec((d, 2 * d)),
            _const_spec((A_WIDTH, d)),
            _const_spec((B_WIDTH, d)),
            _const_spec((d, d)),
        ],
        out_specs=tok_spec(d),
        out_shape=jax.ShapeDtypeStruct((n_tok, d), F32),
        compiler_params=_params("parallel"),
        name="merge",
    )(x_flat, oa.reshape(n_tok, A_WIDTH), ob.reshape(n_tok, B_WIDTH), row(norm_mix),
      w_in[:, OFF_GATES:].astype(BF16), w_proj_a.astype(BF16), w_proj_b.astype(BF16),
      w_out.astype(BF16))

    x2 = pl.pallas_call(
        functools.partial(_ffn_kernel, final),
        grid=(n_tok // DENSE_TM,),
        in_specs=[
            tok_spec(d),
            _const_spec((1, d)),
            _const_spec((d, d_ff)),
            _const_spec((d, d_ff)),
            _const_spec((d_ff, d)),
            _const_spec((1, d)),
        ],
        out_specs=tok_spec(d),
        out_shape=jax.ShapeDtypeStruct((n_tok, d), F32),
        compiler_params=_params("parallel"),
        name="ffn",
    )(x1, row(norm_ffn), w_gate.astype(BF16), w_up.astype(BF16), w_down.astype(BF16),
      row(norm_final))
    return x2.reshape(b, s, d)


def kernel(x, norm_mix, w_in, q_norm_a, k_norm_a, lambda_q1, lambda_k1, lambda_q2, lambda_k2,
           subln_b, w_proj_a, w_proj_b, w_out, norm_ffn, w_gate_ffn, w_up_ffn, w_down_ffn,
           norm_final):
    depth = norm_mix.shape[0]
    tables = _angle_tables(x.shape[1])
    for l in range(depth):
        lambda_init = 0.8 - 0.6 * math.exp(-0.3 * l)
        x = _layer(x, lambda_init, l == depth - 1, norm_mix[l], w_in[l], q_norm_a[l], k_norm_a[l],
                   lambda_q1[l], lambda_k1[l], lambda_q2[l], lambda_k2[l], subln_b[l],
                   w_proj_a[l], w_proj_b[l], w_out[l], norm_ffn[l], w_gate_ffn[l], w_up_ffn[l],
                   w_down_ffn[l], norm_final, tables)
    return x
```

```python
import functools
import math

import jax
import jax.numpy as jnp
from jax import lax
from jax.experimental import pallas as pl
from jax.experimental.pallas import tpu as pltpu

F32 = jnp.float32
BF16 = jnp.bfloat16

GRID_W = 64
HEAD_DIM = 64
HALF = HEAD_DIM // 2
A_Q_HEADS = 8
A_KV_HEADS = 2
A_GROUP = A_Q_HEADS // A_KV_HEADS
A_WIDTH = A_Q_HEADS * HEAD_DIM
B_HEADS = 4
B_V_DIM = 2 * HEAD_DIM
B_WIDTH = B_HEADS * B_V_DIM
ROPE_THETA = 10000.0
AXIAL_THETA = 10000.0
NORM_EPS = 1e-6
QK_SCALE = math.log2(math.e) / math.sqrt(HEAD_DIM)
A_STREAMS = 2
SCORE_LAG = 2
SCORE_SLOTS = 3

OFF_AQ = 0
OFF_AK = OFF_AQ + A_WIDTH
OFF_AV = OFF_AK + A_KV_HEADS * HEAD_DIM
OFF_BQ = OFF_AV + A_KV_HEADS * HEAD_DIM
OFF_BK = OFF_BQ + B_HEADS * 2 * HEAD_DIM
OFF_BV = OFF_BK + B_HEADS * 2 * HEAD_DIM
OFF_GATES = OFF_BV + B_WIDTH
SLAB = 2 * HEAD_DIM
FAST_BOUND = 40.0

PREP_TM = 512
ATTN_TQ = 512
FAST_KC = 4096
FAST_SB = 256
FAST_SKEW = 4
DENSE_TM = 512
VMEM_LIMIT = 56 * 1024 * 1024


def _rms(x, axis):
    return x * lax.rsqrt(jnp.mean(x * x, axis=axis, keepdims=True) + NORM_EPS)


def _rope_t(xt, cos, sin):
    x1, x2 = xt[:HALF], xt[HALF:]
    return jnp.concatenate([x1 * cos - x2 * sin, x2 * cos + x1 * sin], axis=0)


def _prep_kernel(x_ref, nw_ref, w_ref, gq_ref, gk_ref, ca_ref, sa_ref, cb_ref, sb_ref,
                 qta_ref, ka_ref, vta_ref, qtb_ref, kb_ref, vtb_ref, kn_ref):
    x = x_ref[0]
    h = _rms(x, -1) * nw_ref[...]
    z = jnp.dot(h.astype(BF16), w_ref[...], preferred_element_type=F32)
    tm = z.shape[0]
    ca, sa, cb, sb = ca_ref[...], sa_ref[...], cb_ref[...], sb_ref[...]
    gq, gk = gq_ref[...], gk_ref[...]
    pad = jnp.zeros((SLAB - HEAD_DIM - 8, tm), F32)
    one_row = (lax.broadcasted_iota(jnp.int32, (8, tm), 0) == 0).astype(F32)

    def norm(v):
        return jnp.sqrt(jnp.sum(v * v, axis=0, keepdims=True))

    def q_block(q):
        return jnp.concatenate([q, jnp.broadcast_to(norm(q), (8, tm)), pad], axis=0).astype(BF16)

    def k_slab(k):
        return jnp.concatenate([k, one_row, pad], axis=0).T.astype(BF16)

    aqt = z[:, OFF_AQ:OFF_AK].T
    for hd in range(A_Q_HEADS):
        q = aqt[hd * HEAD_DIM:(hd + 1) * HEAD_DIM]
        qta_ref[0, hd] = q_block(_rope_t(_rms(q, 0) * gq, ca, sa) * QK_SCALE)
    akt = z[:, OFF_AK:OFF_AV].T
    k_norms = []
    for g in range(A_KV_HEADS):
        k = _rope_t(_rms(akt[g * HEAD_DIM:(g + 1) * HEAD_DIM], 0) * gk, ca, sa)
        ka_ref[0, g] = k_slab(k)
        k_norms.append(norm(k))
    k_norms.append(jnp.zeros((8 - A_KV_HEADS, tm), F32))
    vta_ref[0, 0] = z[:, OFF_AV:OFF_BQ].T.astype(BF16)

    bqt = z[:, OFF_BQ:OFF_BK].T
    bkt = z[:, OFF_BK:OFF_BV].T
    for hd in range(B_HEADS):
        for c in range(2):
            r0 = (hd * 2 + c) * HEAD_DIM
            qtb_ref[0, hd, c] = q_block(_rope_t(bqt[r0:r0 + HEAD_DIM], cb, sb) * QK_SCALE)
            k = _rope_t(bkt[r0:r0 + HEAD_DIM], cb, sb)
            kb_ref[0, hd, c] = k_slab(k)
            k_norms.append(norm(k))
    vtb_ref[0, 0] = z[:, OFF_BV:OFF_GATES].T.astype(BF16)
    kn_ref[0] = jnp.concatenate(k_norms, axis=0)


def _plain_query(qt):
    qf = qt.astype(F32)
    rows = lax.broadcasted_iota(jnp.int32, qf.shape, 0)
    return jnp.where(rows < HEAD_DIM, qf, 0.0).astype(BF16)


def _shifted_query(qt, kmax):
    qf = qt.astype(F32)
    rows = lax.broadcasted_iota(jnp.int32, qf.shape, 0)
    shifted = jnp.where(rows == HEAD_DIM, -(qf * kmax), jnp.where(rows < HEAD_DIM, qf, 0.0))
    return shifted.astype(BF16)


def _flash_fast(q_list, k_of, vt_ref, l_ref, acc_ref):
    n = len(q_list)
    vb = vt_ref.shape[3]
    sb = FAST_SB
    l_ref[...] = jnp.zeros(l_ref.shape, F32)
    acc_ref[...] = jnp.zeros(acc_ref.shape, F32)

    def body(i, carry):
        off = pl.multiple_of(i * FAST_KC, FAST_KC)
        l = [l_ref[j] for j in range(n)]
        acc = [acc_ref[j] for j in range(n)]
        items = [(g, j) for g in range(FAST_KC // sb) for j in range(n)]
        scores = {}

        def consume(g, j):
            k0 = g * sb
            p = jnp.exp2(scores.pop((g, j)))
            l[j] = l[j] + jnp.sum(p, axis=0, keepdims=True)
            vblk = vt_ref[0, i * (FAST_KC // vb) + k0 // vb, :, k0 % vb:k0 % vb + sb]
            acc[j] = acc[j] + jnp.dot(vblk, p.astype(BF16), preferred_element_type=F32)

        for t, (g, j) in enumerate(items):
            scores[(g, j)] = jnp.dot(k_of(j, off + g * sb, sb), q_list[j],
                                     preferred_element_type=F32)
            if t >= FAST_SKEW:
                consume(*items[t - FAST_SKEW])
        for g, j in items[len(items) - FAST_SKEW:]:
            consume(g, j)
        for j in range(n):
            l_ref[j], acc_ref[j] = l[j], acc[j]
        return carry

    lax.fori_loop(0, vt_ref.shape[1] * vb // FAST_KC, body, 0)


def _flash_streams(q_list, k_of, vt_ref, s_buf, mx_buf, m_ref, l_ref, acc_ref):
    n = len(q_list)
    n_chunks, kc = vt_ref.shape[1], vt_ref.shape[3]
    m_ref[...] = jnp.full(m_ref.shape, -jnp.inf, F32)
    l_ref[...] = jnp.zeros(l_ref.shape, F32)
    acc_ref[...] = jnp.zeros(acc_ref.shape, F32)

    def scores(c, slot):
        off = c * kc if isinstance(c, int) else pl.multiple_of(c * kc, kc)
        for j in range(n):
            s = jnp.dot(k_of(j, off, kc), q_list[j], preferred_element_type=F32)
            s_buf[slot, j] = s
            mx_buf[slot, j] = jnp.max(s, axis=0, keepdims=True)

    def softmax_pv(c, slot):
        vblk = vt_ref[0, c]
        for j in range(n):
            m_old = m_ref[j]
            m_new = jnp.maximum(m_old, mx_buf[slot, j])
            alpha = jnp.exp2(m_old - m_new)
            p = jnp.exp2(s_buf[slot, j] - m_new)
            l_ref[j] = alpha * l_ref[j] + jnp.sum(p, axis=0, keepdims=True)
            acc_ref[j] = alpha * acc_ref[j] + jnp.dot(vblk, p.astype(BF16),
                                                      preferred_element_type=F32)
            m_ref[j] = m_new

    n_slots = s_buf.shape[0]

    def stage(c, u):
        if not isinstance(c, int) or c + SCORE_LAG < n_chunks:
            scores(c + SCORE_LAG, (u + SCORE_LAG) % n_slots)
        softmax_pv(c, u)

    for c in range(SCORE_LAG):
        scores(c, c % n_slots)
    n_loop = (n_chunks - SCORE_LAG) // n_slots

    def body(i, carry):
        for u in range(n_slots):
            stage(i * n_slots + u, u)
        return carry

    lax.fori_loop(0, n_loop, body, 0)
    for c in range(n_loop * n_slots, n_chunks):
        stage(c, c % n_slots)


def _finish_a(o_ref, l_ref, acc_ref):
    outs = [acc_ref[j] * (1.0 / l_ref[j]) for j in range(A_STREAMS)]
    o_ref[0] = jnp.concatenate(outs, axis=0).T.astype(BF16)


def _finish_b(lambda_init, lam_refs, sub_ref, o_ref, l_ref, acc_ref):
    lq1_ref, lk1_ref, lq2_ref, lk2_ref = lam_refs
    lam = (jnp.exp(jnp.sum(lq1_ref[...] * lk1_ref[...], axis=-1, keepdims=True))
           - jnp.exp(jnp.sum(lq2_ref[...] * lk2_ref[...], axis=-1, keepdims=True))
           + lambda_init)
    o = acc_ref[0] * (1.0 / l_ref[0]) - lam * (acc_ref[1] * (1.0 / l_ref[1]))
    o = _rms(o, 0) * sub_ref[...] * (1.0 - lambda_init)
    o_ref[0] = o.T.astype(BF16)


def _k_of_a(k_ref):
    return lambda j, off, size: k_ref[0, 0, pl.ds(off, size), :]


def _k_of_b(k_ref):
    return lambda j, off, size: k_ref[0, 0, j, pl.ds(off, size), :]


def _attn_a_general(qt_ref, k_ref, vt_ref, o_ref, s_buf, mx_buf, m_ref, l_ref, acc_ref):
    q_list = [_plain_query(qt_ref[0, j]) for j in range(A_STREAMS)]
    _flash_streams(q_list, _k_of_a(k_ref), vt_ref, s_buf, mx_buf, m_ref, l_ref, acc_ref)
    _finish_a(o_ref, l_ref, acc_ref)


def _attn_a_fast(qt_ref, k_ref, vt_ref, kmax_ref, o_ref, l_ref, acc_ref):
    q_list = [_shifted_query(qt_ref[0, j], kmax_ref[0, 0]) for j in range(A_STREAMS)]
    _flash_fast(q_list, _k_of_a(k_ref), vt_ref, l_ref, acc_ref)
    _finish_a(o_ref, l_ref, acc_ref)


def _attn_b_general(lambda_init, qt_ref, k_ref, vt_ref, lq1_ref, lk1_ref, lq2_ref, lk2_ref,
                    sub_ref, o_ref, s_buf, mx_buf, m_ref, l_ref, acc_ref):
    q_list = [_plain_query(qt_ref[0, 0, c]) for c in range(2)]
    _flash_streams(q_list, _k_of_b(k_ref), vt_ref, s_buf, mx_buf, m_ref, l_ref, acc_ref)
    _finish_b(lambda_init, (lq1_ref, lk1_ref, lq2_ref, lk2_ref), sub_ref, o_ref, l_ref, acc_ref)


def _attn_b_fast(lambda_init, qt_ref, k_ref, vt_ref, kmax_ref, lq1_ref, lk1_ref, lq2_ref, lk2_ref,
                 sub_ref, o_ref, l_ref, acc_ref):
    q_list = [_shifted_query(qt_ref[0, 0, c], kmax_ref[0, 0, c]) for c in range(2)]
    _flash_fast(q_list, _k_of_b(k_ref), vt_ref, l_ref, acc_ref)
    _finish_b(lambda_init, (lq1_ref, lk1_ref, lq2_ref, lk2_ref), sub_ref, o_ref, l_ref, acc_ref)


def _merge_kernel(x_ref, oa_ref, ob_ref, nw_ref, wg_ref, wa_ref, wb_ref, wo_ref, x1_ref):
    x = x_ref[...]
    h = (_rms(x, -1) * nw_ref[...]).astype(BF16)
    gates = jnp.dot(h, wg_ref[...], preferred_element_type=F32)
    d = x.shape[-1]
    ya = jnp.dot(oa_ref[...], wa_ref[...], preferred_element_type=F32)
    yb = jnp.dot(ob_ref[...], wb_ref[...], preferred_element_type=F32)
    y = jax.nn.sigmoid(gates[:, :d]) * ya + jax.nn.sigmoid(gates[:, d:]) * yb
    x1_ref[...] = x + jnp.dot(y.astype(BF16), wo_ref[...], preferred_element_type=F32)


def _ffn_kernel(final, x_ref, nw_ref, wg_ref, wu_ref, wd_ref, nf_ref, o_ref):
    x = x_ref[...]
    h = (_rms(x, -1) * nw_ref[...]).astype(BF16)
    gate = jnp.dot(h, wg_ref[...], preferred_element_type=F32)
    up = jnp.dot(h, wu_ref[...], preferred_element_type=F32)
    act = (jax.nn.silu(gate) * up).astype(BF16)
    x2 = x + jnp.dot(act, wd_ref[...], preferred_element_type=F32)
    if final:
        x2 = _rms(x2, -1) * nf_ref[...]
    o_ref[...] = x2


def _const_spec(shape):
    nd = len(shape)
    return pl.BlockSpec(shape, lambda *_: (0,) * nd, pipeline_mode=pl.Buffered(1))


def _rope_angles_t(pos, dim, theta):
    inv_freq = theta ** (-jnp.arange(0, dim, 2, dtype=F32) / dim)
    return (pos[:, None] * inv_freq[None, :]).T


def _angle_tables(seq_len):
    rows = seq_len // GRID_W
    row = jnp.broadcast_to(jnp.arange(rows, dtype=F32)[:, None], (rows, GRID_W)).reshape(-1)
    col = jnp.broadcast_to(jnp.arange(GRID_W, dtype=F32)[None, :], (rows, GRID_W)).reshape(-1)
    ang_a = jnp.concatenate([_rope_angles_t(row, HALF, AXIAL_THETA),
                             _rope_angles_t(col, HALF, AXIAL_THETA)], axis=0)
    ang_b = _rope_angles_t(jnp.arange(seq_len, dtype=F32), HEAD_DIM, ROPE_THETA)
    return jnp.cos(ang_a), jnp.sin(ang_a), jnp.cos(ang_b), jnp.sin(ang_b)


def _params(*sem):
    return pltpu.CompilerParams(dimension_semantics=sem, vmem_limit_bytes=VMEM_LIMIT)


def _layer(x, lambda_init, final, norm_mix, w_in, q_norm_a, k_norm_a, lq1, lk1, lq2, lk2, subln,
           w_proj_a, w_proj_b, w_out, norm_ffn, w_gate, w_up, w_down, norm_final, tables):
    b, s, d = x.shape
    d_ff = w_gate.shape[-1]
    n_tok = b * s
    n_chunks = s // PREP_TM
    ca, sa, cb, sb = tables
    row = lambda v: v.reshape(1, -1).astype(F32)
    col = lambda v: v.reshape(-1, 1).astype(F32)

    tab_spec = pl.BlockSpec((HALF, PREP_TM), lambda bi, ti: (0, ti))
    qta, ka, vta, qtb, kb, vtb, kn = pl.pallas_call(
        _prep_kernel,
        grid=(b, n_chunks),
        in_specs=[
            pl.BlockSpec((1, PREP_TM, d), lambda bi, ti: (bi, ti, 0)),
            _const_spec((1, d)),
            _const_spec((d, OFF_GATES)),
            _const_spec((HEAD_DIM, 1)),
            _const_spec((HEAD_DIM, 1)),
            tab_spec, tab_spec, tab_spec, tab_spec,
        ],
        out_specs=[
            pl.BlockSpec((1, A_Q_HEADS, SLAB, PREP_TM), lambda bi, ti: (bi, 0, 0, ti)),
            pl.BlockSpec((1, A_KV_HEADS, PREP_TM, SLAB), lambda bi, ti: (bi, 0, ti, 0)),
            pl.BlockSpec((1, 1, A_KV_HEADS * HEAD_DIM, PREP_TM), lambda bi, ti: (bi, ti, 0, 0)),
            pl.BlockSpec((1, B_HEADS, 2, SLAB, PREP_TM), lambda bi, ti: (bi, 0, 0, 0, ti)),
            pl.BlockSpec((1, B_HEADS, 2, PREP_TM, SLAB), lambda bi, ti: (bi, 0, 0, ti, 0)),
            pl.BlockSpec((1, 1, B_WIDTH, PREP_TM), lambda bi, ti: (bi, ti, 0, 0)),
            pl.BlockSpec((1, 16, PREP_TM), lambda bi, ti: (bi, 0, ti)),
        ],
        out_shape=[
            jax.ShapeDtypeStruct((b, A_Q_HEADS, SLAB, s), BF16),
            jax.ShapeDtypeStruct((b, A_KV_HEADS, s, SLAB), BF16),
            jax.ShapeDtypeStruct((b, n_chunks, A_KV_HEADS * HEAD_DIM, PREP_TM), BF16),
            jax.ShapeDtypeStruct((b, B_HEADS, 2, SLAB, s), BF16),
            jax.ShapeDtypeStruct((b, B_HEADS, 2, s, SLAB), BF16),
            jax.ShapeDtypeStruct((b, n_chunks, B_WIDTH, PREP_TM), BF16),
            jax.ShapeDtypeStruct((b, 16, s), F32),
        ],
        compiler_params=_params("parallel", "parallel"),
        name="prep",
    )(x, row(norm_mix), w_in[:, :OFF_GATES].astype(BF16), col(q_norm_a), col(k_norm_a),
      ca, sa, cb, sb)

    nq = s // ATTN_TQ
    steps_per_group = A_GROUP // A_STREAMS
    kmax_a = jnp.max(kn[:, :A_KV_HEADS], axis=-1)
    kmax_b = jnp.max(kn[:, 8:], axis=-1).reshape(b, B_HEADS, 2)
    qmax_a = jnp.max(qta[:, :, HEAD_DIM, :].astype(F32), axis=-1)
    qmax_b = jnp.max(qtb[:, :, :, HEAD_DIM, :].astype(F32), axis=-1)
    fast_a = jnp.max(qmax_a * jnp.repeat(kmax_a, A_GROUP, axis=1)) < FAST_BOUND
    fast_b = jnp.max(qmax_b * kmax_b) < FAST_BOUND

    def general_scratch(n, dv):
        return [
            pltpu.VMEM((SCORE_SLOTS, n, PREP_TM, ATTN_TQ), F32),
            pltpu.VMEM((SCORE_SLOTS, n, 1, ATTN_TQ), F32),
            pltpu.VMEM((n, 1, ATTN_TQ), F32),
            pltpu.VMEM((n, 1, ATTN_TQ), F32),
            pltpu.VMEM((n, dv, ATTN_TQ), F32),
        ]

    def fast_scratch(n, dv):
        return [pltpu.VMEM((n, 1, ATTN_TQ), F32), pltpu.VMEM((n, dv, ATTN_TQ), F32)]

    a_specs = [
        pl.BlockSpec((1, A_STREAMS, SLAB, ATTN_TQ), lambda bi, hp, qi: (bi, hp, 0, qi)),
        pl.BlockSpec((1, 1, s, SLAB), lambda bi, hp, qi: (bi, hp // steps_per_group, 0, 0)),
        pl.BlockSpec((1, n_chunks, HEAD_DIM, PREP_TM),
                     lambda bi, hp, qi: (bi, 0, hp // steps_per_group, 0)),
    ]
    a_kmax_spec = pl.BlockSpec((1, 1, 1, ATTN_TQ), lambda bi, hp, qi: (bi, hp // steps_per_group, 0, 0))
    a_common = dict(
        grid=(b, A_Q_HEADS // A_STREAMS, nq),
        out_specs=pl.BlockSpec((1, ATTN_TQ, A_STREAMS * HEAD_DIM), lambda bi, hp, qi: (bi, qi, hp)),
        out_shape=jax.ShapeDtypeStruct((b, s, A_WIDTH), BF16),
        compiler_params=_params("parallel", "parallel", "arbitrary"),
    )

    def attn_a_fast():
        kmax = jnp.broadcast_to(kmax_a[:, :, None, None], (b, A_KV_HEADS, 1, ATTN_TQ))
        return pl.pallas_call(_attn_a_fast, in_specs=a_specs + [a_kmax_spec],
                              scratch_shapes=fast_scratch(A_STREAMS, HEAD_DIM), name="attn_a_fast",
                              **a_common)(qta, ka, vta, kmax)

    def attn_a_general():
        return pl.pallas_call(_attn_a_general, in_specs=a_specs,
                              scratch_shapes=general_scratch(A_STREAMS, HEAD_DIM), name="attn_a",
                              **a_common)(qta, ka, vta)

    oa = lax.cond(fast_a, attn_a_fast, attn_a_general)

    lam_spec = _const_spec((1, HEAD_DIM))
    b_specs = [
        pl.BlockSpec((1, 1, 2, SLAB, ATTN_TQ), lambda bi, hd, qi: (bi, hd, 0, 0, qi)),
        pl.BlockSpec((1, 1, 2, s, SLAB), lambda bi, hd, qi: (bi, hd, 0, 0, 0)),
        pl.BlockSpec((1, n_chunks, B_V_DIM, PREP_TM), lambda bi, hd, qi: (bi, 0, hd, 0)),
    ]
    b_kmax_spec = pl.BlockSpec((1, 1, 2, 1, ATTN_TQ), lambda bi, hd, qi: (bi, hd, 0, 0, 0))
    b_tail_specs = [lam_spec, lam_spec, lam_spec, lam_spec, _const_spec((B_V_DIM, 1))]
    b_tail = (row(lq1), row(lk1), row(lq2), row(lk2), col(subln))
    b_common = dict(
        grid=(b, B_HEADS, nq),
        out_specs=pl.BlockSpec((1, ATTN_TQ, B_V_DIM), lambda bi, hd, qi: (bi, qi, hd)),
        out_shape=jax.ShapeDtypeStruct((b, s, B_WIDTH), BF16),
        compiler_params=_params("parallel", "parallel", "arbitrary"),
    )

    def attn_b_fast():
        kmax = jnp.broadcast_to(kmax_b[:, :, :, None, None], (b, B_HEADS, 2, 1, ATTN_TQ))
        return pl.pallas_call(functools.partial(_attn_b_fast, lambda_init),
                              in_specs=b_specs + [b_kmax_spec] + b_tail_specs,
                              scratch_shapes=fast_scratch(2, B_V_DIM), name="attn_b_fast",
                              **b_common)(qtb, kb, vtb, kmax, *b_tail)

    def attn_b_general():
        return pl.pallas_call(functools.partial(_attn_b_general, lambda_init),
                              in_specs=b_specs + b_tail_specs,
                              scratch_shapes=general_scratch(2, B_V_DIM), name="attn_b",
                              **b_common)(qtb, kb, vtb, *b_tail)

    ob = lax.cond(fast_b, attn_b_fast, attn_b_general)

    tok_spec = lambda width: pl.BlockSpec((DENSE_TM, width), lambda ti: (ti, 0))
    x_flat = x.reshape(n_tok, d)
    x1 = pl.pallas_call(
        _merge_kernel,
        grid=(n_tok // DENSE_TM,),
        in_specs=[
            tok_spec(d), tok_spec(A_WIDTH), tok_spec(B_WIDTH),
            _const_spec((1, d)),
            _const_spec((d, 2 * d)),
            _const_spec((A_WIDTH, d)),
            _const_spec((B_WIDTH, d)),
            _const_spec((d, d)),
        ],
        out_specs=tok_spec(d),
        out_shape=jax.ShapeDtypeStruct((n_tok, d), F32),
        compiler_params=_params("parallel"),
        name="merge",
    )(x_flat, oa.reshape(n_tok, A_WIDTH), ob.reshape(n_tok, B_WIDTH), row(norm_mix),
      w_in[:, OFF_GATES:].astype(BF16), w_proj_a.astype(BF16), w_proj_b.astype(BF16),
      w_out.astype(BF16))

    x2 = pl.pallas_call(
        functools.partial(_ffn_kernel, final),
        grid=(n_tok // DENSE_TM,),
        in_specs=[
            tok_spec(d),
            _const_spec((1, d)),
            _const_spec((d, d_ff)),
            _const_spec((d, d_ff)),
            _const_spec((d_ff, d)),
            _const_spec((1, d)),
        ],
        out_specs=tok_spec(d),
        out_shape=jax.ShapeDtypeStruct((n_tok, d), F32),
        compiler_params=_params("parallel"),
        name="ffn",
    )(x1, row(norm_ffn), w_gate.astype(BF16), w_up.astype(BF16), w_down.astype(BF16),
      row(norm_final))
    return x2.reshape(b, s, d)


def kernel(x, norm_mix, w_in, q_norm_a, k_norm_a, lambda_q1, lambda_k1, lambda_q2, lambda_k2,
           subln_b, w_proj_a, w_proj_b, w_out, norm_ffn, w_gate_ffn, w_up_ffn, w_down_ffn,
           norm_final):
    depth = norm_mix.shape[0]
    tables = _angle_tables(x.shape[1])
    for l in range(depth):
        lambda_init = 0.8 - 0.6 * math.exp(-0.3 * l)
        x = _layer(x, lambda_init, l == depth - 1, norm_mix[l], w_in[l], q_norm_a[l], k_norm_a[l],
                   lambda_q1[l], lambda_k1[l], lambda_q2[l], lambda_k2[l], subln_b[l],
                   w_proj_a[l], w_proj_b[l], w_out[l], norm_ffn[l], w_gate_ffn[l], w_up_ffn[l],
                   w_down_ffn[l], norm_final, tables)
    return x
```

```python
import functools
import math

import jax
import jax.numpy as jnp
from jax import lax
from jax.experimental import pallas as pl
from jax.experimental.pallas import tpu as pltpu

F32 = jnp.float32
BF16 = jnp.bfloat16

GRID_W = 64
HEAD_DIM = 64
HALF = HEAD_DIM // 2
A_Q_HEADS = 8
A_KV_HEADS = 2
A_GROUP = A_Q_HEADS // A_KV_HEADS
A_WIDTH = A_Q_HEADS * HEAD_DIM
B_HEADS = 4
B_V_DIM = 2 * HEAD_DIM
B_WIDTH = B_HEADS * B_V_DIM
ROPE_THETA = 10000.0
AXIAL_THETA = 10000.0
NORM_EPS = 1e-6
QK_SCALE = math.log2(math.e) / math.sqrt(HEAD_DIM)
A_STREAMS = 4
B_STEP_HEADS = 2
SCORE_LAG = 2
SCORE_SLOTS = 3

OFF_AQ = 0
OFF_AK = OFF_AQ + A_WIDTH
OFF_AV = OFF_AK + A_KV_HEADS * HEAD_DIM
OFF_BQ = OFF_AV + A_KV_HEADS * HEAD_DIM
OFF_BK = OFF_BQ + B_HEADS * 2 * HEAD_DIM
OFF_BV = OFF_BK + B_HEADS * 2 * HEAD_DIM
OFF_GATES = OFF_BV + B_WIDTH
SLAB = 2 * HEAD_DIM
FAST_BOUND = 40.0

PREP_TM = 512
ATTN_TQ = 512
FAST_KC = 4096
FAST_ORDER_A = (256, 4)
FAST_ORDER_B = (4096, 2)
DENSE_TM = 512
VMEM_LIMIT = 56 * 1024 * 1024


def _rms(x, axis):
    return x * lax.rsqrt(jnp.mean(x * x, axis=axis, keepdims=True) + NORM_EPS)


def _rope_t(xt, cos, sin):
    x1, x2 = xt[:HALF], xt[HALF:]
    return jnp.concatenate([x1 * cos - x2 * sin, x2 * cos + x1 * sin], axis=0)


def _prep_kernel(x_ref, nw_ref, w_ref, gq_ref, gk_ref, ca_ref, sa_ref, cb_ref, sb_ref,
                 qta_ref, ka_ref, vta_ref, qtb_ref, kb_ref, vtb_ref, kn_ref):
    x = x_ref[0]
    h = _rms(x, -1) * nw_ref[...]
    z = jnp.dot(h.astype(BF16), w_ref[...], preferred_element_type=F32)
    tm = z.shape[0]
    ca, sa, cb, sb = ca_ref[...], sa_ref[...], cb_ref[...], sb_ref[...]
    gq, gk = gq_ref[...], gk_ref[...]
    pad = jnp.zeros((SLAB - HEAD_DIM - 8, tm), F32)
    one_row = (lax.broadcasted_iota(jnp.int32, (8, tm), 0) == 0).astype(F32)

    def norm(v):
        return jnp.sqrt(jnp.sum(v * v, axis=0, keepdims=True))

    def q_block(q):
        return jnp.concatenate([q, jnp.broadcast_to(norm(q), (8, tm)), pad], axis=0).astype(BF16)

    def k_slab(k):
        return jnp.concatenate([k, one_row, pad], axis=0).T.astype(BF16)

    aqt = z[:, OFF_AQ:OFF_AK].T
    for hd in range(A_Q_HEADS):
        q = aqt[hd * HEAD_DIM:(hd + 1) * HEAD_DIM]
        qta_ref[0, hd] = q_block(_rope_t(_rms(q, 0) * gq, ca, sa) * QK_SCALE)
    akt = z[:, OFF_AK:OFF_AV].T
    k_norms = []
    for g in range(A_KV_HEADS):
        k = _rope_t(_rms(akt[g * HEAD_DIM:(g + 1) * HEAD_DIM], 0) * gk, ca, sa)
        ka_ref[0, g] = k_slab(k)
        k_norms.append(norm(k))
    k_norms.append(jnp.zeros((8 - A_KV_HEADS, tm), F32))
    vta_ref[0, 0] = z[:, OFF_AV:OFF_BQ].T.astype(BF16)

    bqt = z[:, OFF_BQ:OFF_BK].T
    bkt = z[:, OFF_BK:OFF_BV].T
    for hd in range(B_HEADS):
        for c in range(2):
            r0 = (hd * 2 + c) * HEAD_DIM
            qtb_ref[0, hd, c] = q_block(_rope_t(bqt[r0:r0 + HEAD_DIM], cb, sb) * QK_SCALE)
            k = _rope_t(bkt[r0:r0 + HEAD_DIM], cb, sb)
            kb_ref[0, hd, c] = k_slab(k)
            k_norms.append(norm(k))
    vtb_ref[0, 0] = z[:, OFF_BV:OFF_GATES].T.astype(BF16)
    kn_ref[0] = jnp.concatenate(k_norms, axis=0)


def _plain_query(qt):
    qf = qt.astype(F32)
    rows = lax.broadcasted_iota(jnp.int32, qf.shape, 0)
    return jnp.where(rows < HEAD_DIM, qf, 0.0).astype(BF16)


def _shifted_query(qt, kmax):
    qf = qt.astype(F32)
    rows = lax.broadcasted_iota(jnp.int32, qf.shape, 0)
    shifted = jnp.where(rows == HEAD_DIM, -(qf * kmax), jnp.where(rows < HEAD_DIM, qf, 0.0))
    return shifted.astype(BF16)


def _flash_fast(q_list, k_of, vt_ref, v_rows, l_ref, acc_ref, sb, skew):
    n = len(q_list)
    vb = vt_ref.shape[3]
    pb = min(sb, vb)
    l_ref[...] = jnp.zeros(l_ref.shape, F32)
    acc_ref[...] = jnp.zeros(acc_ref.shape, F32)

    def body(i, carry):
        off = pl.multiple_of(i * FAST_KC, FAST_KC)
        l = [l_ref[j] for j in range(n)]
        acc = [acc_ref[j] for j in range(n)]
        items = [(g, j) for g in range(FAST_KC // sb) for j in range(n)]
        scores = {}

        def consume(g, j):
            p = jnp.exp2(scores.pop((g, j)))
            l[j] = l[j] + jnp.sum(p, axis=0, keepdims=True)
            p = p.astype(BF16)
            for u in range(sb // pb):
                k0 = g * sb + u * pb
                vblk = vt_ref[0, i * (FAST_KC // vb) + k0 // vb, v_rows(j), k0 % vb:k0 % vb + pb]
                acc[j] = acc[j] + jnp.dot(vblk, p[u * pb:(u + 1) * pb], preferred_element_type=F32)

        for t, (g, j) in enumerate(items):
            scores[(g, j)] = jnp.dot(k_of(j, off + g * sb, sb), q_list[j],
                                     preferred_element_type=F32)
            if t >= skew:
                consume(*items[t - skew])
        for g, j in items[max(len(items) - skew, 0):]:
            consume(g, j)
        for j in range(n):
            l_ref[j], acc_ref[j] = l[j], acc[j]
        return carry

    lax.fori_loop(0, vt_ref.shape[1] * vb // FAST_KC, body, 0)


def _flash_streams(q_list, k_of, vt_ref, v_rows, s_buf, mx_buf, m_ref, l_ref, acc_ref):
    n = len(q_list)
    n_chunks, kc = vt_ref.shape[1], vt_ref.shape[3]
    m_ref[...] = jnp.full(m_ref.shape, -jnp.inf, F32)
    l_ref[...] = jnp.zeros(l_ref.shape, F32)
    acc_ref[...] = jnp.zeros(acc_ref.shape, F32)

    def scores(c, slot):
        off = c * kc if isinstance(c, int) else pl.multiple_of(c * kc, kc)
        for j in range(n):
            s = jnp.dot(k_of(j, off, kc), q_list[j], preferred_element_type=F32)
            s_buf[slot, j] = s
            mx_buf[slot, j] = jnp.max(s, axis=0, keepdims=True)

    def softmax_pv(c, slot):
        for j in range(n):
            vblk = vt_ref[0, c, v_rows(j)]
            m_old = m_ref[j]
            m_new = jnp.maximum(m_old, mx_buf[slot, j])
            alpha = jnp.exp2(m_old - m_new)
            p = jnp.exp2(s_buf[slot, j] - m_new)
            l_ref[j] = alpha * l_ref[j] + jnp.sum(p, axis=0, keepdims=True)
            acc_ref[j] = alpha * acc_ref[j] + jnp.dot(vblk, p.astype(BF16),
                                                      preferred_element_type=F32)
            m_ref[j] = m_new

    n_slots = s_buf.shape[0]

    def stage(c, u):
        if not isinstance(c, int) or c + SCORE_LAG < n_chunks:
            scores(c + SCORE_LAG, (u + SCORE_LAG) % n_slots)
        softmax_pv(c, u)

    for c in range(SCORE_LAG):
        scores(c, c % n_slots)
    n_loop = (n_chunks - SCORE_LAG) // n_slots

    def body(i, carry):
        for u in range(n_slots):
            stage(i * n_slots + u, u)
        return carry

    lax.fori_loop(0, n_loop, body, 0)
    for c in range(n_loop * n_slots, n_chunks):
        stage(c, c % n_slots)


def _finish_a(o_ref, l_ref, acc_ref):
    outs = [acc_ref[j] * (1.0 / l_ref[j]) for j in range(A_STREAMS)]
    o_ref[0] = jnp.concatenate(outs, axis=0).T.astype(BF16)


def _finish_b(lambda_init, lam_refs, sub_ref, o_ref, l_ref, acc_ref):
    lq1_ref, lk1_ref, lq2_ref, lk2_ref = lam_refs
    lam = (jnp.exp(jnp.sum(lq1_ref[...] * lk1_ref[...], axis=-1, keepdims=True))
           - jnp.exp(jnp.sum(lq2_ref[...] * lk2_ref[...], axis=-1, keepdims=True))
           + lambda_init)
    outs = []
    for hd in range(B_STEP_HEADS):
        j1, j2 = 2 * hd, 2 * hd + 1
        o = acc_ref[j1] * (1.0 / l_ref[j1]) - lam * (acc_ref[j2] * (1.0 / l_ref[j2]))
        outs.append(_rms(o, 0) * sub_ref[...] * (1.0 - lambda_init))
    o_ref[0] = jnp.concatenate(outs, axis=0).T.astype(BF16)


def _all_rows(j):
    return slice(None)


def _b_rows(j):
    return slice((j // 2) * B_V_DIM, (j // 2 + 1) * B_V_DIM)


def _k_of_a(k_ref):
    return lambda j, off, size: k_ref[0, 0, pl.ds(off, size), :]


def _k_of_b(k_ref):
    return lambda j, off, size: k_ref[0, j // 2, j % 2, pl.ds(off, size), :]


def _attn_a_general(qt_ref, k_ref, vt_ref, o_ref, s_buf, mx_buf, m_ref, l_ref, acc_ref):
    q_list = [_plain_query(qt_ref[0, j]) for j in range(A_STREAMS)]
    _flash_streams(q_list, _k_of_a(k_ref), vt_ref, _all_rows, s_buf, mx_buf, m_ref, l_ref, acc_ref)
    _finish_a(o_ref, l_ref, acc_ref)


def _attn_a_fast(qt_ref, k_ref, vt_ref, kmax_ref, o_ref, l_ref, acc_ref):
    q_list = [_shifted_query(qt_ref[0, j], kmax_ref[0, 0]) for j in range(A_STREAMS)]
    _flash_fast(q_list, _k_of_a(k_ref), vt_ref, _all_rows, l_ref, acc_ref, *FAST_ORDER_A)
    _finish_a(o_ref, l_ref, acc_ref)


def _attn_b_general(lambda_init, qt_ref, k_ref, vt_ref, lq1_ref, lk1_ref, lq2_ref, lk2_ref,
                    sub_ref, o_ref, s_buf, mx_buf, m_ref, l_ref, acc_ref):
    q_list = [_plain_query(qt_ref[0, hd, c]) for hd in range(B_STEP_HEADS) for c in range(2)]
    _flash_streams(q_list, _k_of_b(k_ref), vt_ref, _b_rows, s_buf, mx_buf, m_ref, l_ref, acc_ref)
    _finish_b(lambda_init, (lq1_ref, lk1_ref, lq2_ref, lk2_ref), sub_ref, o_ref, l_ref, acc_ref)


def _attn_b_fast(lambda_init, qt_ref, k_ref, vt_ref, kmax_ref, lq1_ref, lk1_ref, lq2_ref, lk2_ref,
                 sub_ref, o_ref, l_ref, acc_ref):
    q_list = [_shifted_query(qt_ref[0, hd, c], kmax_ref[0, hd, c])
              for hd in range(B_STEP_HEADS) for c in range(2)]
    _flash_fast(q_list, _k_of_b(k_ref), vt_ref, _b_rows, l_ref, acc_ref, *FAST_ORDER_B)
    _finish_b(lambda_init, (lq1_ref, lk1_ref, lq2_ref, lk2_ref), sub_ref, o_ref, l_ref, acc_ref)


def _merge_kernel(x_ref, oa_ref, ob_ref, nw_ref, wg_ref, wa_ref, wb_ref, wo_ref, x1_ref):
    x = x_ref[...]
    h = (_rms(x, -1) * nw_ref[...]).astype(BF16)
    gates = jnp.dot(h, wg_ref[...], preferred_element_type=F32)
    d = x.shape[-1]
    ya = jnp.dot(oa_ref[...], wa_ref[...], preferred_element_type=F32)
    yb = jnp.dot(ob_ref[...], wb_ref[...], preferred_element_type=F32)
    y = jax.nn.sigmoid(gates[:, :d]) * ya + jax.nn.sigmoid(gates[:, d:]) * yb
    x1_ref[...] = x + jnp.dot(y.astype(BF16), wo_ref[...], preferred_element_type=F32)


def _ffn_kernel(final, x_ref, nw_ref, wg_ref, wu_ref, wd_ref, nf_ref, o_ref):
    x = x_ref[...]
    h = (_rms(x, -1) * nw_ref[...]).astype(BF16)
    gate = jnp.dot(h, wg_ref[...], preferred_element_type=F32)
    up = jnp.dot(h, wu_ref[...], preferred_element_type=F32)
    act = (jax.nn.silu(gate) * up).astype(BF16)
    x2 = x + jnp.dot(act, wd_ref[...], preferred_element_type=F32)
    if final:
        x2 = _rms(x2, -1) * nf_ref[...]
    o_ref[...] = x2


def _const_spec(shape):
    nd = len(shape)
    return pl.BlockSpec(shape, lambda *_: (0,) * nd, pipeline_mode=pl.Buffered(1))


def _rope_angles_t(pos, dim, theta):
    inv_freq = theta ** (-jnp.arange(0, dim, 2, dtype=F32) / dim)
    return (pos[:, None] * inv_freq[None, :]).T


def _angle_tables(seq_len):
    rows = seq_len // GRID_W
    row = jnp.broadcast_to(jnp.arange(rows, dtype=F32)[:, None], (rows, GRID_W)).reshape(-1)
    col = jnp.broadcast_to(jnp.arange(GRID_W, dtype=F32)[None, :], (rows, GRID_W)).reshape(-1)
    ang_a = jnp.concatenate([_rope_angles_t(row, HALF, AXIAL_THETA),
                             _rope_angles_t(col, HALF, AXIAL_THETA)], axis=0)
    ang_b = _rope_angles_t(jnp.arange(seq_len, dtype=F32), HEAD_DIM, ROPE_THETA)
    return jnp.cos(ang_a), jnp.sin(ang_a), jnp.cos(ang_b), jnp.sin(ang_b)


def _params(*sem):
    return pltpu.CompilerParams(dimension_semantics=sem, vmem_limit_bytes=VMEM_LIMIT)


def _layer(x, lambda_init, final, norm_mix, w_in, q_norm_a, k_norm_a, lq1, lk1, lq2, lk2, subln,
           w_proj_a, w_proj_b, w_out, norm_ffn, w_gate, w_up, w_down, norm_final, tables):
    b, s, d = x.shape
    d_ff = w_gate.shape[-1]
    n_tok = b * s
    n_chunks = s // PREP_TM
    ca, sa, cb, sb = tables
    row = lambda v: v.reshape(1, -1).astype(F32)
    col = lambda v: v.reshape(-1, 1).astype(F32)

    tab_spec = pl.BlockSpec((HALF, PREP_TM), lambda bi, ti: (0, ti))
    qta, ka, vta, qtb, kb, vtb, kn = pl.pallas_call(
        _prep_kernel,
        grid=(b, n_chunks),
        in_specs=[
            pl.BlockSpec((1, PREP_TM, d), lambda bi, ti: (bi, ti, 0)),
            _const_spec((1, d)),
            _const_spec((d, OFF_GATES)),
            _const_spec((HEAD_DIM, 1)),
            _const_spec((HEAD_DIM, 1)),
            tab_spec, tab_spec, tab_spec, tab_spec,
        ],
        out_specs=[
            pl.BlockSpec((1, A_Q_HEADS, SLAB, PREP_TM), lambda bi, ti: (bi, 0, 0, ti)),
            pl.BlockSpec((1, A_KV_HEADS, PREP_TM, SLAB), lambda bi, ti: (bi, 0, ti, 0)),
            pl.BlockSpec((1, 1, A_KV_HEADS * HEAD_DIM, PREP_TM), lambda bi, ti: (bi, ti, 0, 0)),
            pl.BlockSpec((1, B_HEADS, 2, SLAB, PREP_TM), lambda bi, ti: (bi, 0, 0, 0, ti)),
            pl.BlockSpec((1, B_HEADS, 2, PREP_TM, SLAB), lambda bi, ti: (bi, 0, 0, ti, 0)),
            pl.BlockSpec((1, 1, B_WIDTH, PREP_TM), lambda bi, ti: (bi, ti, 0, 0)),
            pl.BlockSpec((1, 16, PREP_TM), lambda bi, ti: (bi, 0, ti)),
        ],
        out_shape=[
            jax.ShapeDtypeStruct((b, A_Q_HEADS, SLAB, s), BF16),
            jax.ShapeDtypeStruct((b, A_KV_HEADS, s, SLAB), BF16),
            jax.ShapeDtypeStruct((b, n_chunks, A_KV_HEADS * HEAD_DIM, PREP_TM), BF16),
            jax.ShapeDtypeStruct((b, B_HEADS, 2, SLAB, s), BF16),
            jax.ShapeDtypeStruct((b, B_HEADS, 2, s, SLAB), BF16),
            jax.ShapeDtypeStruct((b, n_chunks, B_WIDTH, PREP_TM), BF16),
            jax.ShapeDtypeStruct((b, 16, s), F32),
        ],
        compiler_params=_params("parallel", "parallel"),
        name="prep",
    )(x, row(norm_mix), w_in[:, :OFF_GATES].astype(BF16), col(q_norm_a), col(k_norm_a),
      ca, sa, cb, sb)

    nq = s // ATTN_TQ
    steps_per_group = A_GROUP // A_STREAMS
    kmax_a = jnp.max(kn[:, :A_KV_HEADS], axis=-1)
    kmax_b = jnp.max(kn[:, 8:], axis=-1).reshape(b, B_HEADS, 2)
    qmax_a = jnp.max(qta[:, :, HEAD_DIM, :].astype(F32), axis=-1)
    qmax_b = jnp.max(qtb[:, :, :, HEAD_DIM, :].astype(F32), axis=-1)
    fast_a = jnp.max(qmax_a * jnp.repeat(kmax_a, A_GROUP, axis=1)) < FAST_BOUND
    fast_b = jnp.max(qmax_b * kmax_b) < FAST_BOUND

    def general_scratch(n, dv):
        return [
            pltpu.VMEM((SCORE_SLOTS, n, PREP_TM, ATTN_TQ), F32),
            pltpu.VMEM((SCORE_SLOTS, n, 1, ATTN_TQ), F32),
            pltpu.VMEM((n, 1, ATTN_TQ), F32),
            pltpu.VMEM((n, 1, ATTN_TQ), F32),
            pltpu.VMEM((n, dv, ATTN_TQ), F32),
        ]

    def fast_scratch(n, dv):
        return [pltpu.VMEM((n, 1, ATTN_TQ), F32), pltpu.VMEM((n, dv, ATTN_TQ), F32)]

    a_specs = [
        pl.BlockSpec((1, A_STREAMS, SLAB, ATTN_TQ), lambda bi, hp, qi: (bi, hp, 0, qi)),
        pl.BlockSpec((1, 1, s, SLAB), lambda bi, hp, qi: (bi, hp // steps_per_group, 0, 0)),
        pl.BlockSpec((1, n_chunks, HEAD_DIM, PREP_TM),
                     lambda bi, hp, qi: (bi, 0, hp // steps_per_group, 0)),
    ]
    a_kmax_spec = pl.BlockSpec((1, 1, 1, ATTN_TQ), lambda bi, hp, qi: (bi, hp // steps_per_group, 0, 0))
    a_common = dict(
        grid=(b, A_Q_HEADS // A_STREAMS, nq),
        out_specs=pl.BlockSpec((1, ATTN_TQ, A_STREAMS * HEAD_DIM), lambda bi, hp, qi: (bi, qi, hp)),
        out_shape=jax.ShapeDtypeStruct((b, s, A_WIDTH), BF16),
        compiler_params=_params("parallel", "parallel", "arbitrary"),
    )

    def attn_a_fast():
        kmax = jnp.broadcast_to(kmax_a[:, :, None, None], (b, A_KV_HEADS, 1, ATTN_TQ))
        return pl.pallas_call(_attn_a_fast, in_specs=a_specs + [a_kmax_spec],
                              scratch_shapes=fast_scratch(A_STREAMS, HEAD_DIM), name="attn_a_fast",
                              **a_common)(qta, ka, vta, kmax)

    def attn_a_general():
        return pl.pallas_call(_attn_a_general, in_specs=a_specs,
                              scratch_shapes=general_scratch(A_STREAMS, HEAD_DIM), name="attn_a",
                              **a_common)(qta, ka, vta)

    oa = lax.cond(fast_a, attn_a_fast, attn_a_general)

    lam_spec = _const_spec((1, HEAD_DIM))
    hb = B_STEP_HEADS
    b_specs = [
        pl.BlockSpec((1, hb, 2, SLAB, ATTN_TQ), lambda bi, hp, qi: (bi, hp, 0, 0, qi)),
        pl.BlockSpec((1, hb, 2, s, SLAB), lambda bi, hp, qi: (bi, hp, 0, 0, 0)),
        pl.BlockSpec((1, n_chunks, hb * B_V_DIM, PREP_TM), lambda bi, hp, qi: (bi, 0, hp, 0)),
    ]
    b_kmax_spec = pl.BlockSpec((1, hb, 2, 1, ATTN_TQ), lambda bi, hp, qi: (bi, hp, 0, 0, 0))
    b_tail_specs = [lam_spec, lam_spec, lam_spec, lam_spec, _const_spec((B_V_DIM, 1))]
    b_tail = (row(lq1), row(lk1), row(lq2), row(lk2), col(subln))
    b_common = dict(
        grid=(b, B_HEADS // hb, nq),
        out_specs=pl.BlockSpec((1, ATTN_TQ, hb * B_V_DIM), lambda bi, hp, qi: (bi, qi, hp)),
        out_shape=jax.ShapeDtypeStruct((b, s, B_WIDTH), BF16),
        compiler_params=_params("parallel", "parallel", "arbitrary"),
    )

    def attn_b_fast():
        kmax = jnp.broadcast_to(kmax_b[:, :, :, None, None], (b, B_HEADS, 2, 1, ATTN_TQ))
        return pl.pallas_call(functools.partial(_attn_b_fast, lambda_init),
                              in_specs=b_specs + [b_kmax_spec] + b_tail_specs,
                              scratch_shapes=fast_scratch(2 * hb, B_V_DIM), name="attn_b_fast",
                              **b_common)(qtb, kb, vtb, kmax, *b_tail)

    def attn_b_general():
        return pl.pallas_call(functools.partial(_attn_b_general, lambda_init),
                              in_specs=b_specs + b_tail_specs,
                              scratch_shapes=general_scratch(2 * hb, B_V_DIM), name="attn_b",
                              **b_common)(qtb, kb, vtb, *b_tail)

    ob = lax.cond(fast_b, attn_b_fast, attn_b_general)

    tok_spec = lambda width: pl.BlockSpec((DENSE_TM, width), lambda ti: (ti, 0))
    x_flat = x.reshape(n_tok, d)
    x1 = pl.pallas_call(
        _merge_kernel,
        grid=(n_tok // DENSE_TM,),
        in_specs=[
            tok_spec(d), tok_spec(A_WIDTH), tok_spec(B_WIDTH),
            _const_spec((1, d)),
            _const_spec((d, 2 * d)),
            _const_spec((A_WIDTH, d)),
            _const_spec((B_WIDTH, d)),
            _const_spec((d, d)),
        ],
        out_specs=tok_spec(d),
        out_shape=jax.ShapeDtypeStruct((n_tok, d), F32),
        compiler_params=_params("parallel"),
        name="merge",
    )(x_flat, oa.reshape(n_tok, A_WIDTH), ob.reshape(n_tok, B_WIDTH), row(norm_mix),
      w_in[:, OFF_GATES:].astype(BF16), w_proj_a.astype(BF16), w_proj_b.astype(BF16),
      w_out.astype(BF16))

    x2 = pl.pallas_call(
        functools.partial(_ffn_kernel, final),
        grid=(n_tok // DENSE_TM,),
        in_specs=[
            tok_spec(d),
            _const_spec((1, d)),
            _const_spec((d, d_ff)),
            _const_spec((d, d_ff)),
            _const_spec((d_ff, d)),
            _const_spec((1, d)),
        ],
        out_specs=tok_spec(d),
        out_shape=jax.ShapeDtypeStruct((n_tok, d), F32),
        compiler_params=_params("parallel"),
        name="ffn",
    )(x1, row(norm_ffn), w_gate.astype(BF16), w_up.astype(BF16), w_down.astype(BF16),
      row(norm_final))
    return x2.reshape(b, s, d)


def kernel(x, norm_mix, w_in, q_norm_a, k_norm_a, lambda_q1, lambda_k1, lambda_q2, lambda_k2,
           subln_b, w_proj_a, w_proj_b, w_out, norm_ffn, w_gate_ffn, w_up_ffn, w_down_ffn,
           norm_final):
    depth = norm_mix.shape[0]
    tables = _angle_tables(x.shape[1])
    for l in range(depth):
        lambda_init = 0.8 - 0.6 * math.exp(-0.3 * l)
        x = _layer(x, lambda_init, l == depth - 1, norm_mix[l], w_in[l], q_norm_a[l], k_norm_a[l],
                   lambda_q1[l], lambda_k1[l], lambda_q2[l], lambda_k2[l], subln_b[l],
                   w_proj_a[l], w_proj_b[l], w_out[l], norm_ffn[l], w_gate_ffn[l], w_up_ffn[l],
                   w_down_ffn[l], norm_final, tables)
    return x
```

```python
import functools
import math

import jax
import jax.numpy as jnp
from jax import lax
from jax.experimental import pallas as pl
from jax.experimental.pallas import tpu as pltpu

F32 = jnp.float32
BF16 = jnp.bfloat16

GRID_W = 64
HEAD_DIM = 64
HALF = HEAD_DIM // 2
A_Q_HEADS = 8
A_KV_HEADS = 2
A_GROUP = A_Q_HEADS // A_KV_HEADS
A_WIDTH = A_Q_HEADS * HEAD_DIM
B_HEADS = 4
B_V_DIM = 2 * HEAD_DIM
B_WIDTH = B_HEADS * B_V_DIM
ROPE_THETA = 10000.0
AXIAL_THETA = 10000.0
NORM_EPS = 1e-6
QK_SCALE = math.log2(math.e) / math.sqrt(HEAD_DIM)
A_STREAMS = 4
B_STEP_HEADS = 2
SCORE_LAG = 2
SCORE_SLOTS = 3

OFF_AQ = 0
OFF_AK = OFF_AQ + A_WIDTH
OFF_AV = OFF_AK + A_KV_HEADS * HEAD_DIM
OFF_BQ = OFF_AV + A_KV_HEADS * HEAD_DIM
OFF_BK = OFF_BQ + B_HEADS * 2 * HEAD_DIM
OFF_BV = OFF_BK + B_HEADS * 2 * HEAD_DIM
OFF_GATES = OFF_BV + B_WIDTH
SLAB = 2 * HEAD_DIM
FAST_BOUND = 40.0

PREP_TM = 512
ATTN_TQ = 512
FAST_KC = 8192
FAST_ORDER_A = (256, 4)
FAST_ORDER_B = (4096, 2)
DENSE_TM = 512
VMEM_LIMIT = 56 * 1024 * 1024


def _rms(x, axis):
    return x * lax.rsqrt(jnp.mean(x * x, axis=axis, keepdims=True) + NORM_EPS)


def _rope_t(xt, cos, sin):
    x1, x2 = xt[:HALF], xt[HALF:]
    return jnp.concatenate([x1 * cos - x2 * sin, x2 * cos + x1 * sin], axis=0)


def _prep_kernel(x_ref, nw_ref, w_ref, gq_ref, gk_ref, ca_ref, sa_ref, cb_ref, sb_ref,
                 qta_ref, ka_ref, vta_ref, qtb_ref, kb_ref, vtb_ref, kn_ref, qn_ref):
    x = x_ref[0]
    h = _rms(x, -1) * nw_ref[...]
    z = jnp.dot(h.astype(BF16), w_ref[...], preferred_element_type=F32)
    tm = z.shape[0]
    ca, sa, cb, sb = ca_ref[...], sa_ref[...], cb_ref[...], sb_ref[...]
    gq, gk = gq_ref[...], gk_ref[...]
    pad = jnp.zeros((SLAB - HEAD_DIM - 8, tm), F32)
    one_row = (lax.broadcasted_iota(jnp.int32, (8, tm), 0) == 0).astype(F32)

    def norm(v):
        return jnp.sqrt(jnp.sum(v * v, axis=0, keepdims=True))

    q_norms = []

    def q_block(q):
        nq = norm(q).astype(BF16).astype(F32)
        q_norms.append(nq)
        return jnp.concatenate([q, jnp.broadcast_to(nq, (8, tm)), pad], axis=0).astype(BF16)

    def k_slab(k):
        return jnp.concatenate([k, one_row, pad], axis=0).T.astype(BF16)

    aqt = z[:, OFF_AQ:OFF_AK].T
    for hd in range(A_Q_HEADS):
        q = aqt[hd * HEAD_DIM:(hd + 1) * HEAD_DIM]
        qta_ref[0, hd] = q_block(_rope_t(_rms(q, 0) * gq, ca, sa) * QK_SCALE)
    akt = z[:, OFF_AK:OFF_AV].T
    k_norms = []
    for g in range(A_KV_HEADS):
        k = _rope_t(_rms(akt[g * HEAD_DIM:(g + 1) * HEAD_DIM], 0) * gk, ca, sa)
        ka_ref[0, g] = k_slab(k)
        k_norms.append(norm(k))
    k_norms.append(jnp.zeros((8 - A_KV_HEADS, tm), F32))
    vta_ref[0, 0] = z[:, OFF_AV:OFF_BQ].T.astype(BF16)

    bqt = z[:, OFF_BQ:OFF_BK].T
    bkt = z[:, OFF_BK:OFF_BV].T
    for hd in range(B_HEADS):
        for c in range(2):
            r0 = (hd * 2 + c) * HEAD_DIM
            qtb_ref[0, hd, c] = q_block(_rope_t(bqt[r0:r0 + HEAD_DIM], cb, sb) * QK_SCALE)
            k = _rope_t(bkt[r0:r0 + HEAD_DIM], cb, sb)
            kb_ref[0, hd, c] = k_slab(k)
            k_norms.append(norm(k))
    vtb_ref[0, 0] = z[:, OFF_BV:OFF_GATES].T.astype(BF16)
    kn_ref[0] = jnp.concatenate(k_norms, axis=0)
    qn_ref[0] = jnp.concatenate(q_norms, axis=0)


def _plain_query(qt):
    qf = qt.astype(F32)
    rows = lax.broadcasted_iota(jnp.int32, qf.shape, 0)
    return jnp.where(rows < HEAD_DIM, qf, 0.0).astype(BF16)


def _shifted_query(qt, kmax):
    qf = qt.astype(F32)
    rows = lax.broadcasted_iota(jnp.int32, qf.shape, 0)
    shifted = jnp.where(rows == HEAD_DIM, -(qf * kmax), jnp.where(rows < HEAD_DIM, qf, 0.0))
    return shifted.astype(BF16)


def _flash_fast(q_list, k_of, vt_ref, v_rows, l_ref, acc_ref, sb, skew):
    n = len(q_list)
    vb = vt_ref.shape[3]
    pb = min(sb, vb)
    l_ref[...] = jnp.zeros(l_ref.shape, F32)
    acc_ref[...] = jnp.zeros(acc_ref.shape, F32)

    def body(i, carry):
        off = pl.multiple_of(i * FAST_KC, FAST_KC)
        l = [l_ref[j] for j in range(n)]
        acc = [acc_ref[j] for j in range(n)]
        items = [(g, j) for g in range(FAST_KC // sb) for j in range(n)]
        scores = {}

        def consume(g, j):
            p = jnp.exp2(scores.pop((g, j)))
            l[j] = l[j] + jnp.sum(p, axis=0, keepdims=True)
            p = p.astype(BF16)
            for u in range(sb // pb):
                k0 = g * sb + u * pb
                vblk = vt_ref[0, i * (FAST_KC // vb) + k0 // vb, v_rows(j), k0 % vb:k0 % vb + pb]
                acc[j] = acc[j] + jnp.dot(vblk, p[u * pb:(u + 1) * pb], preferred_element_type=F32)

        for t, (g, j) in enumerate(items):
            scores[(g, j)] = jnp.dot(k_of(j, off + g * sb, sb), q_list[j],
                                     preferred_element_type=F32)
            if t >= skew:
                consume(*items[t - skew])
        for g, j in items[max(len(items) - skew, 0):]:
            consume(g, j)
        for j in range(n):
            l_ref[j], acc_ref[j] = l[j], acc[j]
        return carry

    lax.fori_loop(0, vt_ref.shape[1] * vb // FAST_KC, body, 0)


def _flash_streams(q_list, k_of, vt_ref, v_rows, s_buf, mx_buf, m_ref, l_ref, acc_ref):
    n = len(q_list)
    n_chunks, kc = vt_ref.shape[1], vt_ref.shape[3]
    m_ref[...] = jnp.full(m_ref.shape, -jnp.inf, F32)
    l_ref[...] = jnp.zeros(l_ref.shape, F32)
    acc_ref[...] = jnp.zeros(acc_ref.shape, F32)

    def scores(c, slot):
        off = c * kc if isinstance(c, int) else pl.multiple_of(c * kc, kc)
        for j in range(n):
            s = jnp.dot(k_of(j, off, kc), q_list[j], preferred_element_type=F32)
            s_buf[slot, j] = s
            mx_buf[slot, j] = jnp.max(s, axis=0, keepdims=True)

    def softmax_pv(c, slot):
        for j in range(n):
            vblk = vt_ref[0, c, v_rows(j)]
            m_old = m_ref[j]
            m_new = jnp.maximum(m_old, mx_buf[slot, j])
            alpha = jnp.exp2(m_old - m_new)
            p = jnp.exp2(s_buf[slot, j] - m_new)
            l_ref[j] = alpha * l_ref[j] + jnp.sum(p, axis=0, keepdims=True)
            acc_ref[j] = alpha * acc_ref[j] + jnp.dot(vblk, p.astype(BF16),
                                                      preferred_element_type=F32)
            m_ref[j] = m_new

    n_slots = s_buf.shape[0]

    def stage(c, u):
        if not isinstance(c, int) or c + SCORE_LAG < n_chunks:
            scores(c + SCORE_LAG, (u + SCORE_LAG) % n_slots)
        softmax_pv(c, u)

    for c in range(SCORE_LAG):
        scores(c, c % n_slots)
    n_loop = (n_chunks - SCORE_LAG) // n_slots

    def body(i, carry):
        for u in range(n_slots):
            stage(i * n_slots + u, u)
        return carry

    lax.fori_loop(0, n_loop, body, 0)
    for c in range(n_loop * n_slots, n_chunks):
        stage(c, c % n_slots)


def _finish_a(o_ref, l_ref, acc_ref):
    outs = [acc_ref[j] * (1.0 / l_ref[j]) for j in range(A_STREAMS)]
    o_ref[0] = jnp.concatenate(outs, axis=0).T.astype(BF16)


def _finish_b(lambda_init, lam_refs, sub_ref, o_ref, l_ref, acc_ref):
    lq1_ref, lk1_ref, lq2_ref, lk2_ref = lam_refs
    lam = (jnp.exp(jnp.sum(lq1_ref[...] * lk1_ref[...], axis=-1, keepdims=True))
           - jnp.exp(jnp.sum(lq2_ref[...] * lk2_ref[...], axis=-1, keepdims=True))
           + lambda_init)
    outs = []
    for hd in range(B_STEP_HEADS):
        j1, j2 = 2 * hd, 2 * hd + 1
        o = acc_ref[j1] * (1.0 / l_ref[j1]) - lam * (acc_ref[j2] * (1.0 / l_ref[j2]))
        outs.append(_rms(o, 0) * sub_ref[...] * (1.0 - lambda_init))
    o_ref[0] = jnp.concatenate(outs, axis=0).T.astype(BF16)


def _all_rows(j):
    return slice(None)


def _b_rows(j):
    return slice((j // 2) * B_V_DIM, (j // 2 + 1) * B_V_DIM)


def _k_of_a(k_ref):
    return lambda j, off, size: k_ref[0, 0, pl.ds(off, size), :]


def _k_of_b(k_ref):
    return lambda j, off, size: k_ref[0, j // 2, j % 2, pl.ds(off, size), :]


def _attn_a_general(qt_ref, k_ref, vt_ref, o_ref, s_buf, mx_buf, m_ref, l_ref, acc_ref):
    q_list = [_plain_query(qt_ref[0, j]) for j in range(A_STREAMS)]
    _flash_streams(q_list, _k_of_a(k_ref), vt_ref, _all_rows, s_buf, mx_buf, m_ref, l_ref, acc_ref)
    _finish_a(o_ref, l_ref, acc_ref)


def _attn_a_fast(qt_ref, k_ref, vt_ref, kmax_ref, o_ref, l_ref, acc_ref):
    q_list = [_shifted_query(qt_ref[0, j], kmax_ref[0, 0]) for j in range(A_STREAMS)]
    _flash_fast(q_list, _k_of_a(k_ref), vt_ref, _all_rows, l_ref, acc_ref, *FAST_ORDER_A)
    _finish_a(o_ref, l_ref, acc_ref)


def _attn_b_general(lambda_init, qt_ref, k_ref, vt_ref, lq1_ref, lk1_ref, lq2_ref, lk2_ref,
                    sub_ref, o_ref, s_buf, mx_buf, m_ref, l_ref, acc_ref):
    q_list = [_plain_query(qt_ref[0, hd, c]) for hd in range(B_STEP_HEADS) for c in range(2)]
    _flash_streams(q_list, _k_of_b(k_ref), vt_ref, _b_rows, s_buf, mx_buf, m_ref, l_ref, acc_ref)
    _finish_b(lambda_init, (lq1_ref, lk1_ref, lq2_ref, lk2_ref), sub_ref, o_ref, l_ref, acc_ref)


def _attn_b_fast(lambda_init, qt_ref, k_ref, vt_ref, kmax_ref, lq1_ref, lk1_ref, lq2_ref, lk2_ref,
                 sub_ref, o_ref, l_ref, acc_ref):
    q_list = [_shifted_query(qt_ref[0, hd, c], kmax_ref[0, hd, c])
              for hd in range(B_STEP_HEADS) for c in range(2)]
    _flash_fast(q_list, _k_of_b(k_ref), vt_ref, _b_rows, l_ref, acc_ref, *FAST_ORDER_B)
    _finish_b(lambda_init, (lq1_ref, lk1_ref, lq2_ref, lk2_ref), sub_ref, o_ref, l_ref, acc_ref)


def _merge_kernel(x_ref, oa_ref, ob_ref, nw_ref, wg_ref, wa_ref, wb_ref, wo_ref, x1_ref):
    x = x_ref[...]
    h = (_rms(x, -1) * nw_ref[...]).astype(BF16)
    gates = jnp.dot(h, wg_ref[...], preferred_element_type=F32)
    d = x.shape[-1]
    ya = jnp.dot(oa_ref[...], wa_ref[...], preferred_element_type=F32)
    yb = jnp.dot(ob_ref[...], wb_ref[...], preferred_element_type=F32)
    y = jax.nn.sigmoid(gates[:, :d]) * ya + jax.nn.sigmoid(gates[:, d:]) * yb
    x1_ref[...] = x + jnp.dot(y.astype(BF16), wo_ref[...], preferred_element_type=F32)


def _ffn_kernel(final, x_ref, nw_ref, wg_ref, wu_ref, wd_ref, nf_ref, o_ref):
    x = x_ref[...]
    h = (_rms(x, -1) * nw_ref[...]).astype(BF16)
    gate = jnp.dot(h, wg_ref[...], preferred_element_type=F32)
    up = jnp.dot(h, wu_ref[...], preferred_element_type=F32)
    act = (jax.nn.silu(gate) * up).astype(BF16)
    x2 = x + jnp.dot(act, wd_ref[...], preferred_element_type=F32)
    if final:
        x2 = _rms(x2, -1) * nf_ref[...]
    o_ref[...] = x2


def _const_spec(shape):
    nd = len(shape)
    return pl.BlockSpec(shape, lambda *_: (0,) * nd, pipeline_mode=pl.Buffered(1))


def _rope_angles_t(pos, dim, theta):
    inv_freq = theta ** (-jnp.arange(0, dim, 2, dtype=F32) / dim)
    return (pos[:, None] * inv_freq[None, :]).T


def _angle_tables(seq_len):
    rows = seq_len // GRID_W
    row = jnp.broadcast_to(jnp.arange(rows, dtype=F32)[:, None], (rows, GRID_W)).reshape(-1)
    col = jnp.broadcast_to(jnp.arange(GRID_W, dtype=F32)[None, :], (rows, GRID_W)).reshape(-1)
    ang_a = jnp.concatenate([_rope_angles_t(row, HALF, AXIAL_THETA),
                             _rope_angles_t(col, HALF, AXIAL_THETA)], axis=0)
    ang_b = _rope_angles_t(jnp.arange(seq_len, dtype=F32), HEAD_DIM, ROPE_THETA)
    return jnp.cos(ang_a), jnp.sin(ang_a), jnp.cos(ang_b), jnp.sin(ang_b)


def _params(*sem):
    return pltpu.CompilerParams(dimension_semantics=sem, vmem_limit_bytes=VMEM_LIMIT)


def _layer(x, lambda_init, final, norm_mix, w_in, q_norm_a, k_norm_a, lq1, lk1, lq2, lk2, subln,
           w_proj_a, w_proj_b, w_out, norm_ffn, w_gate, w_up, w_down, norm_final, tables):
    b, s, d = x.shape
    d_ff = w_gate.shape[-1]
    n_tok = b * s
    n_chunks = s // PREP_TM
    ca, sa, cb, sb = tables
    row = lambda v: v.reshape(1, -1).astype(F32)
    col = lambda v: v.reshape(-1, 1).astype(F32)

    tab_spec = pl.BlockSpec((HALF, PREP_TM), lambda bi, ti: (0, ti))
    qta, ka, vta, qtb, kb, vtb, kn, qn = pl.pallas_call(
        _prep_kernel,
        grid=(b, n_chunks),
        in_specs=[
            pl.BlockSpec((1, PREP_TM, d), lambda bi, ti: (bi, ti, 0)),
            _const_spec((1, d)),
            _const_spec((d, OFF_GATES)),
            _const_spec((HEAD_DIM, 1)),
            _const_spec((HEAD_DIM, 1)),
            tab_spec, tab_spec, tab_spec, tab_spec,
        ],
        out_specs=[
            pl.BlockSpec((1, A_Q_HEADS, SLAB, PREP_TM), lambda bi, ti: (bi, 0, 0, ti)),
            pl.BlockSpec((1, A_KV_HEADS, PREP_TM, SLAB), lambda bi, ti: (bi, 0, ti, 0)),
            pl.BlockSpec((1, 1, A_KV_HEADS * HEAD_DIM, PREP_TM), lambda bi, ti: (bi, ti, 0, 0)),
            pl.BlockSpec((1, B_HEADS, 2, SLAB, PREP_TM), lambda bi, ti: (bi, 0, 0, 0, ti)),
            pl.BlockSpec((1, B_HEADS, 2, PREP_TM, SLAB), lambda bi, ti: (bi, 0, 0, ti, 0)),
            pl.BlockSpec((1, 1, B_WIDTH, PREP_TM), lambda bi, ti: (bi, ti, 0, 0)),
            pl.BlockSpec((1, 16, PREP_TM), lambda bi, ti: (bi, 0, ti)),
            pl.BlockSpec((1, 16, PREP_TM), lambda bi, ti: (bi, 0, ti)),
        ],
        out_shape=[
            jax.ShapeDtypeStruct((b, A_Q_HEADS, SLAB, s), BF16),
            jax.ShapeDtypeStruct((b, A_KV_HEADS, s, SLAB), BF16),
            jax.ShapeDtypeStruct((b, n_chunks, A_KV_HEADS * HEAD_DIM, PREP_TM), BF16),
            jax.ShapeDtypeStruct((b, B_HEADS, 2, SLAB, s), BF16),
            jax.ShapeDtypeStruct((b, B_HEADS, 2, s, SLAB), BF16),
            jax.ShapeDtypeStruct((b, n_chunks, B_WIDTH, PREP_TM), BF16),
            jax.ShapeDtypeStruct((b, 16, s), F32),
            jax.ShapeDtypeStruct((b, 16, s), F32),
        ],
        compiler_params=_params("parallel", "parallel"),
        name="prep",
    )(x, row(norm_mix), w_in[:, :OFF_GATES].astype(BF16), col(q_norm_a), col(k_norm_a),
      ca, sa, cb, sb)

    nq = s // ATTN_TQ
    steps_per_group = A_GROUP // A_STREAMS
    kmax_a = jnp.max(kn[:, :A_KV_HEADS], axis=-1)
    kmax_b = jnp.max(kn[:, 8:], axis=-1).reshape(b, B_HEADS, 2)
    qmax_a = jnp.max(qn[:, :A_Q_HEADS], axis=-1)
    qmax_b = jnp.max(qn[:, A_Q_HEADS:], axis=-1).reshape(b, B_HEADS, 2)
    fast_a = jnp.max(qmax_a * jnp.repeat(kmax_a, A_GROUP, axis=1)) < FAST_BOUND
    fast_b = jnp.max(qmax_b * kmax_b) < FAST_BOUND

    def general_scratch(n, dv):
        return [
            pltpu.VMEM((SCORE_SLOTS, n, PREP_TM, ATTN_TQ), F32),
            pltpu.VMEM((SCORE_SLOTS, n, 1, ATTN_TQ), F32),
            pltpu.VMEM((n, 1, ATTN_TQ), F32),
            pltpu.VMEM((n, 1, ATTN_TQ), F32),
            pltpu.VMEM((n, dv, ATTN_TQ), F32),
        ]

    def fast_scratch(n, dv):
        return [pltpu.VMEM((n, 1, ATTN_TQ), F32), pltpu.VMEM((n, dv, ATTN_TQ), F32)]

    a_specs = [
        pl.BlockSpec((1, A_STREAMS, SLAB, ATTN_TQ), lambda bi, hp, qi: (bi, hp, 0, qi)),
        pl.BlockSpec((1, 1, s, SLAB), lambda bi, hp, qi: (bi, hp // steps_per_group, 0, 0)),
        pl.BlockSpec((1, n_chunks, HEAD_DIM, PREP_TM),
                     lambda bi, hp, qi: (bi, 0, hp // steps_per_group, 0)),
    ]
    a_kmax_spec = pl.BlockSpec((1, 1, 1, ATTN_TQ), lambda bi, hp, qi: (bi, hp // steps_per_group, 0, 0))
    a_common = dict(
        grid=(b, A_Q_HEADS // A_STREAMS, nq),
        out_specs=pl.BlockSpec((1, ATTN_TQ, A_STREAMS * HEAD_DIM), lambda bi, hp, qi: (bi, qi, hp)),
        out_shape=jax.ShapeDtypeStruct((b, s, A_WIDTH), BF16),
        compiler_params=_params("parallel", "parallel", "arbitrary"),
    )

    def attn_a_fast():
        kmax = jnp.broadcast_to(kmax_a[:, :, None, None], (b, A_KV_HEADS, 1, ATTN_TQ))
        return pl.pallas_call(_attn_a_fast, in_specs=a_specs + [a_kmax_spec],
                              scratch_shapes=fast_scratch(A_STREAMS, HEAD_DIM), name="attn_a_fast",
                              **a_common)(qta, ka, vta, kmax)

    def attn_a_general():
        return pl.pallas_call(_attn_a_general, in_specs=a_specs,
                              scratch_shapes=general_scratch(A_STREAMS, HEAD_DIM), name="attn_a",
                              **a_common)(qta, ka, vta)

    oa = lax.cond(fast_a, attn_a_fast, attn_a_general)

    lam_spec = _const_spec((1, HEAD_DIM))
    hb = B_STEP_HEADS
    b_specs = [
        pl.BlockSpec((1, hb, 2, SLAB, ATTN_TQ), lambda bi, hp, qi: (bi, hp, 0, 0, qi)),
        pl.BlockSpec((1, hb, 2, s, SLAB), lambda bi, hp, qi: (bi, hp, 0, 0, 0)),
        pl.BlockSpec((1, n_chunks, hb * B_V_DIM, PREP_TM), lambda bi, hp, qi: (bi, 0, hp, 0)),
    ]
    b_kmax_spec = pl.BlockSpec((1, hb, 2, 1, ATTN_TQ), lambda bi, hp, qi: (bi, hp, 0, 0, 0))
    b_tail_specs = [lam_spec, lam_spec, lam_spec, lam_spec, _const_spec((B_V_DIM, 1))]
    b_tail = (row(lq1), row(lk1), row(lq2), row(lk2), col(subln))
    b_common = dict(
        grid=(b, B_HEADS // hb, nq),
        out_specs=pl.BlockSpec((1, ATTN_TQ, hb * B_V_DIM), lambda bi, hp, qi: (bi, qi, hp)),
        out_shape=jax.ShapeDtypeStruct((b, s, B_WIDTH), BF16),
        compiler_params=_params("parallel", "parallel", "arbitrary"),
    )

    def attn_b_fast():
        kmax = jnp.broadcast_to(kmax_b[:, :, :, None, None], (b, B_HEADS, 2, 1, ATTN_TQ))
        return pl.pallas_call(functools.partial(_attn_b_fast, lambda_init),
                              in_specs=b_specs + [b_kmax_spec] + b_tail_specs,
                              scratch_shapes=fast_scratch(2 * hb, B_V_DIM), name="attn_b_fast",
                              **b_common)(qtb, kb, vtb, kmax, *b_tail)

    def attn_b_general():
        return pl.pallas_call(functools.partial(_attn_b_general, lambda_init),
                              in_specs=b_specs + b_tail_specs,
                              scratch_shapes=general_scratch(2 * hb, B_V_DIM), name="attn_b",
                              **b_common)(qtb, kb, vtb, *b_tail)

    ob = lax.cond(fast_b, attn_b_fast, attn_b_general)

    tok_spec = lambda width: pl.BlockSpec((DENSE_TM, width), lambda ti: (ti, 0))
    x_flat = x.reshape(n_tok, d)
    x1 = pl.pallas_call(
        _merge_kernel,
        grid=(n_tok // DENSE_TM,),
        in_specs=[
            tok_spec(d), tok_spec(A_WIDTH), tok_spec(B_WIDTH),
            _const_spec((1, d)),
            _const_spec((d, 2 * d)),
            _const_spec((A_WIDTH, d)),
            _const_spec((B_WIDTH, d)),
            _const_spec((d, d)),
        ],
        out_specs=tok_spec(d),
        out_shape=jax.ShapeDtypeStruct((n_tok, d), F32),
        compiler_params=_params("parallel"),
        name="merge",
    )(x_flat, oa.reshape(n_tok, A_WIDTH), ob.reshape(n_tok, B_WIDTH), row(norm_mix),
      w_in[:, OFF_GATES:].astype(BF16), w_proj_a.astype(BF16), w_proj_b.astype(BF16),
      w_out.astype(BF16))

    x2 = pl.pallas_call(
        functools.partial(_ffn_kernel, final),
        grid=(n_tok // DENSE_TM,),
        in_specs=[
            tok_spec(d),
            _const_spec((1, d)),
            _const_spec((d, d_ff)),
            _const_spec((d, d_ff)),
            _const_spec((d_ff, d)),
            _const_spec((1, d)),
        ],
        out_specs=tok_spec(d),
        out_shape=jax.ShapeDtypeStruct((n_tok, d), F32),
        compiler_params=_params("parallel"),
        name="ffn",
    )(x1, row(norm_ffn), w_gate.astype(BF16), w_up.astype(BF16), w_down.astype(BF16),
      row(norm_final))
    return x2.reshape(b, s, d)


def kernel(x, norm_mix, w_in, q_norm_a, k_norm_a, lambda_q1, lambda_k1, lambda_q2, lambda_k2,
           subln_b, w_proj_a, w_proj_b, w_out, norm_ffn, w_gate_ffn, w_up_ffn, w_down_ffn,
           norm_final):
    depth = norm_mix.shape[0]
    tables = _angle_tables(x.shape[1])
    for l in range(depth):
        lambda_init = 0.8 - 0.6 * math.exp(-0.3 * l)
        x = _layer(x, lambda_init, l == depth - 1, norm_mix[l], w_in[l], q_norm_a[l], k_norm_a[l],
                   lambda_q1[l], lambda_k1[l], lambda_q2[l], lambda_k2[l], subln_b[l],
                   w_proj_a[l], w_proj_b[l], w_out[l], norm_ffn[l], w_gate_ffn[l], w_up_ffn[l],
                   w_down_ffn[l], norm_final, tables)
    return x
```

```python
import functools
import math

import jax
import jax.numpy as jnp
from jax import lax
from jax.experimental import pallas as pl
from jax.experimental.pallas import tpu as pltpu

F32 = jnp.float32
BF16 = jnp.bfloat16

GRID_W = 64
HEAD_DIM = 64
HALF = HEAD_DIM // 2
A_Q_HEADS = 8
A_KV_HEADS = 2
A_GROUP = A_Q_HEADS // A_KV_HEADS
A_WIDTH = A_Q_HEADS * HEAD_DIM
B_HEADS = 4
B_V_DIM = 2 * HEAD_DIM
B_WIDTH = B_HEADS * B_V_DIM
ROPE_THETA = 10000.0
AXIAL_THETA = 10000.0
NORM_EPS = 1e-6
QK_SCALE = math.log2(math.e) / math.sqrt(HEAD_DIM)
A_STREAMS = 4
B_STEP_HEADS = 2
SCORE_LAG = 2
SCORE_SLOTS = 3

OFF_AQ = 0
OFF_AK = OFF_AQ + A_WIDTH
OFF_AV = OFF_AK + A_KV_HEADS * HEAD_DIM
OFF_BQ = OFF_AV + A_KV_HEADS * HEAD_DIM
OFF_BK = OFF_BQ + B_HEADS * 2 * HEAD_DIM
OFF_BV = OFF_BK + B_HEADS * 2 * HEAD_DIM
OFF_GATES = OFF_BV + B_WIDTH
SLAB = 2 * HEAD_DIM
FAST_BOUND = 40.0

PREP_TM = 512
PREP_PARTS = 2
ATTN_TQ = 512
FAST_KC = 8192
FAST_ORDER_A = (256, 4)
FAST_ORDER_B = (4096, 2)
DENSE_TM = 512
DENSE_PARTS = 2
VMEM_LIMIT = 56 * 1024 * 1024


def _rms(x, axis):
    return x * lax.rsqrt(jnp.mean(x * x, axis=axis, keepdims=True) + NORM_EPS)


def _rope_t(xt, cos, sin):
    x1, x2 = xt[:HALF], xt[HALF:]
    return jnp.concatenate([x1 * cos - x2 * sin, x2 * cos + x1 * sin], axis=0)


def _prep_kernel(x_ref, nw_ref, w_ref, gq_ref, gk_ref, ca_ref, sa_ref, cb_ref, sb_ref,
                 qta_ref, ka_ref, vta_ref, qtb_ref, kb_ref, vtb_ref, kn_ref, qn_ref, wb_ref):
    @pl.when((pl.program_id(0) == 0) & (pl.program_id(1) == 0))
    def _():
        wb_ref[...] = w_ref[...].astype(BF16)

    tm = x_ref.shape[1] // PREP_PARTS
    gq, gk = gq_ref[...], gk_ref[...]
    pad = jnp.zeros((SLAB - HEAD_DIM - 8, tm), F32)
    one_row = (lax.broadcasted_iota(jnp.int32, (8, tm), 0) == 0).astype(F32)

    def norm(v):
        return jnp.sqrt(jnp.sum(v * v, axis=0, keepdims=True))

    def k_slab(k):
        return jnp.concatenate([k, one_row, pad], axis=0).T.astype(BF16)

    groups = [slice(i * tm, (i + 1) * tm) for i in range(PREP_PARTS)]
    zs = []
    for t in groups:
        h = _rms(x_ref[0, t], -1) * nw_ref[...]
        zs.append(jnp.dot(h.astype(BF16), wb_ref[...], preferred_element_type=F32))

    for t, z in zip(groups, zs):
        ca, sa, cb, sb = ca_ref[:, t], sa_ref[:, t], cb_ref[:, t], sb_ref[:, t]
        q_norms, k_norms = [], []

        def q_block(q):
            nq = norm(q).astype(BF16).astype(F32)
            q_norms.append(nq)
            return jnp.concatenate([q, jnp.broadcast_to(nq, (8, tm)), pad], axis=0).astype(BF16)

        aqt = z[:, OFF_AQ:OFF_AK].T
        for hd in range(A_Q_HEADS):
            q = aqt[hd * HEAD_DIM:(hd + 1) * HEAD_DIM]
            qta_ref[0, hd, :, t] = q_block(_rope_t(_rms(q, 0) * gq, ca, sa) * QK_SCALE)
        akt = z[:, OFF_AK:OFF_AV].T
        for g in range(A_KV_HEADS):
            k = _rope_t(_rms(akt[g * HEAD_DIM:(g + 1) * HEAD_DIM], 0) * gk, ca, sa)
            ka_ref[0, g, t, :] = k_slab(k)
            k_norms.append(norm(k))
        k_norms.append(jnp.zeros((8 - A_KV_HEADS, tm), F32))
        vta_ref[0, 0, :, t] = z[:, OFF_AV:OFF_BQ].T.astype(BF16)

        bqt = z[:, OFF_BQ:OFF_BK].T
        bkt = z[:, OFF_BK:OFF_BV].T
        for hd in range(B_HEADS):
            for c in range(2):
                r0 = (hd * 2 + c) * HEAD_DIM
                qtb_ref[0, hd, c, :, t] = q_block(_rope_t(bqt[r0:r0 + HEAD_DIM], cb, sb) * QK_SCALE)
                k = _rope_t(bkt[r0:r0 + HEAD_DIM], cb, sb)
                kb_ref[0, hd, c, t, :] = k_slab(k)
                k_norms.append(norm(k))
        vtb_ref[0, 0, :, t] = z[:, OFF_BV:OFF_GATES].T.astype(BF16)
        kn_ref[0, :, t] = jnp.concatenate(k_norms, axis=0)
        qn_ref[0, :, t] = jnp.concatenate(q_norms, axis=0)


def _plain_query(qt):
    qf = qt.astype(F32)
    rows = lax.broadcasted_iota(jnp.int32, qf.shape, 0)
    return jnp.where(rows < HEAD_DIM, qf, 0.0).astype(BF16)


def _shifted_query(qt, kmax):
    qf = qt.astype(F32)
    rows = lax.broadcasted_iota(jnp.int32, qf.shape, 0)
    shifted = jnp.where(rows == HEAD_DIM, -(qf * kmax), jnp.where(rows < HEAD_DIM, qf, 0.0))
    return shifted.astype(BF16)


def _flash_fast(q_list, k_of, vt_ref, v_rows, l_ref, acc_ref, sb, skew):
    n = len(q_list)
    vb = vt_ref.shape[3]
    pb = min(sb, vb)
    l_ref[...] = jnp.zeros(l_ref.shape, F32)
    acc_ref[...] = jnp.zeros(acc_ref.shape, F32)

    def body(i, carry):
        off = pl.multiple_of(i * FAST_KC, FAST_KC)
        l = [l_ref[j] for j in range(n)]
        acc = [acc_ref[j] for j in range(n)]
        items = [(g, j) for g in range(FAST_KC // sb) for j in range(n)]
        scores = {}

        def consume(g, j):
            p = jnp.exp2(scores.pop((g, j)))
            l[j] = l[j] + jnp.sum(p, axis=0, keepdims=True)
            p = p.astype(BF16)
            for u in range(sb // pb):
                k0 = g * sb + u * pb
                vblk = vt_ref[0, i * (FAST_KC // vb) + k0 // vb, v_rows(j), k0 % vb:k0 % vb + pb]
                acc[j] = acc[j] + jnp.dot(vblk, p[u * pb:(u + 1) * pb], preferred_element_type=F32)

        for t, (g, j) in enumerate(items):
            scores[(g, j)] = jnp.dot(k_of(j, off + g * sb, sb), q_list[j],
                                     preferred_element_type=F32)
            if t >= skew:
                consume(*items[t - skew])
        for g, j in items[max(len(items) - skew, 0):]:
            consume(g, j)
        for j in range(n):
            l_ref[j], acc_ref[j] = l[j], acc[j]
        return carry

    lax.fori_loop(0, vt_ref.shape[1] * vb // FAST_KC, body, 0)


def _flash_streams(q_list, k_of, vt_ref, v_rows, s_buf, mx_buf, m_ref, l_ref, acc_ref):
    n = len(q_list)
    n_chunks, kc = vt_ref.shape[1], vt_ref.shape[3]
    m_ref[...] = jnp.full(m_ref.shape, -jnp.inf, F32)
    l_ref[...] = jnp.zeros(l_ref.shape, F32)
    acc_ref[...] = jnp.zeros(acc_ref.shape, F32)

    def scores(c, slot):
        off = c * kc if isinstance(c, int) else pl.multiple_of(c * kc, kc)
        for j in range(n):
            s = jnp.dot(k_of(j, off, kc), q_list[j], preferred_element_type=F32)
            s_buf[slot, j] = s
            mx_buf[slot, j] = jnp.max(s, axis=0, keepdims=True)

    def softmax_pv(c, slot):
        for j in range(n):
            vblk = vt_ref[0, c, v_rows(j)]
            m_old = m_ref[j]
            m_new = jnp.maximum(m_old, mx_buf[slot, j])
            alpha = jnp.exp2(m_old - m_new)
            p = jnp.exp2(s_buf[slot, j] - m_new)
            l_ref[j] = alpha * l_ref[j] + jnp.sum(p, axis=0, keepdims=True)
            acc_ref[j] = alpha * acc_ref[j] + jnp.dot(vblk, p.astype(BF16),
                                                      preferred_element_type=F32)
            m_ref[j] = m_new

    n_slots = s_buf.shape[0]

    def stage(c, u):
        if not isinstance(c, int) or c + SCORE_LAG < n_chunks:
            scores(c + SCORE_LAG, (u + SCORE_LAG) % n_slots)
        softmax_pv(c, u)

    for c in range(SCORE_LAG):
        scores(c, c % n_slots)
    n_loop = (n_chunks - SCORE_LAG) // n_slots

    def body(i, carry):
        for u in range(n_slots):
            stage(i * n_slots + u, u)
        return carry

    lax.fori_loop(0, n_loop, body, 0)
    for c in range(n_loop * n_slots, n_chunks):
        stage(c, c % n_slots)


def _finish_a(o_ref, l_ref, acc_ref):
    outs = [acc_ref[j] * (1.0 / l_ref[j]) for j in range(A_STREAMS)]
    o_ref[0] = jnp.concatenate(outs, axis=0).T.astype(BF16)


def _finish_b(lambda_init, lam_refs, sub_ref, o_ref, l_ref, acc_ref):
    lq1_ref, lk1_ref, lq2_ref, lk2_ref = lam_refs
    lam = (jnp.exp(jnp.sum(lq1_ref[...] * lk1_ref[...], axis=-1, keepdims=True))
           - jnp.exp(jnp.sum(lq2_ref[...] * lk2_ref[...], axis=-1, keepdims=True))
           + lambda_init)
    outs = []
    for hd in range(B_STEP_HEADS):
        j1, j2 = 2 * hd, 2 * hd + 1
        o = acc_ref[j1] * (1.0 / l_ref[j1]) - lam * (acc_ref[j2] * (1.0 / l_ref[j2]))
        outs.append(_rms(o, 0) * sub_ref[...] * (1.0 - lambda_init))
    o_ref[0] = jnp.concatenate(outs, axis=0).T.astype(BF16)


def _all_rows(j):
    return slice(None)


def _b_rows(j):
    return slice((j // 2) * B_V_DIM, (j // 2 + 1) * B_V_DIM)


def _k_of_a(k_ref):
    return lambda j, off, size: k_ref[0, 0, pl.ds(off, size), :]


def _k_of_b(k_ref):
    return lambda j, off, size: k_ref[0, j // 2, j % 2, pl.ds(off, size), :]


def _attn_a_general(qt_ref, k_ref, vt_ref, o_ref, s_buf, mx_buf, m_ref, l_ref, acc_ref):
    q_list = [_plain_query(qt_ref[0, j]) for j in range(A_STREAMS)]
    _flash_streams(q_list, _k_of_a(k_ref), vt_ref, _all_rows, s_buf, mx_buf, m_ref, l_ref, acc_ref)
    _finish_a(o_ref, l_ref, acc_ref)


def _attn_a_fast(qt_ref, k_ref, vt_ref, kmax_ref, o_ref, l_ref, acc_ref):
    q_list = [_shifted_query(qt_ref[0, j], kmax_ref[0, 0]) for j in range(A_STREAMS)]
    _flash_fast(q_list, _k_of_a(k_ref), vt_ref, _all_rows, l_ref, acc_ref, *FAST_ORDER_A)
    _finish_a(o_ref, l_ref, acc_ref)


def _attn_b_general(lambda_init, qt_ref, k_ref, vt_ref, lq1_ref, lk1_ref, lq2_ref, lk2_ref,
                    sub_ref, o_ref, s_buf, mx_buf, m_ref, l_ref, acc_ref):
    q_list = [_plain_query(qt_ref[0, hd, c]) for hd in range(B_STEP_HEADS) for c in range(2)]
    _flash_streams(q_list, _k_of_b(k_ref), vt_ref, _b_rows, s_buf, mx_buf, m_ref, l_ref, acc_ref)
    _finish_b(lambda_init, (lq1_ref, lk1_ref, lq2_ref, lk2_ref), sub_ref, o_ref, l_ref, acc_ref)


def _attn_b_fast(lambda_init, qt_ref, k_ref, vt_ref, kmax_ref, lq1_ref, lk1_ref, lq2_ref, lk2_ref,
                 sub_ref, o_ref, l_ref, acc_ref):
    q_list = [_shifted_query(qt_ref[0, hd, c], kmax_ref[0, hd, c])
              for hd in range(B_STEP_HEADS) for c in range(2)]
    _flash_fast(q_list, _k_of_b(k_ref), vt_ref, _b_rows, l_ref, acc_ref, *FAST_ORDER_B)
    _finish_b(lambda_init, (lq1_ref, lk1_ref, lq2_ref, lk2_ref), sub_ref, o_ref, l_ref, acc_ref)


def _row_parts(ref):
    rows = ref.shape[0] // DENSE_PARTS
    return [slice(i * rows, (i + 1) * rows) for i in range(DENSE_PARTS)]


def _merge_kernel(x_ref, oa_ref, ob_ref, nw_ref, win_ref, wa_ref, wb_ref, wo_ref, x1_ref,
                  wg_s, wa_s, wb_s, wo_s):
    @pl.when(pl.program_id(0) == 0)
    def _():
        wg_s[...] = win_ref[:, OFF_GATES:].astype(BF16)
        wa_s[...] = wa_ref[...].astype(BF16)
        wb_s[...] = wb_ref[...].astype(BF16)
        wo_s[...] = wo_ref[...].astype(BF16)

    d = x_ref.shape[-1]
    parts = _row_parts(x_ref)
    stage1 = []
    for r in parts:
        x = x_ref[r]
        h = (_rms(x, -1) * nw_ref[...]).astype(BF16)
        gates = jnp.dot(h, wg_s[...], preferred_element_type=F32)
        ya = jnp.dot(oa_ref[r], wa_s[...], preferred_element_type=F32)
        yb = jnp.dot(ob_ref[r], wb_s[...], preferred_element_type=F32)
        stage1.append((x, gates, ya, yb))
    for r, (x, gates, ya, yb) in zip(parts, stage1):
        y = jax.nn.sigmoid(gates[:, :d]) * ya + jax.nn.sigmoid(gates[:, d:]) * yb
        x1_ref[r] = x + jnp.dot(y.astype(BF16), wo_s[...], preferred_element_type=F32)


def _ffn_kernel(final, x_ref, nw_ref, wg_ref, wu_ref, wd_ref, nf_ref, o_ref):
    parts = _row_parts(x_ref)
    stage1 = []
    for r in parts:
        x = x_ref[r]
        h = (_rms(x, -1) * nw_ref[...]).astype(BF16)
        gate = jnp.dot(h, wg_ref[...], preferred_element_type=F32)
        up = jnp.dot(h, wu_ref[...], preferred_element_type=F32)
        stage1.append((x, gate, up))
    for r, (x, gate, up) in zip(parts, stage1):
        act = (jax.nn.silu(gate) * up).astype(BF16)
        x2 = x + jnp.dot(act, wd_ref[...], preferred_element_type=F32)
        if final:
            x2 = _rms(x2, -1) * nf_ref[...]
        o_ref[r] = x2


def _const_spec(shape):
    nd = len(shape)
    return pl.BlockSpec(shape, lambda *_: (0,) * nd, pipeline_mode=pl.Buffered(1))


def _rope_angles_t(pos, dim, theta):
    inv_freq = theta ** (-jnp.arange(0, dim, 2, dtype=F32) / dim)
    return (pos[:, None] * inv_freq[None, :]).T


def _angle_tables(seq_len):
    rows = seq_len // GRID_W
    row = jnp.broadcast_to(jnp.arange(rows, dtype=F32)[:, None], (rows, GRID_W)).reshape(-1)
    col = jnp.broadcast_to(jnp.arange(GRID_W, dtype=F32)[None, :], (rows, GRID_W)).reshape(-1)
    ang_a = jnp.concatenate([_rope_angles_t(row, HALF, AXIAL_THETA),
                             _rope_angles_t(col, HALF, AXIAL_THETA)], axis=0)
    ang_b = _rope_angles_t(jnp.arange(seq_len, dtype=F32), HEAD_DIM, ROPE_THETA)
    return jnp.cos(ang_a), jnp.sin(ang_a), jnp.cos(ang_b), jnp.sin(ang_b)


def _params(*sem):
    return pltpu.CompilerParams(dimension_semantics=sem, vmem_limit_bytes=VMEM_LIMIT)


def _layer(x, lambda_init, final, norm_mix, w_in, q_norm_a, k_norm_a, lq1, lk1, lq2, lk2, subln,
           w_proj_a, w_proj_b, w_out, norm_ffn, w_gate, w_up, w_down, norm_final, tables):
    b, s, d = x.shape
    d_ff = w_gate.shape[-1]
    n_tok = b * s
    n_chunks = s // PREP_TM
    ca, sa, cb, sb = tables
    row = lambda v: v.reshape(1, -1).astype(F32)
    col = lambda v: v.reshape(-1, 1).astype(F32)

    tab_spec = pl.BlockSpec((HALF, PREP_TM), lambda bi, ti: (0, ti))
    qta, ka, vta, qtb, kb, vtb, kn, qn = pl.pallas_call(
        _prep_kernel,
        grid=(b, n_chunks),
        in_specs=[
            pl.BlockSpec((1, PREP_TM, d), lambda bi, ti: (bi, ti, 0)),
            _const_spec((1, d)),
            _const_spec((d, OFF_GATES)),
            _const_spec((HEAD_DIM, 1)),
            _const_spec((HEAD_DIM, 1)),
            tab_spec, tab_spec, tab_spec, tab_spec,
        ],
        out_specs=[
            pl.BlockSpec((1, A_Q_HEADS, SLAB, PREP_TM), lambda bi, ti: (bi, 0, 0, ti)),
            pl.BlockSpec((1, A_KV_HEADS, PREP_TM, SLAB), lambda bi, ti: (bi, 0, ti, 0)),
            pl.BlockSpec((1, 1, A_KV_HEADS * HEAD_DIM, PREP_TM), lambda bi, ti: (bi, ti, 0, 0)),
            pl.BlockSpec((1, B_HEADS, 2, SLAB, PREP_TM), lambda bi, ti: (bi, 0, 0, 0, ti)),
            pl.BlockSpec((1, B_HEADS, 2, PREP_TM, SLAB), lambda bi, ti: (bi, 0, 0, ti, 0)),
            pl.BlockSpec((1, 1, B_WIDTH, PREP_TM), lambda bi, ti: (bi, ti, 0, 0)),
            pl.BlockSpec((1, 16, PREP_TM), lambda bi, ti: (bi, 0, ti)),
            pl.BlockSpec((1, 16, PREP_TM), lambda bi, ti: (bi, 0, ti)),
        ],
        out_shape=[
            jax.ShapeDtypeStruct((b, A_Q_HEADS, SLAB, s), BF16),
            jax.ShapeDtypeStruct((b, A_KV_HEADS, s, SLAB), BF16),
            jax.ShapeDtypeStruct((b, n_chunks, A_KV_HEADS * HEAD_DIM, PREP_TM), BF16),
            jax.ShapeDtypeStruct((b, B_HEADS, 2, SLAB, s), BF16),
            jax.ShapeDtypeStruct((b, B_HEADS, 2, s, SLAB), BF16),
            jax.ShapeDtypeStruct((b, n_chunks, B_WIDTH, PREP_TM), BF16),
            jax.ShapeDtypeStruct((b, 16, s), F32),
            jax.ShapeDtypeStruct((b, 16, s), F32),
        ],
        scratch_shapes=[pltpu.VMEM((d, OFF_GATES), BF16)],
        compiler_params=_params("arbitrary", "arbitrary"),
        name="prep",
    )(x, row(norm_mix), w_in, col(q_norm_a), col(k_norm_a), ca, sa, cb, sb)

    nq = s // ATTN_TQ
    steps_per_group = A_GROUP // A_STREAMS
    kmax_a = jnp.max(kn[:, :A_KV_HEADS], axis=-1)
    kmax_b = jnp.max(kn[:, 8:], axis=-1).reshape(b, B_HEADS, 2)
    qmax_a = jnp.max(qn[:, :A_Q_HEADS], axis=-1)
    qmax_b = jnp.max(qn[:, A_Q_HEADS:], axis=-1).reshape(b, B_HEADS, 2)
    fast_a = jnp.max(qmax_a * jnp.repeat(kmax_a, A_GROUP, axis=1)) < FAST_BOUND
    fast_b = jnp.max(qmax_b * kmax_b) < FAST_BOUND

    def general_scratch(n, dv):
        return [
            pltpu.VMEM((SCORE_SLOTS, n, PREP_TM, ATTN_TQ), F32),
            pltpu.VMEM((SCORE_SLOTS, n, 1, ATTN_TQ), F32),
            pltpu.VMEM((n, 1, ATTN_TQ), F32),
            pltpu.VMEM((n, 1, ATTN_TQ), F32),
            pltpu.VMEM((n, dv, ATTN_TQ), F32),
        ]

    def fast_scratch(n, dv):
        return [pltpu.VMEM((n, 1, ATTN_TQ), F32), pltpu.VMEM((n, dv, ATTN_TQ), F32)]

    a_specs = [
        pl.BlockSpec((1, A_STREAMS, SLAB, ATTN_TQ), lambda bi, hp, qi: (bi, hp, 0, qi)),
        pl.BlockSpec((1, 1, s, SLAB), lambda bi, hp, qi: (bi, hp // steps_per_group, 0, 0)),
        pl.BlockSpec((1, n_chunks, HEAD_DIM, PREP_TM),
                     lambda bi, hp, qi: (bi, 0, hp // steps_per_group, 0)),
    ]
    a_kmax_spec = pl.BlockSpec((1, 1, 1, ATTN_TQ), lambda bi, hp, qi: (bi, hp // steps_per_group, 0, 0))
    a_common = dict(
        grid=(b, A_Q_HEADS // A_STREAMS, nq),
        out_specs=pl.BlockSpec((1, ATTN_TQ, A_STREAMS * HEAD_DIM), lambda bi, hp, qi: (bi, qi, hp)),
        out_shape=jax.ShapeDtypeStruct((b, s, A_WIDTH), BF16),
        compiler_params=_params("parallel", "parallel", "arbitrary"),
    )

    def attn_a_fast():
        kmax = jnp.broadcast_to(kmax_a[:, :, None, None], (b, A_KV_HEADS, 1, ATTN_TQ))
        return pl.pallas_call(_attn_a_fast, in_specs=a_specs + [a_kmax_spec],
                              scratch_shapes=fast_scratch(A_STREAMS, HEAD_DIM), name="attn_a_fast",
                              **a_common)(qta, ka, vta, kmax)

    def attn_a_general():
        return pl.pallas_call(_attn_a_general, in_specs=a_specs,
                              scratch_shapes=general_scratch(A_STREAMS, HEAD_DIM), name="attn_a",
                              **a_common)(qta, ka, vta)

    oa = lax.cond(fast_a, attn_a_fast, attn_a_general)

    lam_spec = _const_spec((1, HEAD_DIM))
    hb = B_STEP_HEADS
    b_specs = [
        pl.BlockSpec((1, hb, 2, SLAB, ATTN_TQ), lambda bi, hp, qi: (bi, hp, 0, 0, qi)),
        pl.BlockSpec((1, hb, 2, s, SLAB), lambda bi, hp, qi: (bi, hp, 0, 0, 0)),
        pl.BlockSpec((1, n_chunks, hb * B_V_DIM, PREP_TM), lambda bi, hp, qi: (bi, 0, hp, 0)),
    ]
    b_kmax_spec = pl.BlockSpec((1, hb, 2, 1, ATTN_TQ), lambda bi, hp, qi: (bi, hp, 0, 0, 0))
    b_tail_specs = [lam_spec, lam_spec, lam_spec, lam_spec, _const_spec((B_V_DIM, 1))]
    b_tail = (row(lq1), row(lk1), row(lq2), row(lk2), col(subln))
    b_common = dict(
        grid=(b, B_HEADS // hb, nq),
        out_specs=pl.BlockSpec((1, ATTN_TQ, hb * B_V_DIM), lambda bi, hp, qi: (bi, qi, hp)),
        out_shape=jax.ShapeDtypeStruct((b, s, B_WIDTH), BF16),
        compiler_params=_params("parallel", "parallel", "arbitrary"),
    )

    def attn_b_fast():
        kmax = jnp.broadcast_to(kmax_b[:, :, :, None, None], (b, B_HEADS, 2, 1, ATTN_TQ))
        return pl.pallas_call(functools.partial(_attn_b_fast, lambda_init),
                              in_specs=b_specs + [b_kmax_spec] + b_tail_specs,
                              scratch_shapes=fast_scratch(2 * hb, B_V_DIM), name="attn_b_fast",
                              **b_common)(qtb, kb, vtb, kmax, *b_tail)

    def attn_b_general():
        return pl.pallas_call(functools.partial(_attn_b_general, lambda_init),
                              in_specs=b_specs + b_tail_specs,
                              scratch_shapes=general_scratch(2 * hb, B_V_DIM), name="attn_b",
                              **b_common)(qtb, kb, vtb, *b_tail)

    ob = lax.cond(fast_b, attn_b_fast, attn_b_general)

    tok_spec = lambda width: pl.BlockSpec((DENSE_TM, width), lambda ti: (ti, 0))
    x_flat = x.reshape(n_tok, d)
    x1 = pl.pallas_call(
        _merge_kernel,
        grid=(n_tok // DENSE_TM,),
        in_specs=[
            tok_spec(d), tok_spec(A_WIDTH), tok_spec(B_WIDTH),
            _const_spec((1, d)),
            _const_spec(w_in.shape),
            _const_spec((A_WIDTH, d)),
            _const_spec((B_WIDTH, d)),
            _const_spec((d, d)),
        ],
        out_specs=tok_spec(d),
        out_shape=jax.ShapeDtypeStruct((n_tok, d), F32),
        scratch_shapes=[pltpu.VMEM((d, 2 * d), BF16), pltpu.VMEM((A_WIDTH, d), BF16),
                        pltpu.VMEM((B_WIDTH, d), BF16), pltpu.VMEM((d, d), BF16)],
        compiler_params=_params("arbitrary"),
        name="merge",
    )(x_flat, oa.reshape(n_tok, A_WIDTH), ob.reshape(n_tok, B_WIDTH), row(norm_mix),
      w_in, w_proj_a, w_proj_b, w_out)

    x2 = pl.pallas_call(
        functools.partial(_ffn_kernel, final),
        grid=(n_tok // DENSE_TM,),
        in_specs=[
            tok_spec(d),
            _const_spec((1, d)),
            _const_spec((d, d_ff)),
            _const_spec((d, d_ff)),
            _const_spec((d_ff, d)),
            _const_spec((1, d)),
        ],
        out_specs=tok_spec(d),
        out_shape=jax.ShapeDtypeStruct((n_tok, d), F32),
        compiler_params=_params("parallel"),
        name="ffn",
    )(x1, row(norm_ffn), w_gate.astype(BF16), w_up.astype(BF16), w_down.astype(BF16),
      row(norm_final))
    return x2.reshape(b, s, d)


def kernel(x, norm_mix, w_in, q_norm_a, k_norm_a, lambda_q1, lambda_k1, lambda_q2, lambda_k2,
           subln_b, w_proj_a, w_proj_b, w_out, norm_ffn, w_gate_ffn, w_up_ffn, w_down_ffn,
           norm_final):
    depth = norm_mix.shape[0]
    tables = _angle_tables(x.shape[1])
    for l in range(depth):
        lambda_init = 0.8 - 0.6 * math.exp(-0.3 * l)
        x = _layer(x, lambda_init, l == depth - 1, norm_mix[l], w_in[l], q_norm_a[l], k_norm_a[l],
                   lambda_q1[l], lambda_k1[l], lambda_q2[l], lambda_k2[l], subln_b[l],
                   w_proj_a[l], w_proj_b[l], w_out[l], norm_ffn[l], w_gate_ffn[l], w_up_ffn[l],
                   w_down_ffn[l], norm_final, tables)
    return x
```

```python
import functools
import math

import jax
import jax.numpy as jnp
from jax import lax
from jax.experimental import pallas as pl
from jax.experimental.pallas import tpu as pltpu

F32 = jnp.float32
BF16 = jnp.bfloat16

GRID_W = 64
HEAD_DIM = 64
HALF = HEAD_DIM // 2
A_Q_HEADS = 8
A_KV_HEADS = 2
A_GROUP = A_Q_HEADS // A_KV_HEADS
A_WIDTH = A_Q_HEADS * HEAD_DIM
B_HEADS = 4
B_V_DIM = 2 * HEAD_DIM
B_WIDTH = B_HEADS * B_V_DIM
ROPE_THETA = 10000.0
AXIAL_THETA = 10000.0
NORM_EPS = 1e-6
QK_SCALE = math.log2(math.e) / math.sqrt(HEAD_DIM)
A_STREAMS = 4
B_STEP_HEADS = 2
SCORE_LAG = 2
SCORE_SLOTS = 3

OFF_AQ = 0
OFF_AK = OFF_AQ + A_WIDTH
OFF_AV = OFF_AK + A_KV_HEADS * HEAD_DIM
OFF_BQ = OFF_AV + A_KV_HEADS * HEAD_DIM
OFF_BK = OFF_BQ + B_HEADS * 2 * HEAD_DIM
OFF_BV = OFF_BK + B_HEADS * 2 * HEAD_DIM
OFF_GATES = OFF_BV + B_WIDTH
SLAB = 2 * HEAD_DIM
FAST_BOUND = 40.0

PREP_TM = 512
PREP_PARTS = 2
ATTN_TQ = PREP_TM
FAST_KC = 8192
FAST_ORDER_A = (256, 4)
FAST_ORDER_B = (4096, 2)
DENSE_TM = 512
DENSE_PARTS = 2
CAST_STEPS = 8
VMEM_LIMIT = 56 * 1024 * 1024


def _rms(x, axis):
    return x * lax.rsqrt(jnp.mean(x * x, axis=axis, keepdims=True) + NORM_EPS)


def _rope_t(xt, cos, sin):
    x1, x2 = xt[:HALF], xt[HALF:]
    return jnp.concatenate([x1 * cos - x2 * sin, x2 * cos + x1 * sin], axis=0)


def _prep_kernel(x_ref, nw_ref, w_ref, gq_ref, gk_ref, ca_ref, sa_ref, cb_ref, sb_ref,
                 qta_ref, ka_ref, vta_ref, qtb_ref, kb_ref, vtb_ref, kn_ref, qn_ref, wb_ref):
    @pl.when((pl.program_id(0) == 0) & (pl.program_id(1) == 0))
    def _():
        wb_ref[...] = w_ref[...].astype(BF16)

    tm = x_ref.shape[1] // PREP_PARTS
    gq, gk = gq_ref[...], gk_ref[...]
    pad = jnp.zeros((SLAB - HEAD_DIM - 8, tm), F32)
    one_row = (lax.broadcasted_iota(jnp.int32, (8, tm), 0) == 0).astype(F32)

    def norm(v):
        return jnp.sqrt(jnp.sum(v * v, axis=0, keepdims=True))

    def k_slab(k):
        return jnp.concatenate([k, one_row, pad], axis=0).T.astype(BF16)

    groups = [slice(i * tm, (i + 1) * tm) for i in range(PREP_PARTS)]
    zs = []
    for t in groups:
        h = _rms(x_ref[0, t], -1) * nw_ref[...]
        zs.append(jnp.dot(h.astype(BF16), wb_ref[...], preferred_element_type=F32))

    for t, z in zip(groups, zs):
        ca, sa, cb, sb = ca_ref[:, t], sa_ref[:, t], cb_ref[:, t], sb_ref[:, t]
        q_norms, k_norms = [], []

        def q_block(q):
            nq = norm(q).astype(BF16).astype(F32)
            q_norms.append(nq)
            return jnp.concatenate([q, jnp.broadcast_to(nq, (8, tm)), pad], axis=0).astype(BF16)

        aqt = z[:, OFF_AQ:OFF_AK].T
        for hd in range(A_Q_HEADS):
            q = aqt[hd * HEAD_DIM:(hd + 1) * HEAD_DIM]
            qta_ref[0, 0, hd, :, t] = q_block(_rope_t(_rms(q, 0) * gq, ca, sa) * QK_SCALE)
        akt = z[:, OFF_AK:OFF_AV].T
        for g in range(A_KV_HEADS):
            k = _rope_t(_rms(akt[g * HEAD_DIM:(g + 1) * HEAD_DIM], 0) * gk, ca, sa)
            ka_ref[0, g, t, :] = k_slab(k)
            k_norms.append(norm(k))
        k_norms.append(jnp.zeros((8 - A_KV_HEADS, tm), F32))
        vta_ref[0, 0, :, t] = z[:, OFF_AV:OFF_BQ].T.astype(BF16)

        bqt = z[:, OFF_BQ:OFF_BK].T
        bkt = z[:, OFF_BK:OFF_BV].T
        for hd in range(B_HEADS):
            for c in range(2):
                r0 = (hd * 2 + c) * HEAD_DIM
                qtb_ref[0, 0, hd, c, :, t] = q_block(_rope_t(bqt[r0:r0 + HEAD_DIM], cb, sb) * QK_SCALE)
                k = _rope_t(bkt[r0:r0 + HEAD_DIM], cb, sb)
                kb_ref[0, hd, c, t, :] = k_slab(k)
                k_norms.append(norm(k))
        vtb_ref[0, 0, :, t] = z[:, OFF_BV:OFF_GATES].T.astype(BF16)
        kn_ref[0, :, t] = jnp.concatenate(k_norms, axis=0)
        qn_ref[0, :, t] = jnp.concatenate(q_norms, axis=0)


def _plain_query(qt):
    qf = qt.astype(F32)
    rows = lax.broadcasted_iota(jnp.int32, qf.shape, 0)
    return jnp.where(rows < HEAD_DIM, qf, 0.0).astype(BF16)


def _shifted_query(qt, kmax):
    qf = qt.astype(F32)
    rows = lax.broadcasted_iota(jnp.int32, qf.shape, 0)
    shifted = jnp.where(rows == HEAD_DIM, -(qf * kmax), jnp.where(rows < HEAD_DIM, qf, 0.0))
    return shifted.astype(BF16)


def _flash_fast(q_list, k_of, vt_ref, v_rows, l_ref, acc_ref, sb, skew):
    n = len(q_list)
    vb = vt_ref.shape[3]
    pb = min(sb, vb)
    l_ref[...] = jnp.zeros(l_ref.shape, F32)
    acc_ref[...] = jnp.zeros(acc_ref.shape, F32)

    def body(i, carry):
        off = pl.multiple_of(i * FAST_KC, FAST_KC)
        l = [l_ref[j] for j in range(n)]
        acc = [acc_ref[j] for j in range(n)]
        items = [(g, j) for g in range(FAST_KC // sb) for j in range(n)]
        scores = {}

        def consume(g, j):
            p = jnp.exp2(scores.pop((g, j)))
            l[j] = l[j] + jnp.sum(p, axis=0, keepdims=True)
            p = p.astype(BF16)
            for u in range(sb // pb):
                k0 = g * sb + u * pb
                vblk = vt_ref[0, i * (FAST_KC // vb) + k0 // vb, v_rows(j), k0 % vb:k0 % vb + pb]
                acc[j] = acc[j] + jnp.dot(vblk, p[u * pb:(u + 1) * pb], preferred_element_type=F32)

        for t, (g, j) in enumerate(items):
            scores[(g, j)] = jnp.dot(k_of(j, off + g * sb, sb), q_list[j],
                                     preferred_element_type=F32)
            if t >= skew:
                consume(*items[t - skew])
        for g, j in items[max(len(items) - skew, 0):]:
            consume(g, j)
        for j in range(n):
            l_ref[j], acc_ref[j] = l[j], acc[j]
        return carry

    lax.fori_loop(0, vt_ref.shape[1] * vb // FAST_KC, body, 0)


def _flash_streams(q_list, k_of, vt_ref, v_rows, s_buf, mx_buf, m_ref, l_ref, acc_ref):
    n = len(q_list)
    n_chunks, kc = vt_ref.shape[1], vt_ref.shape[3]
    m_ref[...] = jnp.full(m_ref.shape, -jnp.inf, F32)
    l_ref[...] = jnp.zeros(l_ref.shape, F32)
    acc_ref[...] = jnp.zeros(acc_ref.shape, F32)

    def scores(c, slot):
        off = c * kc if isinstance(c, int) else pl.multiple_of(c * kc, kc)
        for j in range(n):
            s = jnp.dot(k_of(j, off, kc), q_list[j], preferred_element_type=F32)
            s_buf[slot, j] = s
            mx_buf[slot, j] = jnp.max(s, axis=0, keepdims=True)

    def softmax_pv(c, slot):
        for j in range(n):
            vblk = vt_ref[0, c, v_rows(j)]
            m_old = m_ref[j]
            m_new = jnp.maximum(m_old, mx_buf[slot, j])
            alpha = jnp.exp2(m_old - m_new)
            p = jnp.exp2(s_buf[slot, j] - m_new)
            l_ref[j] = alpha * l_ref[j] + jnp.sum(p, axis=0, keepdims=True)
            acc_ref[j] = alpha * acc_ref[j] + jnp.dot(vblk, p.astype(BF16),
                                                      preferred_element_type=F32)
            m_ref[j] = m_new

    n_slots = s_buf.shape[0]

    def stage(c, u):
        if not isinstance(c, int) or c + SCORE_LAG < n_chunks:
            scores(c + SCORE_LAG, (u + SCORE_LAG) % n_slots)
        softmax_pv(c, u)

    for c in range(SCORE_LAG):
        scores(c, c % n_slots)
    n_loop = (n_chunks - SCORE_LAG) // n_slots

    def body(i, carry):
        for u in range(n_slots):
            stage(i * n_slots + u, u)
        return carry

    lax.fori_loop(0, n_loop, body, 0)
    for c in range(n_loop * n_slots, n_chunks):
        stage(c, c % n_slots)


def _finish_a(o_ref, l_ref, acc_ref):
    outs = [acc_ref[j] * (1.0 / l_ref[j]) for j in range(A_STREAMS)]
    o_ref[0] = jnp.concatenate(outs, axis=0).T.astype(BF16)


def _finish_b(lambda_init, lam_refs, sub_ref, o_ref, l_ref, acc_ref):
    lq1_ref, lk1_ref, lq2_ref, lk2_ref = lam_refs
    lam = (jnp.exp(jnp.sum(lq1_ref[...] * lk1_ref[...], axis=-1, keepdims=True))
           - jnp.exp(jnp.sum(lq2_ref[...] * lk2_ref[...], axis=-1, keepdims=True))
           + lambda_init)
    outs = []
    for hd in range(B_STEP_HEADS):
        j1, j2 = 2 * hd, 2 * hd + 1
        o = acc_ref[j1] * (1.0 / l_ref[j1]) - lam * (acc_ref[j2] * (1.0 / l_ref[j2]))
        outs.append(_rms(o, 0) * sub_ref[...] * (1.0 - lambda_init))
    o_ref[0] = jnp.concatenate(outs, axis=0).T.astype(BF16)


def _all_rows(j):
    return slice(None)


def _b_rows(j):
    return slice((j // 2) * B_V_DIM, (j // 2 + 1) * B_V_DIM)


def _k_of_a(k_ref):
    return lambda j, off, size: k_ref[0, 0, pl.ds(off, size), :]


def _k_of_b(k_ref):
    return lambda j, off, size: k_ref[0, j // 2, j % 2, pl.ds(off, size), :]


def _attn_a_general(qt_ref, k_ref, vt_ref, o_ref, s_buf, mx_buf, m_ref, l_ref, acc_ref):
    q_list = [_plain_query(qt_ref[0, 0, j]) for j in range(A_STREAMS)]
    _flash_streams(q_list, _k_of_a(k_ref), vt_ref, _all_rows, s_buf, mx_buf, m_ref, l_ref, acc_ref)
    _finish_a(o_ref, l_ref, acc_ref)


def _attn_a_fast(qt_ref, k_ref, vt_ref, kmax_ref, o_ref, l_ref, acc_ref):
    q_list = [_shifted_query(qt_ref[0, 0, j], kmax_ref[0, 0]) for j in range(A_STREAMS)]
    _flash_fast(q_list, _k_of_a(k_ref), vt_ref, _all_rows, l_ref, acc_ref, *FAST_ORDER_A)
    _finish_a(o_ref, l_ref, acc_ref)


def _attn_b_general(lambda_init, qt_ref, k_ref, vt_ref, lq1_ref, lk1_ref, lq2_ref, lk2_ref,
                    sub_ref, o_ref, s_buf, mx_buf, m_ref, l_ref, acc_ref):
    q_list = [_plain_query(qt_ref[0, 0, hd, c]) for hd in range(B_STEP_HEADS) for c in range(2)]
    _flash_streams(q_list, _k_of_b(k_ref), vt_ref, _b_rows, s_buf, mx_buf, m_ref, l_ref, acc_ref)
    _finish_b(lambda_init, (lq1_ref, lk1_ref, lq2_ref, lk2_ref), sub_ref, o_ref, l_ref, acc_ref)


def _attn_b_fast(lambda_init, qt_ref, k_ref, vt_ref, kmax_ref, lq1_ref, lk1_ref, lq2_ref, lk2_ref,
                 sub_ref, o_ref, l_ref, acc_ref):
    q_list = [_shifted_query(qt_ref[0, 0, hd, c], kmax_ref[0, hd, c])
              for hd in range(B_STEP_HEADS) for c in range(2)]
    _flash_fast(q_list, _k_of_b(k_ref), vt_ref, _b_rows, l_ref, acc_ref, *FAST_ORDER_B)
    _finish_b(lambda_init, (lq1_ref, lk1_ref, lq2_ref, lk2_ref), sub_ref, o_ref, l_ref, acc_ref)


def _row_parts(ref):
    rows = ref.shape[0] // DENSE_PARTS
    return [slice(i * rows, (i + 1) * rows) for i in range(DENSE_PARTS)]


def _merge_kernel(x_ref, oa_ref, ob_ref, nw_ref, win_ref, wa_ref, wb_ref, wo_ref, x1_ref,
                  wg_s, wa_s, wb_s, wo_s):
    @pl.when(pl.program_id(0) == 0)
    def _():
        wg_s[...] = win_ref[:, OFF_GATES:].astype(BF16)
        wa_s[...] = wa_ref[...].astype(BF16)
        wb_s[...] = wb_ref[...].astype(BF16)
        wo_s[...] = wo_ref[...].astype(BF16)

    d = x_ref.shape[-1]
    parts = _row_parts(x_ref)
    stage1 = []
    for r in parts:
        x = x_ref[r]
        h = (_rms(x, -1) * nw_ref[...]).astype(BF16)
        gates = jnp.dot(h, wg_s[...], preferred_element_type=F32)
        ya = jnp.dot(oa_ref[r], wa_s[...], preferred_element_type=F32)
        yb = jnp.dot(ob_ref[r], wb_s[...], preferred_element_type=F32)
        stage1.append((x, gates, ya, yb))
    for r, (x, gates, ya, yb) in zip(parts, stage1):
        y = jax.nn.sigmoid(gates[:, :d]) * ya + jax.nn.sigmoid(gates[:, d:]) * yb
        x1_ref[r] = x + jnp.dot(y.astype(BF16), wo_s[...], preferred_element_type=F32)


def _cast_kernel(*refs):
    n = len(refs) // 2
    for src, dst in zip(refs[:n], refs[n:]):
        dst[...] = src[...].astype(dst.dtype)


def _ffn_kernel(final, x_ref, nw_ref, wg_ref, wu_ref, wd_ref, nf_ref, o_ref):
    parts = _row_parts(x_ref)
    stage1 = []
    for r in parts:
        x = x_ref[r]
        h = (_rms(x, -1) * nw_ref[...]).astype(BF16)
        gate = jnp.dot(h, wg_ref[...], preferred_element_type=F32)
        up = jnp.dot(h, wu_ref[...], preferred_element_type=F32)
        stage1.append((x, gate, up))
    for r, (x, gate, up) in zip(parts, stage1):
        act = (jax.nn.silu(gate) * up).astype(BF16)
        x2 = x + jnp.dot(act, wd_ref[...], preferred_element_type=F32)
        if final:
            x2 = _rms(x2, -1) * nf_ref[...]
        o_ref[r] = x2


def _const_spec(shape):
    nd = len(shape)
    return pl.BlockSpec(shape, lambda *_: (0,) * nd, pipeline_mode=pl.Buffered(1))


def _rope_angles_t(pos, dim, theta):
    inv_freq = theta ** (-jnp.arange(0, dim, 2, dtype=F32) / dim)
    return (pos[:, None] * inv_freq[None, :]).T


def _angle_tables(seq_len):
    rows = seq_len // GRID_W
    row = jnp.broadcast_to(jnp.arange(rows, dtype=F32)[:, None], (rows, GRID_W)).reshape(-1)
    col = jnp.broadcast_to(jnp.arange(GRID_W, dtype=F32)[None, :], (rows, GRID_W)).reshape(-1)
    ang_a = jnp.concatenate([_rope_angles_t(row, HALF, AXIAL_THETA),
                             _rope_angles_t(col, HALF, AXIAL_THETA)], axis=0)
    ang_b = _rope_angles_t(jnp.arange(seq_len, dtype=F32), HEAD_DIM, ROPE_THETA)
    return jnp.cos(ang_a), jnp.sin(ang_a), jnp.cos(ang_b), jnp.sin(ang_b)


def _params(*sem):
    return pltpu.CompilerParams(dimension_semantics=sem, vmem_limit_bytes=VMEM_LIMIT)


def _layer(x, lambda_init, final, norm_mix, w_in, q_norm_a, k_norm_a, lq1, lk1, lq2, lk2, subln,
           w_proj_a, w_proj_b, w_out, norm_ffn, w_gate, w_up, w_down, norm_final, tables):
    b, s, d = x.shape
    d_ff = w_gate.shape[-1]
    n_tok = b * s
    n_chunks = s // PREP_TM
    ca, sa, cb, sb = tables
    row = lambda v: v.reshape(1, -1).astype(F32)
    col = lambda v: v.reshape(-1, 1).astype(F32)

    tab_spec = pl.BlockSpec((HALF, PREP_TM), lambda bi, ti: (0, ti))
    qta, ka, vta, qtb, kb, vtb, kn, qn = pl.pallas_call(
        _prep_kernel,
        grid=(b, n_chunks),
        in_specs=[
            pl.BlockSpec((1, PREP_TM, d), lambda bi, ti: (bi, ti, 0)),
            _const_spec((1, d)),
            _const_spec((d, OFF_GATES)),
            _const_spec((HEAD_DIM, 1)),
            _const_spec((HEAD_DIM, 1)),
            tab_spec, tab_spec, tab_spec, tab_spec,
        ],
        out_specs=[
            pl.BlockSpec((1, 1, A_Q_HEADS, SLAB, PREP_TM), lambda bi, ti: (bi, ti, 0, 0, 0)),
            pl.BlockSpec((1, A_KV_HEADS, PREP_TM, SLAB), lambda bi, ti: (bi, 0, ti, 0)),
            pl.BlockSpec((1, 1, A_KV_HEADS * HEAD_DIM, PREP_TM), lambda bi, ti: (bi, ti, 0, 0)),
            pl.BlockSpec((1, 1, B_HEADS, 2, SLAB, PREP_TM), lambda bi, ti: (bi, ti, 0, 0, 0, 0)),
            pl.BlockSpec((1, B_HEADS, 2, PREP_TM, SLAB), lambda bi, ti: (bi, 0, 0, ti, 0)),
            pl.BlockSpec((1, 1, B_WIDTH, PREP_TM), lambda bi, ti: (bi, ti, 0, 0)),
            pl.BlockSpec((1, 16, PREP_TM), lambda bi, ti: (bi, 0, ti)),
            pl.BlockSpec((1, 16, PREP_TM), lambda bi, ti: (bi, 0, ti)),
        ],
        out_shape=[
            jax.ShapeDtypeStruct((b, n_chunks, A_Q_HEADS, SLAB, PREP_TM), BF16),
            jax.ShapeDtypeStruct((b, A_KV_HEADS, s, SLAB), BF16),
            jax.ShapeDtypeStruct((b, n_chunks, A_KV_HEADS * HEAD_DIM, PREP_TM), BF16),
            jax.ShapeDtypeStruct((b, n_chunks, B_HEADS, 2, SLAB, PREP_TM), BF16),
            jax.ShapeDtypeStruct((b, B_HEADS, 2, s, SLAB), BF16),
            jax.ShapeDtypeStruct((b, n_chunks, B_WIDTH, PREP_TM), BF16),
            jax.ShapeDtypeStruct((b, 16, s), F32),
            jax.ShapeDtypeStruct((b, 16, s), F32),
        ],
        scratch_shapes=[pltpu.VMEM((d, OFF_GATES), BF16)],
        compiler_params=_params("arbitrary", "arbitrary"),
        name="prep",
    )(x, row(norm_mix), w_in, col(q_norm_a), col(k_norm_a), ca, sa, cb, sb)

    nq = s // ATTN_TQ
    steps_per_group = A_GROUP // A_STREAMS
    kmax_a = jnp.max(kn[:, :A_KV_HEADS], axis=-1)
    kmax_b = jnp.max(kn[:, 8:], axis=-1).reshape(b, B_HEADS, 2)
    qmax_a = jnp.max(qn[:, :A_Q_HEADS], axis=-1)
    qmax_b = jnp.max(qn[:, A_Q_HEADS:], axis=-1).reshape(b, B_HEADS, 2)
    fast_a = jnp.max(qmax_a * jnp.repeat(kmax_a, A_GROUP, axis=1)) < FAST_BOUND
    fast_b = jnp.max(qmax_b * kmax_b) < FAST_BOUND

    def general_scratch(n, dv):
        return [
            pltpu.VMEM((SCORE_SLOTS, n, PREP_TM, ATTN_TQ), F32),
            pltpu.VMEM((SCORE_SLOTS, n, 1, ATTN_TQ), F32),
            pltpu.VMEM((n, 1, ATTN_TQ), F32),
            pltpu.VMEM((n, 1, ATTN_TQ), F32),
            pltpu.VMEM((n, dv, ATTN_TQ), F32),
        ]

    def fast_scratch(n, dv):
        return [pltpu.VMEM((n, 1, ATTN_TQ), F32), pltpu.VMEM((n, dv, ATTN_TQ), F32)]

    a_specs = [
        pl.BlockSpec((1, 1, A_STREAMS, SLAB, ATTN_TQ), lambda bi, hp, qi: (bi, qi, hp, 0, 0)),
        pl.BlockSpec((1, 1, s, SLAB), lambda bi, hp, qi: (bi, hp // steps_per_group, 0, 0)),
        pl.BlockSpec((1, n_chunks, HEAD_DIM, PREP_TM),
                     lambda bi, hp, qi: (bi, 0, hp // steps_per_group, 0)),
    ]
    a_kmax_spec = pl.BlockSpec((1, 1, 1, ATTN_TQ), lambda bi, hp, qi: (bi, hp // steps_per_group, 0, 0))
    a_common = dict(
        grid=(b, A_Q_HEADS // A_STREAMS, nq),
        out_specs=pl.BlockSpec((1, ATTN_TQ, A_STREAMS * HEAD_DIM), lambda bi, hp, qi: (bi, qi, hp)),
        out_shape=jax.ShapeDtypeStruct((b, s, A_WIDTH), BF16),
        compiler_params=_params("parallel", "parallel", "arbitrary"),
    )

    def attn_a_fast():
        kmax = jnp.broadcast_to(kmax_a[:, :, None, None], (b, A_KV_HEADS, 1, ATTN_TQ))
        return pl.pallas_call(_attn_a_fast, in_specs=a_specs + [a_kmax_spec],
                              scratch_shapes=fast_scratch(A_STREAMS, HEAD_DIM), name="attn_a_fast",
                              **a_common)(qta, ka, vta, kmax)

    def attn_a_general():
        return pl.pallas_call(_attn_a_general, in_specs=a_specs,
                              scratch_shapes=general_scratch(A_STREAMS, HEAD_DIM), name="attn_a",
                              **a_common)(qta, ka, vta)

    oa = lax.cond(fast_a, attn_a_fast, attn_a_general)

    lam_spec = _const_spec((1, HEAD_DIM))
    hb = B_STEP_HEADS
    b_specs = [
        pl.BlockSpec((1, 1, hb, 2, SLAB, ATTN_TQ), lambda bi, hp, qi: (bi, qi, hp, 0, 0, 0)),
        pl.BlockSpec((1, hb, 2, s, SLAB), lambda bi, hp, qi: (bi, hp, 0, 0, 0)),
        pl.BlockSpec((1, n_chunks, hb * B_V_DIM, PREP_TM), lambda bi, hp, qi: (bi, 0, hp, 0)),
    ]
    b_kmax_spec = pl.BlockSpec((1, hb, 2, 1, ATTN_TQ), lambda bi, hp, qi: (bi, hp, 0, 0, 0))
    b_tail_specs = [lam_spec, lam_spec, lam_spec, lam_spec, _const_spec((B_V_DIM, 1))]
    b_tail = (row(lq1), row(lk1), row(lq2), row(lk2), col(subln))
    b_common = dict(
        grid=(b, B_HEADS // hb, nq),
        out_specs=pl.BlockSpec((1, ATTN_TQ, hb * B_V_DIM), lambda bi, hp, qi: (bi, qi, hp)),
        out_shape=jax.ShapeDtypeStruct((b, s, B_WIDTH), BF16),
        compiler_params=_params("parallel", "parallel", "arbitrary"),
    )

    def attn_b_fast():
        kmax = jnp.broadcast_to(kmax_b[:, :, :, None, None], (b, B_HEADS, 2, 1, ATTN_TQ))
        return pl.pallas_call(functools.partial(_attn_b_fast, lambda_init),
                              in_specs=b_specs + [b_kmax_spec] + b_tail_specs,
                              scratch_shapes=fast_scratch(2 * hb, B_V_DIM), name="attn_b_fast",
                              **b_common)(qtb, kb, vtb, kmax, *b_tail)

    def attn_b_general():
        return pl.pallas_call(functools.partial(_attn_b_general, lambda_init),
                              in_specs=b_specs + b_tail_specs,
                              scratch_shapes=general_scratch(2 * hb, B_V_DIM), name="attn_b",
                              **b_common)(qtb, kb, vtb, *b_tail)

    ob = lax.cond(fast_b, attn_b_fast, attn_b_general)

    tok_spec = lambda width: pl.BlockSpec((DENSE_TM, width), lambda ti: (ti, 0))
    x_flat = x.reshape(n_tok, d)
    x1 = pl.pallas_call(
        _merge_kernel,
        grid=(n_tok // DENSE_TM,),
        in_specs=[
            tok_spec(d), tok_spec(A_WIDTH), tok_spec(B_WIDTH),
            _const_spec((1, d)),
            _const_spec(w_in.shape),
            _const_spec((A_WIDTH, d)),
            _const_spec((B_WIDTH, d)),
            _const_spec((d, d)),
        ],
        out_specs=tok_spec(d),
        out_shape=jax.ShapeDtypeStruct((n_tok, d), F32),
        scratch_shapes=[pltpu.VMEM((d, 2 * d), BF16), pltpu.VMEM((A_WIDTH, d), BF16),
                        pltpu.VMEM((B_WIDTH, d), BF16), pltpu.VMEM((d, d), BF16)],
        compiler_params=_params("arbitrary"),
        name="merge",
    )(x_flat, oa.reshape(n_tok, A_WIDTH), ob.reshape(n_tok, B_WIDTH), row(norm_mix),
      w_in, w_proj_a, w_proj_b, w_out)

    ffn_shapes = [w_gate.shape, w_up.shape, w_down.shape]
    ffn_specs = [pl.BlockSpec((sh[0] // CAST_STEPS, sh[1]), lambda i: (i, 0)) for sh in ffn_shapes]
    ffn_w = pl.pallas_call(
        _cast_kernel,
        grid=(CAST_STEPS,),
        in_specs=ffn_specs,
        out_specs=ffn_specs,
        out_shape=[jax.ShapeDtypeStruct(sh, BF16) for sh in ffn_shapes],
        compiler_params=_params("parallel"),
        name="ffn_weights",
    )(w_gate, w_up, w_down)

    x2 = pl.pallas_call(
        functools.partial(_ffn_kernel, final),
        grid=(n_tok // DENSE_TM,),
        in_specs=[
            tok_spec(d),
            _const_spec((1, d)),
            _const_spec((d, d_ff)),
            _const_spec((d, d_ff)),
            _const_spec((d_ff, d)),
            _const_spec((1, d)),
        ],
        out_specs=tok_spec(d),
        out_shape=jax.ShapeDtypeStruct((n_tok, d), F32),
        compiler_params=_params("parallel"),
        name="ffn",
    )(x1, row(norm_ffn), *ffn_w, row(norm_final))
    return x2.reshape(b, s, d)


def kernel(x, norm_mix, w_in, q_norm_a, k_norm_a, lambda_q1, lambda_k1, lambda_q2, lambda_k2,
           subln_b, w_proj_a, w_proj_b, w_out, norm_ffn, w_gate_ffn, w_up_ffn, w_down_ffn,
           norm_final):
    depth = norm_mix.shape[0]
    tables = _angle_tables(x.shape[1])
    for l in range(depth):
        lambda_init = 0.8 - 0.6 * math.exp(-0.3 * l)
        x = _layer(x, lambda_init, l == depth - 1, norm_mix[l], w_in[l], q_norm_a[l], k_norm_a[l],
                   lambda_q1[l], lambda_k1[l], lambda_q2[l], lambda_k2[l], subln_b[l],
                   w_proj_a[l], w_proj_b[l], w_out[l], norm_ffn[l], w_gate_ffn[l], w_up_ffn[l],
                   w_down_ffn[l], norm_final, tables)
    return x
```

```python
import functools
import math

import jax
import jax.numpy as jnp
from jax import lax
from jax.experimental import pallas as pl
from jax.experimental.pallas import tpu as pltpu

F32 = jnp.float32
BF16 = jnp.bfloat16

GRID_W = 64
HEAD_DIM = 64
HALF = HEAD_DIM // 2
A_Q_HEADS = 8
A_KV_HEADS = 2
A_GROUP = A_Q_HEADS // A_KV_HEADS
A_WIDTH = A_Q_HEADS * HEAD_DIM
B_HEADS = 4
B_V_DIM = 2 * HEAD_DIM
B_WIDTH = B_HEADS * B_V_DIM
ROPE_THETA = 10000.0
AXIAL_THETA = 10000.0
NORM_EPS = 1e-6
QK_SCALE = math.log2(math.e) / math.sqrt(HEAD_DIM)
A_STREAMS = 4
B_STEP_HEADS = 2
SCORE_LAG = 2
SCORE_SLOTS = 3

OFF_AQ = 0
OFF_AK = OFF_AQ + A_WIDTH
OFF_AV = OFF_AK + A_KV_HEADS * HEAD_DIM
OFF_BQ = OFF_AV + A_KV_HEADS * HEAD_DIM
OFF_BK = OFF_BQ + B_HEADS * 2 * HEAD_DIM
OFF_BV = OFF_BK + B_HEADS * 2 * HEAD_DIM
OFF_GATES = OFF_BV + B_WIDTH
SLAB = 2 * HEAD_DIM
FAST_BOUND = 40.0

PREP_TM = 512
PREP_PARTS = 2
ATTN_TQ = PREP_TM
FAST_ORDER_A = (256, 4)
FAST_ORDER_B = (4096, 2)
DENSE_TM = 512
DENSE_PARTS = 2
CAST_STEPS = 8
VMEM_LIMIT = 56 * 1024 * 1024


def _rms(x, axis):
    return x * lax.rsqrt(jnp.mean(x * x, axis=axis, keepdims=True) + NORM_EPS)


def _rope_t(xt, cos, sin):
    x1, x2 = xt[:HALF], xt[HALF:]
    return jnp.concatenate([x1 * cos - x2 * sin, x2 * cos + x1 * sin], axis=0)


def _prep_kernel(x_ref, nw_ref, w_ref, gq_ref, gk_ref, ca_ref, sa_ref, cb_ref, sb_ref,
                 qta_ref, ka_ref, vta_ref, qtb_ref, kb_ref, vtb_ref, kn_ref, qn_ref, wb_ref):
    @pl.when((pl.program_id(0) == 0) & (pl.program_id(1) == 0))
    def _():
        wb_ref[...] = w_ref[...].astype(BF16)

    tm = x_ref.shape[1] // PREP_PARTS
    gq, gk = gq_ref[...], gk_ref[...]
    pad = jnp.zeros((SLAB - HEAD_DIM - 8, tm), F32)
    one_row = (lax.broadcasted_iota(jnp.int32, (8, tm), 0) == 0).astype(F32)

    def norm(v):
        return jnp.sqrt(jnp.sum(v * v, axis=0, keepdims=True))

    def k_slab(k):
        return jnp.concatenate([k, one_row, pad], axis=0).T.astype(BF16)

    groups = [slice(i * tm, (i + 1) * tm) for i in range(PREP_PARTS)]
    zs = []
    for t in groups:
        h = _rms(x_ref[0, t], -1) * nw_ref[...]
        zs.append(jnp.dot(h.astype(BF16), wb_ref[...], preferred_element_type=F32))

    for t, z in zip(groups, zs):
        ca, sa, cb, sb = ca_ref[:, t], sa_ref[:, t], cb_ref[:, t], sb_ref[:, t]
        q_norms, k_norms = [], []

        def q_block(q):
            nq = norm(q).astype(BF16).astype(F32)
            q_norms.append(nq)
            return jnp.concatenate([q, jnp.broadcast_to(nq, (8, tm)), pad], axis=0).astype(BF16)

        aqt = z[:, OFF_AQ:OFF_AK].T
        for hd in range(A_Q_HEADS):
            q = aqt[hd * HEAD_DIM:(hd + 1) * HEAD_DIM]
            qta_ref[0, 0, hd, :, t] = q_block(_rope_t(_rms(q, 0) * gq, ca, sa) * QK_SCALE)
        akt = z[:, OFF_AK:OFF_AV].T
        for g in range(A_KV_HEADS):
            k = _rope_t(_rms(akt[g * HEAD_DIM:(g + 1) * HEAD_DIM], 0) * gk, ca, sa)
            ka_ref[0, g, t, :] = k_slab(k)
            k_norms.append(norm(k))
        k_norms.append(jnp.zeros((8 - A_KV_HEADS, tm), F32))
        vta_ref[0, 0, :, t] = z[:, OFF_AV:OFF_BQ].T.astype(BF16)

        bqt = z[:, OFF_BQ:OFF_BK].T
        bkt = z[:, OFF_BK:OFF_BV].T
        for hd in range(B_HEADS):
            for c in range(2):
                r0 = (hd * 2 + c) * HEAD_DIM
                qtb_ref[0, 0, hd, c, :, t] = q_block(_rope_t(bqt[r0:r0 + HEAD_DIM], cb, sb) * QK_SCALE)
                k = _rope_t(bkt[r0:r0 + HEAD_DIM], cb, sb)
                kb_ref[0, hd, c, t, :] = k_slab(k)
                k_norms.append(norm(k))
        vtb_ref[0, 0, :, t] = z[:, OFF_BV:OFF_GATES].T.astype(BF16)
        kn_ref[0, :, t] = jnp.concatenate(k_norms, axis=0)
        qn_ref[0, :, t] = jnp.concatenate(q_norms, axis=0)


def _plain_query(qt):
    qf = qt.astype(F32)
    rows = lax.broadcasted_iota(jnp.int32, qf.shape, 0)
    return jnp.where(rows < HEAD_DIM, qf, 0.0).astype(BF16)


def _shifted_query(qt, kmax):
    qf = qt.astype(F32)
    rows = lax.broadcasted_iota(jnp.int32, qf.shape, 0)
    shifted = jnp.where(rows == HEAD_DIM, -(qf * kmax), jnp.where(rows < HEAD_DIM, qf, 0.0))
    return shifted.astype(BF16)


def _flash_fast(q_list, k_of, vt_ref, v_rows, sb, skew):
    n = len(q_list)
    vb = vt_ref.shape[3]
    pb = min(sb, vb)
    l, acc = [None] * n, [None] * n
    items = [(g, j) for g in range(vt_ref.shape[1] * vb // sb) for j in range(n)]
    scores = {}

    def consume(g, j):
        p = jnp.exp2(scores.pop((g, j)))
        row_sum = jnp.sum(p, axis=0, keepdims=True)
        l[j] = row_sum if l[j] is None else l[j] + row_sum
        p = p.astype(BF16)
        for u in range(sb // pb):
            k0 = g * sb + u * pb
            vblk = vt_ref[0, k0 // vb, v_rows(j), k0 % vb:k0 % vb + pb]
            pv = jnp.dot(vblk, p[u * pb:(u + 1) * pb], preferred_element_type=F32)
            acc[j] = pv if acc[j] is None else acc[j] + pv

    for t, (g, j) in enumerate(items):
        scores[(g, j)] = jnp.dot(k_of(j, g * sb, sb), q_list[j], preferred_element_type=F32)
        if t >= skew:
            consume(*items[t - skew])
    for g, j in items[max(len(items) - skew, 0):]:
        consume(g, j)
    return l, acc


def _flash_streams(q_list, k_of, vt_ref, v_rows, s_buf, mx_buf, m_ref, l_ref, acc_ref):
    n = len(q_list)
    n_chunks, kc = vt_ref.shape[1], vt_ref.shape[3]
    m_ref[...] = jnp.full(m_ref.shape, -jnp.inf, F32)
    l_ref[...] = jnp.zeros(l_ref.shape, F32)
    acc_ref[...] = jnp.zeros(acc_ref.shape, F32)

    def scores(c, slot):
        off = c * kc if isinstance(c, int) else pl.multiple_of(c * kc, kc)
        for j in range(n):
            s = jnp.dot(k_of(j, off, kc), q_list[j], preferred_element_type=F32)
            s_buf[slot, j] = s
            mx_buf[slot, j] = jnp.max(s, axis=0, keepdims=True)

    def softmax_pv(c, slot):
        for j in range(n):
            vblk = vt_ref[0, c, v_rows(j)]
            m_old = m_ref[j]
            m_new = jnp.maximum(m_old, mx_buf[slot, j])
            alpha = jnp.exp2(m_old - m_new)
            p = jnp.exp2(s_buf[slot, j] - m_new)
            l_ref[j] = alpha * l_ref[j] + jnp.sum(p, axis=0, keepdims=True)
            acc_ref[j] = alpha * acc_ref[j] + jnp.dot(vblk, p.astype(BF16),
                                                      preferred_element_type=F32)
            m_ref[j] = m_new

    n_slots = s_buf.shape[0]

    def stage(c, u):
        if not isinstance(c, int) or c + SCORE_LAG < n_chunks:
            scores(c + SCORE_LAG, (u + SCORE_LAG) % n_slots)
        softmax_pv(c, u)

    for c in range(SCORE_LAG):
        scores(c, c % n_slots)
    n_loop = (n_chunks - SCORE_LAG) // n_slots

    def body(i, carry):
        for u in range(n_slots):
            stage(i * n_slots + u, u)
        return carry

    lax.fori_loop(0, n_loop, body, 0)
    for c in range(n_loop * n_slots, n_chunks):
        stage(c, c % n_slots)


def _finish_a(o_ref, l, acc):
    outs = [acc[j] * (1.0 / l[j]) for j in range(A_STREAMS)]
    o_ref[0] = jnp.concatenate(outs, axis=0).T.astype(BF16)


def _finish_b(lambda_init, lam_refs, sub_ref, o_ref, l, acc):
    lq1_ref, lk1_ref, lq2_ref, lk2_ref = lam_refs
    lam = (jnp.exp(jnp.sum(lq1_ref[...] * lk1_ref[...], axis=-1, keepdims=True))
           - jnp.exp(jnp.sum(lq2_ref[...] * lk2_ref[...], axis=-1, keepdims=True))
           + lambda_init)
    outs = []
    for hd in range(B_STEP_HEADS):
        j1, j2 = 2 * hd, 2 * hd + 1
        o = acc[j1] * (1.0 / l[j1]) - lam * (acc[j2] * (1.0 / l[j2]))
        outs.append(_rms(o, 0) * sub_ref[...] * (1.0 - lambda_init))
    o_ref[0] = jnp.concatenate(outs, axis=0).T.astype(BF16)


def _all_rows(j):
    return slice(None)


def _b_rows(j):
    return slice((j // 2) * B_V_DIM, (j // 2 + 1) * B_V_DIM)


def _k_of_a(k_ref):
    return lambda j, off, size: k_ref[0, 0, pl.ds(off, size), :]


def _k_of_b(k_ref):
    return lambda j, off, size: k_ref[0, j // 2, j % 2, pl.ds(off, size), :]


def _attn_a_general(qt_ref, k_ref, vt_ref, o_ref, s_buf, mx_buf, m_ref, l_ref, acc_ref):
    q_list = [_plain_query(qt_ref[0, 0, j]) for j in range(A_STREAMS)]
    _flash_streams(q_list, _k_of_a(k_ref), vt_ref, _all_rows, s_buf, mx_buf, m_ref, l_ref, acc_ref)
    _finish_a(o_ref, l_ref, acc_ref)


def _attn_a_fast(qt_ref, k_ref, vt_ref, kmax_ref, o_ref):
    q_list = [_shifted_query(qt_ref[0, 0, j], kmax_ref[0, 0]) for j in range(A_STREAMS)]
    l, acc = _flash_fast(q_list, _k_of_a(k_ref), vt_ref, _all_rows, *FAST_ORDER_A)
    _finish_a(o_ref, l, acc)


def _attn_b_general(lambda_init, qt_ref, k_ref, vt_ref, lq1_ref, lk1_ref, lq2_ref, lk2_ref,
                    sub_ref, o_ref, s_buf, mx_buf, m_ref, l_ref, acc_ref):
    q_list = [_plain_query(qt_ref[0, 0, hd, c]) for hd in range(B_STEP_HEADS) for c in range(2)]
    _flash_streams(q_list, _k_of_b(k_ref), vt_ref, _b_rows, s_buf, mx_buf, m_ref, l_ref, acc_ref)
    _finish_b(lambda_init, (lq1_ref, lk1_ref, lq2_ref, lk2_ref), sub_ref, o_ref, l_ref, acc_ref)


def _attn_b_fast(lambda_init, qt_ref, k_ref, vt_ref, kmax_ref, lq1_ref, lk1_ref, lq2_ref, lk2_ref,
                 sub_ref, o_ref):
    q_list = [_shifted_query(qt_ref[0, 0, hd, c], kmax_ref[0, hd, c])
              for hd in range(B_STEP_HEADS) for c in range(2)]
    l, acc = _flash_fast(q_list, _k_of_b(k_ref), vt_ref, _b_rows, *FAST_ORDER_B)
    _finish_b(lambda_init, (lq1_ref, lk1_ref, lq2_ref, lk2_ref), sub_ref, o_ref, l, acc)


def _row_parts(ref):
    rows = ref.shape[0] // DENSE_PARTS
    return [slice(i * rows, (i + 1) * rows) for i in range(DENSE_PARTS)]


def _merge_kernel(x_ref, oa_ref, ob_ref, nw_ref, win_ref, wa_ref, wb_ref, wo_ref, x1_ref,
                  wg_s, wa_s, wb_s, wo_s):
    @pl.when(pl.program_id(0) == 0)
    def _():
        wg_s[...] = win_ref[:, OFF_GATES:].astype(BF16)
        wa_s[...] = wa_ref[...].astype(BF16)
        wb_s[...] = wb_ref[...].astype(BF16)
        wo_s[...] = wo_ref[...].astype(BF16)

    d = x_ref.shape[-1]
    parts = _row_parts(x_ref)
    stage1 = []
    for r in parts:
        x = x_ref[r]
        h = (_rms(x, -1) * nw_ref[...]).astype(BF16)
        gates = jnp.dot(h, wg_s[...], preferred_element_type=F32)
        ya = jnp.dot(oa_ref[r], wa_s[...], preferred_element_type=F32)
        yb = jnp.dot(ob_ref[r], wb_s[...], preferred_element_type=F32)
        stage1.append((x, gates, ya, yb))
    for r, (x, gates, ya, yb) in zip(parts, stage1):
        y = jax.nn.sigmoid(gates[:, :d]) * ya + jax.nn.sigmoid(gates[:, d:]) * yb
        x1_ref[r] = x + jnp.dot(y.astype(BF16), wo_s[...], preferred_element_type=F32)


def _cast_kernel(*refs):
    n = len(refs) // 2
    for src, dst in zip(refs[:n], refs[n:]):
        dst[...] = src[...].astype(dst.dtype)


def _ffn_kernel(final, x_ref, nw_ref, wg_ref, wu_ref, wd_ref, nf_ref, o_ref):
    parts = _row_parts(x_ref)
    stage1 = []
    for r in parts:
        x = x_ref[r]
        h = (_rms(x, -1) * nw_ref[...]).astype(BF16)
        gate = jnp.dot(h, wg_ref[...], preferred_element_type=F32)
        up = jnp.dot(h, wu_ref[...], preferred_element_type=F32)
        stage1.append((x, gate, up))
    for r, (x, gate, up) in zip(parts, stage1):
        act = (jax.nn.silu(gate) * up).astype(BF16)
        x2 = x + jnp.dot(act, wd_ref[...], preferred_element_type=F32)
        if final:
            x2 = _rms(x2, -1) * nf_ref[...]
        o_ref[r] = x2


def _const_spec(shape):
    nd = len(shape)
    return pl.BlockSpec(shape, lambda *_: (0,) * nd, pipeline_mode=pl.Buffered(1))


def _rope_angles_t(pos, dim, theta):
    inv_freq = theta ** (-jnp.arange(0, dim, 2, dtype=F32) / dim)
    return (pos[:, None] * inv_freq[None, :]).T


def _angle_tables(seq_len):
    rows = seq_len // GRID_W
    row = jnp.broadcast_to(jnp.arange(rows, dtype=F32)[:, None], (rows, GRID_W)).reshape(-1)
    col = jnp.broadcast_to(jnp.arange(GRID_W, dtype=F32)[None, :], (rows, GRID_W)).reshape(-1)
    ang_a = jnp.concatenate([_rope_angles_t(row, HALF, AXIAL_THETA),
                             _rope_angles_t(col, HALF, AXIAL_THETA)], axis=0)
    ang_b = _rope_angles_t(jnp.arange(seq_len, dtype=F32), HEAD_DIM, ROPE_THETA)
    return jnp.cos(ang_a), jnp.sin(ang_a), jnp.cos(ang_b), jnp.sin(ang_b)


def _params(*sem):
    return pltpu.CompilerParams(dimension_semantics=sem, vmem_limit_bytes=VMEM_LIMIT)


def _layer(x, lambda_init, final, norm_mix, w_in, q_norm_a, k_norm_a, lq1, lk1, lq2, lk2, subln,
           w_proj_a, w_proj_b, w_out, norm_ffn, w_gate, w_up, w_down, norm_final, tables):
    b, s, d = x.shape
    d_ff = w_gate.shape[-1]
    n_tok = b * s
    n_chunks = s // PREP_TM
    ca, sa, cb, sb = tables
    row = lambda v: v.reshape(1, -1).astype(F32)
    col = lambda v: v.reshape(-1, 1).astype(F32)

    tab_spec = pl.BlockSpec((HALF, PREP_TM), lambda bi, ti: (0, ti))
    qta, ka, vta, qtb, kb, vtb, kn, qn = pl.pallas_call(
        _prep_kernel,
        grid=(b, n_chunks),
        in_specs=[
            pl.BlockSpec((1, PREP_TM, d), lambda bi, ti: (bi, ti, 0)),
            _const_spec((1, d)),
            _const_spec((d, OFF_GATES)),
            _const_spec((HEAD_DIM, 1)),
            _const_spec((HEAD_DIM, 1)),
            tab_spec, tab_spec, tab_spec, tab_spec,
        ],
        out_specs=[
            pl.BlockSpec((1, 1, A_Q_HEADS, SLAB, PREP_TM), lambda bi, ti: (bi, ti, 0, 0, 0)),
            pl.BlockSpec((1, A_KV_HEADS, PREP_TM, SLAB), lambda bi, ti: (bi, 0, ti, 0)),
            pl.BlockSpec((1, 1, A_KV_HEADS * HEAD_DIM, PREP_TM), lambda bi, ti: (bi, ti, 0, 0)),
            pl.BlockSpec((1, 1, B_HEADS, 2, SLAB, PREP_TM), lambda bi, ti: (bi, ti, 0, 0, 0, 0)),
            pl.BlockSpec((1, B_HEADS, 2, PREP_TM, SLAB), lambda bi, ti: (bi, 0, 0, ti, 0)),
            pl.BlockSpec((1, 1, B_WIDTH, PREP_TM), lambda bi, ti: (bi, ti, 0, 0)),
            pl.BlockSpec((1, 16, PREP_TM), lambda bi, ti: (bi, 0, ti)),
            pl.BlockSpec((1, 16, PREP_TM), lambda bi, ti: (bi, 0, ti)),
        ],
        out_shape=[
            jax.ShapeDtypeStruct((b, n_chunks, A_Q_HEADS, SLAB, PREP_TM), BF16),
            jax.ShapeDtypeStruct((b, A_KV_HEADS, s, SLAB), BF16),
            jax.ShapeDtypeStruct((b, n_chunks, A_KV_HEADS * HEAD_DIM, PREP_TM), BF16),
            jax.ShapeDtypeStruct((b, n_chunks, B_HEADS, 2, SLAB, PREP_TM), BF16),
            jax.ShapeDtypeStruct((b, B_HEADS, 2, s, SLAB), BF16),
            jax.ShapeDtypeStruct((b, n_chunks, B_WIDTH, PREP_TM), BF16),
            jax.ShapeDtypeStruct((b, 16, s), F32),
            jax.ShapeDtypeStruct((b, 16, s), F32),
        ],
        scratch_shapes=[pltpu.VMEM((d, OFF_GATES), BF16)],
        compiler_params=_params("arbitrary", "arbitrary"),
        name="prep",
    )(x, row(norm_mix), w_in, col(q_norm_a), col(k_norm_a), ca, sa, cb, sb)

    nq = s // ATTN_TQ
    steps_per_group = A_GROUP // A_STREAMS
    kmax_a = jnp.max(kn[:, :A_KV_HEADS], axis=-1)
    kmax_b = jnp.max(kn[:, 8:], axis=-1).reshape(b, B_HEADS, 2)
    qmax_a = jnp.max(qn[:, :A_Q_HEADS], axis=-1)
    qmax_b = jnp.max(qn[:, A_Q_HEADS:], axis=-1).reshape(b, B_HEADS, 2)
    fast_a = jnp.max(qmax_a * jnp.repeat(kmax_a, A_GROUP, axis=1)) < FAST_BOUND
    fast_b = jnp.max(qmax_b * kmax_b) < FAST_BOUND

    def general_scratch(n, dv):
        return [
            pltpu.VMEM((SCORE_SLOTS, n, PREP_TM, ATTN_TQ), F32),
            pltpu.VMEM((SCORE_SLOTS, n, 1, ATTN_TQ), F32),
            pltpu.VMEM((n, 1, ATTN_TQ), F32),
            pltpu.VMEM((n, 1, ATTN_TQ), F32),
            pltpu.VMEM((n, dv, ATTN_TQ), F32),
        ]

    a_specs = [
        pl.BlockSpec((1, 1, A_STREAMS, SLAB, ATTN_TQ), lambda bi, hp, qi: (bi, qi, hp, 0, 0)),
        pl.BlockSpec((1, 1, s, SLAB), lambda bi, hp, qi: (bi, hp // steps_per_group, 0, 0)),
        pl.BlockSpec((1, n_chunks, HEAD_DIM, PREP_TM),
                     lambda bi, hp, qi: (bi, 0, hp // steps_per_group, 0)),
    ]
    a_kmax_spec = pl.BlockSpec((1, 1, 1, ATTN_TQ), lambda bi, hp, qi: (bi, hp // steps_per_group, 0, 0))
    a_common = dict(
        grid=(b, A_Q_HEADS // A_STREAMS, nq),
        out_specs=pl.BlockSpec((1, ATTN_TQ, A_STREAMS * HEAD_DIM), lambda bi, hp, qi: (bi, qi, hp)),
        out_shape=jax.ShapeDtypeStruct((b, s, A_WIDTH), BF16),
        compiler_params=_params("parallel", "parallel", "arbitrary"),
    )

    def attn_a_fast():
        kmax = jnp.broadcast_to(kmax_a[:, :, None, None], (b, A_KV_HEADS, 1, ATTN_TQ))
        return pl.pallas_call(_attn_a_fast, in_specs=a_specs + [a_kmax_spec], name="attn_a_fast",
                              **a_common)(qta, ka, vta, kmax)

    def attn_a_general():
        return pl.pallas_call(_attn_a_general, in_specs=a_specs,
                              scratch_shapes=general_scratch(A_STREAMS, HEAD_DIM), name="attn_a",
                              **a_common)(qta, ka, vta)

    oa = lax.cond(fast_a, attn_a_fast, attn_a_general)

    lam_spec = _const_spec((1, HEAD_DIM))
    hb = B_STEP_HEADS
    b_specs = [
        pl.BlockSpec((1, 1, hb, 2, SLAB, ATTN_TQ), lambda bi, hp, qi: (bi, qi, hp, 0, 0, 0)),
        pl.BlockSpec((1, hb, 2, s, SLAB), lambda bi, hp, qi: (bi, hp, 0, 0, 0)),
        pl.BlockSpec((1, n_chunks, hb * B_V_DIM, PREP_TM), lambda bi, hp, qi: (bi, 0, hp, 0)),
    ]
    b_kmax_spec = pl.BlockSpec((1, hb, 2, 1, ATTN_TQ), lambda bi, hp, qi: (bi, hp, 0, 0, 0))
    b_tail_specs = [lam_spec, lam_spec, lam_spec, lam_spec, _const_spec((B_V_DIM, 1))]
    b_tail = (row(lq1), row(lk1), row(lq2), row(lk2), col(subln))
    b_common = dict(
        grid=(b, B_HEADS // hb, nq),
        out_specs=pl.BlockSpec((1, ATTN_TQ, hb * B_V_DIM), lambda bi, hp, qi: (bi, qi, hp)),
        out_shape=jax.ShapeDtypeStruct((b, s, B_WIDTH), BF16),
        compiler_params=_params("parallel", "parallel", "arbitrary"),
    )

    def attn_b_fast():
        kmax = jnp.broadcast_to(kmax_b[:, :, :, None, None], (b, B_HEADS, 2, 1, ATTN_TQ))
        return pl.pallas_call(functools.partial(_attn_b_fast, lambda_init),
                              in_specs=b_specs + [b_kmax_spec] + b_tail_specs, name="attn_b_fast",
                              **b_common)(qtb, kb, vtb, kmax, *b_tail)

    def attn_b_general():
        return pl.pallas_call(functools.partial(_attn_b_general, lambda_init),
                              in_specs=b_specs + b_tail_specs,
                              scratch_shapes=general_scratch(2 * hb, B_V_DIM), name="attn_b",
                              **b_common)(qtb, kb, vtb, *b_tail)

    ob = lax.cond(fast_b, attn_b_fast, attn_b_general)

    tok_spec = lambda width: pl.BlockSpec((DENSE_TM, width), lambda ti: (ti, 0))
    x_flat = x.reshape(n_tok, d)
    x1 = pl.pallas_call(
        _merge_kernel,
        grid=(n_tok // DENSE_TM,),
        in_specs=[
            tok_spec(d), tok_spec(A_WIDTH), tok_spec(B_WIDTH),
            _const_spec((1, d)),
            _const_spec(w_in.shape),
            _const_spec((A_WIDTH, d)),
            _const_spec((B_WIDTH, d)),
            _const_spec((d, d)),
        ],
        out_specs=tok_spec(d),
        out_shape=jax.ShapeDtypeStruct((n_tok, d), F32),
        scratch_shapes=[pltpu.VMEM((d, 2 * d), BF16), pltpu.VMEM((A_WIDTH, d), BF16),
                        pltpu.VMEM((B_WIDTH, d), BF16), pltpu.VMEM((d, d), BF16)],
        compiler_params=_params("arbitrary"),
        name="merge",
    )(x_flat, oa.reshape(n_tok, A_WIDTH), ob.reshape(n_tok, B_WIDTH), row(norm_mix),
      w_in, w_proj_a, w_proj_b, w_out)

    ffn_shapes = [w_gate.shape, w_up.shape, w_down.shape]
    ffn_specs = [pl.BlockSpec((sh[0] // CAST_STEPS, sh[1]), lambda i: (i, 0)) for sh in ffn_shapes]
    ffn_w = pl.pallas_call(
        _cast_kernel,
        grid=(CAST_STEPS,),
        in_specs=ffn_specs,
        out_specs=ffn_specs,
        out_shape=[jax.ShapeDtypeStruct(sh, BF16) for sh in ffn_shapes],
        compiler_params=_params("parallel"),
        name="ffn_weights",
    )(w_gate, w_up, w_down)

    x2 = pl.pallas_call(
        functools.partial(_ffn_kernel, final),
        grid=(n_tok // DENSE_TM,),
        in_specs=[
            tok_spec(d),
            _const_spec((1, d)),
            _const_spec((d, d_ff)),
            _const_spec((d, d_ff)),
            _const_spec((d_ff, d)),
            _const_spec((1, d)),
        ],
        out_specs=tok_spec(d),
        out_shape=jax.ShapeDtypeStruct((n_tok, d), F32),
        compiler_params=_params("parallel"),
        name="ffn",
    )(x1, row(norm_ffn), *ffn_w, row(norm_final))
    return x2.reshape(b, s, d)


def kernel(x, norm_mix, w_in, q_norm_a, k_norm_a, lambda_q1, lambda_k1, lambda_q2, lambda_k2,
           subln_b, w_proj_a, w_proj_b, w_out, norm_ffn, w_gate_ffn, w_up_ffn, w_down_ffn,
           norm_final):
    depth = norm_mix.shape[0]
    tables = _angle_tables(x.shape[1])
    for l in range(depth):
        lambda_init = 0.8 - 0.6 * math.exp(-0.3 * l)
        x = _layer(x, lambda_init, l == depth - 1, norm_mix[l], w_in[l], q_norm_a[l], k_norm_a[l],
                   lambda_q1[l], lambda_k1[l], lambda_q2[l], lambda_k2[l], subln_b[l],
                   w_proj_a[l], w_proj_b[l], w_out[l], norm_ffn[l], w_gate_ffn[l], w_up_ffn[l],
                   w_down_ffn[l], norm_final, tables)
    return x
```

```python
import functools
import math

import jax
import jax.numpy as jnp
from jax import lax
from jax.experimental import pallas as pl
from jax.experimental.pallas import tpu as pltpu

F32 = jnp.float32
BF16 = jnp.bfloat16

GRID_W = 64
HEAD_DIM = 64
HALF = HEAD_DIM // 2
A_Q_HEADS = 8
A_KV_HEADS = 2
A_GROUP = A_Q_HEADS // A_KV_HEADS
A_WIDTH = A_Q_HEADS * HEAD_DIM
B_HEADS = 4
B_V_DIM = 2 * HEAD_DIM
B_WIDTH = B_HEADS * B_V_DIM
ROPE_THETA = 10000.0
AXIAL_THETA = 10000.0
NORM_EPS = 1e-6
QK_SCALE = math.log2(math.e) / math.sqrt(HEAD_DIM)
A_STREAMS = 4
B_STEP_HEADS = 2
SCORE_LAG = 2
SCORE_SLOTS = 3

OFF_AQ = 0
OFF_AK = OFF_AQ + A_WIDTH
OFF_AV = OFF_AK + A_KV_HEADS * HEAD_DIM
OFF_BQ = OFF_AV + A_KV_HEADS * HEAD_DIM
OFF_BK = OFF_BQ + B_HEADS * 2 * HEAD_DIM
OFF_BV = OFF_BK + B_HEADS * 2 * HEAD_DIM
OFF_GATES = OFF_BV + B_WIDTH
SLAB = 2 * HEAD_DIM
SUBLANES = 8
NORM_ROWS = 2 * SUBLANES
FAST_BOUND = 40.0

PREP_TM = 512
PREP_PARTS = 2
ATTN_TQ = PREP_TM
FAST_KC = 8192
FAST_ORDER_A = (256, 4)
FAST_ORDER_B = (4096, 2)
DENSE_TM = 512
DENSE_PARTS = 2
CAST_STEPS = 8
VMEM_LIMIT = 56 * 1024 * 1024


def _rms(x, axis):
    return x * lax.rsqrt(jnp.mean(x * x, axis=axis, keepdims=True) + NORM_EPS)


def _rope_t(xt, cos, sin):
    x1, x2 = xt[:HALF], xt[HALF:]
    return jnp.concatenate([x1 * cos - x2 * sin, x2 * cos + x1 * sin], axis=0)


def _prep_kernel(x_ref, nw_ref, w_ref, gq_ref, gk_ref, ca_ref, sa_ref, cb_ref, sb_ref,
                 qta_ref, ka_ref, vta_ref, qtb_ref, kb_ref, vtb_ref, kn_ref, qn_ref, wb_ref):
    @pl.when((pl.program_id(0) == 0) & (pl.program_id(1) == 0))
    def _():
        wb_ref[...] = w_ref[...].astype(BF16)

    tm = x_ref.shape[1] // PREP_PARTS
    gq, gk = gq_ref[...], gk_ref[...]
    pad = jnp.zeros((SLAB - HEAD_DIM - SUBLANES, tm), F32)
    one_row = (lax.broadcasted_iota(jnp.int32, (SUBLANES, tm), 0) == 0).astype(F32)

    def norm(v):
        return jnp.sqrt(jnp.sum(v * v, axis=0, keepdims=True))

    def k_slab(k):
        return jnp.concatenate([k, one_row, pad], axis=0).T.astype(BF16)

    groups = [slice(i * tm, (i + 1) * tm) for i in range(PREP_PARTS)]
    zs = []
    for t in groups:
        h = _rms(x_ref[0, t], -1) * nw_ref[...]
        zs.append(jnp.dot(h.astype(BF16), wb_ref[...], preferred_element_type=F32))

    for t, z in zip(groups, zs):
        ca, sa, cb, sb = ca_ref[:, t], sa_ref[:, t], cb_ref[:, t], sb_ref[:, t]
        q_norms, k_norms = [], []

        def q_block(q):
            nq = norm(q).astype(BF16).astype(F32)
            q_norms.append(nq)
            return jnp.concatenate([q, jnp.broadcast_to(nq, (SUBLANES, tm)), pad], axis=0).astype(BF16)

        aqt = z[:, OFF_AQ:OFF_AK].T
        for hd in range(A_Q_HEADS):
            q = aqt[hd * HEAD_DIM:(hd + 1) * HEAD_DIM]
            qta_ref[0, 0, hd, :, t] = q_block(_rope_t(_rms(q, 0) * gq, ca, sa) * QK_SCALE)
        akt = z[:, OFF_AK:OFF_AV].T
        for g in range(A_KV_HEADS):
            k = _rope_t(_rms(akt[g * HEAD_DIM:(g + 1) * HEAD_DIM], 0) * gk, ca, sa)
            ka_ref[0, g, t, :] = k_slab(k)
            k_norms.append(norm(k))
        k_norms.append(jnp.zeros((SUBLANES - A_KV_HEADS, tm), F32))
        vta_ref[0, 0, :, t] = z[:, OFF_AV:OFF_BQ].T.astype(BF16)

        bqt = z[:, OFF_BQ:OFF_BK].T
        bkt = z[:, OFF_BK:OFF_BV].T
        for hd in range(B_HEADS):
            for c in range(2):
                r0 = (hd * 2 + c) * HEAD_DIM
                qtb_ref[0, 0, hd, c, :, t] = q_block(_rope_t(bqt[r0:r0 + HEAD_DIM], cb, sb) * QK_SCALE)
                k = _rope_t(bkt[r0:r0 + HEAD_DIM], cb, sb)
                kb_ref[0, hd, c, t, :] = k_slab(k)
                k_norms.append(norm(k))
        vtb_ref[0, 0, :, t] = z[:, OFF_BV:OFF_GATES].T.astype(BF16)
        kn_ref[0, :, t] = jnp.concatenate(k_norms, axis=0)
        qn_ref[0, :, t] = jnp.concatenate(q_norms, axis=0)


def _plain_query(qt):
    qf = qt.astype(F32)
    rows = lax.broadcasted_iota(jnp.int32, qf.shape, 0)
    return jnp.where(rows < HEAD_DIM, qf, 0.0).astype(BF16)


def _shifted_query(qt, kmax):
    qf = qt.astype(F32)
    rows = lax.broadcasted_iota(jnp.int32, qf.shape, 0)
    shifted = jnp.where(rows == HEAD_DIM, -(qf * kmax), jnp.where(rows < HEAD_DIM, qf, 0.0))
    return shifted.astype(BF16)


def _flash_fast(q_list, k_of, vt_ref, v_rows, l_ref, acc_ref, sb, skew):
    n = len(q_list)
    vb = vt_ref.shape[3]
    pb = min(sb, vb)
    l_ref[...] = jnp.zeros(l_ref.shape, F32)
    acc_ref[...] = jnp.zeros(acc_ref.shape, F32)

    def body(i, carry):
        off = pl.multiple_of(i * FAST_KC, FAST_KC)
        l = [l_ref[j] for j in range(n)]
        acc = [acc_ref[j] for j in range(n)]
        items = [(g, j) for g in range(FAST_KC // sb) for j in range(n)]
        scores = {}

        def consume(g, j):
            p = jnp.exp2(scores.pop((g, j)))
            l[j] = l[j] + jnp.sum(p, axis=0, keepdims=True)
            p = p.astype(BF16)
            for u in range(sb // pb):
                k0 = g * sb + u * pb
                vblk = vt_ref[0, i * (FAST_KC // vb) + k0 // vb, v_rows(j), k0 % vb:k0 % vb + pb]
                acc[j] = acc[j] + jnp.dot(vblk, p[u * pb:(u + 1) * pb], preferred_element_type=F32)

        for t, (g, j) in enumerate(items):
            scores[(g, j)] = jnp.dot(k_of(j, off + g * sb, sb), q_list[j],
                                     preferred_element_type=F32)
            if t >= skew:
                consume(*items[t - skew])
        for g, j in items[max(len(items) - skew, 0):]:
            consume(g, j)
        for j in range(n):
            l_ref[j], acc_ref[j] = l[j], acc[j]
        return carry

    lax.fori_loop(0, vt_ref.shape[1] * vb // FAST_KC, body, 0)


def _flash_streams(q_list, k_of, vt_ref, v_rows, s_buf, mx_buf, m_ref, l_ref, acc_ref):
    n = len(q_list)
    n_chunks, kc = vt_ref.shape[1], vt_ref.shape[3]
    m_ref[...] = jnp.full(m_ref.shape, -jnp.inf, F32)
    l_ref[...] = jnp.zeros(l_ref.shape, F32)
    acc_ref[...] = jnp.zeros(acc_ref.shape, F32)

    def scores(c, slot):
        off = c * kc if isinstance(c, int) else pl.multiple_of(c * kc, kc)
        for j in range(n):
            s = jnp.dot(k_of(j, off, kc), q_list[j], preferred_element_type=F32)
            s_buf[slot, j] = s
            mx_buf[slot, j] = jnp.max(s, axis=0, keepdims=True)

    def softmax_pv(c, slot):
        for j in range(n):
            vblk = vt_ref[0, c, v_rows(j)]
            m_old = m_ref[j]
            m_new = jnp.maximum(m_old, mx_buf[slot, j])
            alpha = jnp.exp2(m_old - m_new)
            p = jnp.exp2(s_buf[slot, j] - m_new)
            l_ref[j] = alpha * l_ref[j] + jnp.sum(p, axis=0, keepdims=True)
            acc_ref[j] = alpha * acc_ref[j] + jnp.dot(vblk, p.astype(BF16),
                                                      preferred_element_type=F32)
            m_ref[j] = m_new

    n_slots = s_buf.shape[0]

    def stage(c, u):
        if not isinstance(c, int) or c + SCORE_LAG < n_chunks:
            scores(c + SCORE_LAG, (u + SCORE_LAG) % n_slots)
        softmax_pv(c, u)

    for c in range(SCORE_LAG):
        scores(c, c % n_slots)
    n_loop = (n_chunks - SCORE_LAG) // n_slots

    def body(i, carry):
        for u in range(n_slots):
            stage(i * n_slots + u, u)
        return carry

    lax.fori_loop(0, n_loop, body, 0)
    for c in range(n_loop * n_slots, n_chunks):
        stage(c, c % n_slots)


def _finish_a(o_ref, l_ref, acc_ref):
    outs = [acc_ref[j] * (1.0 / l_ref[j]) for j in range(A_STREAMS)]
    o_ref[0] = jnp.concatenate(outs, axis=0).T.astype(BF16)


def _finish_b(lambda_init, lam_refs, sub_ref, o_ref, l_ref, acc_ref):
    lq1_ref, lk1_ref, lq2_ref, lk2_ref = lam_refs
    lam = (jnp.exp(jnp.sum(lq1_ref[...] * lk1_ref[...], axis=-1, keepdims=True))
           - jnp.exp(jnp.sum(lq2_ref[...] * lk2_ref[...], axis=-1, keepdims=True))
           + lambda_init)
    outs = []
    for hd in range(B_STEP_HEADS):
        j1, j2 = 2 * hd, 2 * hd + 1
        o = acc_ref[j1] * (1.0 / l_ref[j1]) - lam * (acc_ref[j2] * (1.0 / l_ref[j2]))
        outs.append(_rms(o, 0) * sub_ref[...] * (1.0 - lambda_init))
    o_ref[0] = jnp.concatenate(outs, axis=0).T.astype(BF16)


def _all_rows(j):
    return slice(None)


def _b_rows(j):
    return slice((j // 2) * B_V_DIM, (j // 2 + 1) * B_V_DIM)


def _k_of_a(k_ref):
    return lambda j, off, size: k_ref[0, 0, pl.ds(off, size), :]


def _k_of_b(k_ref):
    return lambda j, off, size: k_ref[0, j // 2, j % 2, pl.ds(off, size), :]


def _attn_a_general(qt_ref, k_ref, vt_ref, o_ref, s_buf, mx_buf, m_ref, l_ref, acc_ref):
    q_list = [_plain_query(qt_ref[0, 0, j]) for j in range(A_STREAMS)]
    _flash_streams(q_list, _k_of_a(k_ref), vt_ref, _all_rows, s_buf, mx_buf, m_ref, l_ref, acc_ref)
    _finish_a(o_ref, l_ref, acc_ref)


def _attn_a_fast(qt_ref, k_ref, vt_ref, kmax_ref, o_ref, l_ref, acc_ref):
    q_list = [_shifted_query(qt_ref[0, 0, j], kmax_ref[0, 0]) for j in range(A_STREAMS)]
    _flash_fast(q_list, _k_of_a(k_ref), vt_ref, _all_rows, l_ref, acc_ref, *FAST_ORDER_A)
    _finish_a(o_ref, l_ref, acc_ref)


def _attn_b_general(lambda_init, qt_ref, k_ref, vt_ref, lq1_ref, lk1_ref, lq2_ref, lk2_ref,
                    sub_ref, o_ref, s_buf, mx_buf, m_ref, l_ref, acc_ref):
    q_list = [_plain_query(qt_ref[0, 0, hd, c]) for hd in range(B_STEP_HEADS) for c in range(2)]
    _flash_streams(q_list, _k_of_b(k_ref), vt_ref, _b_rows, s_buf, mx_buf, m_ref, l_ref, acc_ref)
    _finish_b(lambda_init, (lq1_ref, lk1_ref, lq2_ref, lk2_ref), sub_ref, o_ref, l_ref, acc_ref)


def _attn_b_fast(lambda_init, qt_ref, k_ref, vt_ref, kmax_ref, lq1_ref, lk1_ref, lq2_ref, lk2_ref,
                 sub_ref, o_ref, l_ref, acc_ref):
    q_list = [_shifted_query(qt_ref[0, 0, hd, c], kmax_ref[0, hd, c])
              for hd in range(B_STEP_HEADS) for c in range(2)]
    _flash_fast(q_list, _k_of_b(k_ref), vt_ref, _b_rows, l_ref, acc_ref, *FAST_ORDER_B)
    _finish_b(lambda_init, (lq1_ref, lk1_ref, lq2_ref, lk2_ref), sub_ref, o_ref, l_ref, acc_ref)


def _row_parts(ref):
    rows = ref.shape[0] // DENSE_PARTS
    return [slice(i * rows, (i + 1) * rows) for i in range(DENSE_PARTS)]


def _merge_kernel(x_ref, oa_ref, ob_ref, nw_ref, win_ref, wa_ref, wb_ref, wo_ref, x1_ref,
                  wg_s, wa_s, wb_s, wo_s):
    @pl.when(pl.program_id(0) == 0)
    def _():
        wg_s[...] = win_ref[:, OFF_GATES:].astype(BF16)
        wa_s[...] = wa_ref[...].astype(BF16)
        wb_s[...] = wb_ref[...].astype(BF16)
        wo_s[...] = wo_ref[...].astype(BF16)

    d = x_ref.shape[-1]
    parts = _row_parts(x_ref)
    stage1 = []
    for r in parts:
        x = x_ref[r]
        h = (_rms(x, -1) * nw_ref[...]).astype(BF16)
        gates = jnp.dot(h, wg_s[...], preferred_element_type=F32)
        ya = jnp.dot(oa_ref[r], wa_s[...], preferred_element_type=F32)
        yb = jnp.dot(ob_ref[r], wb_s[...], preferred_element_type=F32)
        stage1.append((x, gates, ya, yb))
    for r, (x, gates, ya, yb) in zip(parts, stage1):
        y = jax.nn.sigmoid(gates[:, :d]) * ya + jax.nn.sigmoid(gates[:, d:]) * yb
        x1_ref[r] = x + jnp.dot(y.astype(BF16), wo_s[...], preferred_element_type=F32)


def _cast_kernel(*refs):
    n = len(refs) // 2
    for src, dst in zip(refs[:n], refs[n:]):
        dst[...] = src[...].astype(dst.dtype)


def _ffn_kernel(final, x_ref, nw_ref, wg_ref, wu_ref, wd_ref, nf_ref, o_ref):
    parts = _row_parts(x_ref)
    stage1 = []
    for r in parts:
        x = x_ref[r]
        h = (_rms(x, -1) * nw_ref[...]).astype(BF16)
        gate = jnp.dot(h, wg_ref[...], preferred_element_type=F32)
        up = jnp.dot(h, wu_ref[...], preferred_element_type=F32)
        stage1.append((x, gate, up))
    for r, (x, gate, up) in zip(parts, stage1):
        act = (jax.nn.silu(gate) * up).astype(BF16)
        x2 = x + jnp.dot(act, wd_ref[...], preferred_element_type=F32)
        if final:
            x2 = _rms(x2, -1) * nf_ref[...]
        o_ref[r] = x2


def _const_spec(shape):
    nd = len(shape)
    return pl.BlockSpec(shape, lambda *_: (0,) * nd, pipeline_mode=pl.Buffered(1))


def _rope_angles_t(pos, dim, theta):
    inv_freq = theta ** (-jnp.arange(0, dim, 2, dtype=F32) / dim)
    return (pos[:, None] * inv_freq[None, :]).T


def _angle_tables(seq_len):
    rows = seq_len // GRID_W
    row = jnp.broadcast_to(jnp.arange(rows, dtype=F32)[:, None], (rows, GRID_W)).reshape(-1)
    col = jnp.broadcast_to(jnp.arange(GRID_W, dtype=F32)[None, :], (rows, GRID_W)).reshape(-1)
    ang_a = jnp.concatenate([_rope_angles_t(row, HALF, AXIAL_THETA),
                             _rope_angles_t(col, HALF, AXIAL_THETA)], axis=0)
    ang_b = _rope_angles_t(jnp.arange(seq_len, dtype=F32), HEAD_DIM, ROPE_THETA)
    return jnp.cos(ang_a), jnp.sin(ang_a), jnp.cos(ang_b), jnp.sin(ang_b)


def _params(*sem):
    return pltpu.CompilerParams(dimension_semantics=sem, vmem_limit_bytes=VMEM_LIMIT)


def _layer(x, lambda_init, final, norm_mix, w_in, q_norm_a, k_norm_a, lq1, lk1, lq2, lk2, subln,
           w_proj_a, w_proj_b, w_out, norm_ffn, w_gate, w_up, w_down, norm_final, tables):
    b, s, d = x.shape
    d_ff = w_gate.shape[-1]
    n_tok = b * s
    n_chunks = s // PREP_TM
    ca, sa, cb, sb = tables
    row = lambda v: v.reshape(1, -1).astype(F32)
    col = lambda v: v.reshape(-1, 1).astype(F32)

    tab_spec = pl.BlockSpec((HALF, PREP_TM), lambda bi, ti: (0, ti))
    qta, ka, vta, qtb, kb, vtb, kn, qn = pl.pallas_call(
        _prep_kernel,
        grid=(b, n_chunks),
        in_specs=[
            pl.BlockSpec((1, PREP_TM, d), lambda bi, ti: (bi, ti, 0)),
            _const_spec((1, d)),
            _const_spec((d, OFF_GATES)),
            _const_spec((HEAD_DIM, 1)),
            _const_spec((HEAD_DIM, 1)),
            tab_spec, tab_spec, tab_spec, tab_spec,
        ],
        out_specs=[
            pl.BlockSpec((1, 1, A_Q_HEADS, SLAB, PREP_TM), lambda bi, ti: (bi, ti, 0, 0, 0)),
            pl.BlockSpec((1, A_KV_HEADS, PREP_TM, SLAB), lambda bi, ti: (bi, 0, ti, 0)),
            pl.BlockSpec((1, 1, A_KV_HEADS * HEAD_DIM, PREP_TM), lambda bi, ti: (bi, ti, 0, 0)),
            pl.BlockSpec((1, 1, B_HEADS, 2, SLAB, PREP_TM), lambda bi, ti: (bi, ti, 0, 0, 0, 0)),
            pl.BlockSpec((1, B_HEADS, 2, PREP_TM, SLAB), lambda bi, ti: (bi, 0, 0, ti, 0)),
            pl.BlockSpec((1, 1, B_WIDTH, PREP_TM), lambda bi, ti: (bi, ti, 0, 0)),
            pl.BlockSpec((1, NORM_ROWS, PREP_TM), lambda bi, ti: (bi, 0, ti)),
            pl.BlockSpec((1, NORM_ROWS, PREP_TM), lambda bi, ti: (bi, 0, ti)),
        ],
        out_shape=[
            jax.ShapeDtypeStruct((b, n_chunks, A_Q_HEADS, SLAB, PREP_TM), BF16),
            jax.ShapeDtypeStruct((b, A_KV_HEADS, s, SLAB), BF16),
            jax.ShapeDtypeStruct((b, n_chunks, A_KV_HEADS * HEAD_DIM, PREP_TM), BF16),
            jax.ShapeDtypeStruct((b, n_chunks, B_HEADS, 2, SLAB, PREP_TM), BF16),
            jax.ShapeDtypeStruct((b, B_HEADS, 2, s, SLAB), BF16),
            jax.ShapeDtypeStruct((b, n_chunks, B_WIDTH, PREP_TM), BF16),
            jax.ShapeDtypeStruct((b, NORM_ROWS, s), F32),
            jax.ShapeDtypeStruct((b, NORM_ROWS, s), F32),
        ],
        scratch_shapes=[pltpu.VMEM((d, OFF_GATES), BF16)],
        compiler_params=_params("arbitrary", "arbitrary"),
        name="prep",
    )(x, row(norm_mix), w_in, col(q_norm_a), col(k_norm_a), ca, sa, cb, sb)

    nq = s // ATTN_TQ
    steps_per_group = A_GROUP // A_STREAMS
    kmax_a = jnp.max(kn[:, :A_KV_HEADS], axis=-1)
    kmax_b = jnp.max(kn[:, SUBLANES:], axis=-1).reshape(b, B_HEADS, 2)
    qmax_a = jnp.max(qn[:, :A_Q_HEADS], axis=-1)
    qmax_b = jnp.max(qn[:, A_Q_HEADS:], axis=-1).reshape(b, B_HEADS, 2)
    fast_a = jnp.max(qmax_a * jnp.repeat(kmax_a, A_GROUP, axis=1)) < FAST_BOUND
    fast_b = jnp.max(qmax_b * kmax_b) < FAST_BOUND

    def general_scratch(n, dv):
        return [
            pltpu.VMEM((SCORE_SLOTS, n, PREP_TM, ATTN_TQ), F32),
            pltpu.VMEM((SCORE_SLOTS, n, 1, ATTN_TQ), F32),
            pltpu.VMEM((n, 1, ATTN_TQ), F32),
            pltpu.VMEM((n, 1, ATTN_TQ), F32),
            pltpu.VMEM((n, dv, ATTN_TQ), F32),
        ]

    def fast_scratch(n, dv):
        return [pltpu.VMEM((n, 1, ATTN_TQ), F32), pltpu.VMEM((n, dv, ATTN_TQ), F32)]

    a_specs = [
        pl.BlockSpec((1, 1, A_STREAMS, SLAB, ATTN_TQ), lambda bi, hp, qi: (bi, qi, hp, 0, 0)),
        pl.BlockSpec((1, 1, s, SLAB), lambda bi, hp, qi: (bi, hp // steps_per_group, 0, 0)),
        pl.BlockSpec((1, n_chunks, HEAD_DIM, PREP_TM),
                     lambda bi, hp, qi: (bi, 0, hp // steps_per_group, 0)),
    ]
    a_kmax_spec = pl.BlockSpec((1, 1, 1, ATTN_TQ), lambda bi, hp, qi: (bi, hp // steps_per_group, 0, 0))
    a_common = dict(
        grid=(b, A_Q_HEADS // A_STREAMS, nq),
        out_specs=pl.BlockSpec((1, ATTN_TQ, A_STREAMS * HEAD_DIM), lambda bi, hp, qi: (bi, qi, hp)),
        out_shape=jax.ShapeDtypeStruct((b, s, A_WIDTH), BF16),
        compiler_params=_params("parallel", "parallel", "arbitrary"),
    )

    kmax_a_rows = jnp.broadcast_to(kmax_a[:, :, None, None], (b, A_KV_HEADS, 1, ATTN_TQ))
    kmax_b_rows = jnp.broadcast_to(kmax_b[:, :, :, None, None], (b, B_HEADS, 2, 1, ATTN_TQ))

    def attn_a_fast():
        return pl.pallas_call(_attn_a_fast, in_specs=a_specs + [a_kmax_spec],
                              scratch_shapes=fast_scratch(A_STREAMS, HEAD_DIM), name="attn_a_fast",
                              **a_common)(qta, ka, vta, kmax_a_rows)

    def attn_a_general():
        return pl.pallas_call(_attn_a_general, in_specs=a_specs,
                              scratch_shapes=general_scratch(A_STREAMS, HEAD_DIM), name="attn_a",
                              **a_common)(qta, ka, vta)

    oa = lax.cond(fast_a, attn_a_fast, attn_a_general)

    lam_spec = _const_spec((1, HEAD_DIM))
    hb = B_STEP_HEADS
    b_specs = [
        pl.BlockSpec((1, 1, hb, 2, SLAB, ATTN_TQ), lambda bi, hp, qi: (bi, qi, hp, 0, 0, 0)),
        pl.BlockSpec((1, hb, 2, s, SLAB), lambda bi, hp, qi: (bi, hp, 0, 0, 0)),
        pl.BlockSpec((1, n_chunks, hb * B_V_DIM, PREP_TM), lambda bi, hp, qi: (bi, 0, hp, 0)),
    ]
    b_kmax_spec = pl.BlockSpec((1, hb, 2, 1, ATTN_TQ), lambda bi, hp, qi: (bi, hp, 0, 0, 0))
    b_tail_specs = [lam_spec, lam_spec, lam_spec, lam_spec, _const_spec((B_V_DIM, 1))]
    b_tail = (row(lq1), row(lk1), row(lq2), row(lk2), col(subln))
    b_common = dict(
        grid=(b, B_HEADS // hb, nq),
        out_specs=pl.BlockSpec((1, ATTN_TQ, hb * B_V_DIM), lambda bi, hp, qi: (bi, qi, hp)),
        out_shape=jax.ShapeDtypeStruct((b, s, B_WIDTH), BF16),
        compiler_params=_params("parallel", "parallel", "arbitrary"),
    )

    def attn_b_fast():
        return pl.pallas_call(functools.partial(_attn_b_fast, lambda_init),
                              in_specs=b_specs + [b_kmax_spec] + b_tail_specs,
                              scratch_shapes=fast_scratch(2 * hb, B_V_DIM), name="attn_b_fast",
                              **b_common)(qtb, kb, vtb, kmax_b_rows, *b_tail)

    def attn_b_general():
        return pl.pallas_call(functools.partial(_attn_b_general, lambda_init),
                              in_specs=b_specs + b_tail_specs,
                              scratch_shapes=general_scratch(2 * hb, B_V_DIM), name="attn_b",
                              **b_common)(qtb, kb, vtb, *b_tail)

    ob = lax.cond(fast_b, attn_b_fast, attn_b_general)

    tok_spec = lambda width: pl.BlockSpec((DENSE_TM, width), lambda ti: (ti, 0))
    x_flat = x.reshape(n_tok, d)
    x1 = pl.pallas_call(
        _merge_kernel,
        grid=(n_tok // DENSE_TM,),
        in_specs=[
            tok_spec(d), tok_spec(A_WIDTH), tok_spec(B_WIDTH),
            _const_spec((1, d)),
            _const_spec(w_in.shape),
            _const_spec((A_WIDTH, d)),
            _const_spec((B_WIDTH, d)),
            _const_spec((d, d)),
        ],
        out_specs=tok_spec(d),
        out_shape=jax.ShapeDtypeStruct((n_tok, d), F32),
        scratch_shapes=[pltpu.VMEM((d, 2 * d), BF16), pltpu.VMEM((A_WIDTH, d), BF16),
                        pltpu.VMEM((B_WIDTH, d), BF16), pltpu.VMEM((d, d), BF16)],
        compiler_params=_params("arbitrary"),
        name="merge",
    )(x_flat, oa.reshape(n_tok, A_WIDTH), ob.reshape(n_tok, B_WIDTH), row(norm_mix),
      w_in, w_proj_a, w_proj_b, w_out)

    ffn_shapes = [w_gate.shape, w_up.shape, w_down.shape]
    ffn_specs = [pl.BlockSpec((sh[0] // CAST_STEPS, sh[1]), lambda i: (i, 0)) for sh in ffn_shapes]
    ffn_w = pl.pallas_call(
        _cast_kernel,
        grid=(CAST_STEPS,),
        in_specs=ffn_specs,
        out_specs=ffn_specs,
        out_shape=[jax.ShapeDtypeStruct(sh, BF16) for sh in ffn_shapes],
        compiler_params=_params("parallel"),
        name="ffn_weights",
    )(w_gate, w_up, w_down)

    x2 = pl.pallas_call(
        functools.partial(_ffn_kernel, final),
        grid=(n_tok // DENSE_TM,),
        in_specs=[
            tok_spec(d),
            _const_spec((1, d)),
            _const_spec((d, d_ff)),
            _const_spec((d, d_ff)),
            _const_spec((d_ff, d)),
            _const_spec((1, d)),
        ],
        out_specs=tok_spec(d),
        out_shape=jax.ShapeDtypeStruct((n_tok, d), F32),
        compiler_params=_params("parallel"),
        name="ffn",
    )(x1, row(norm_ffn), *ffn_w, row(norm_final))
    return x2.reshape(b, s, d)


def kernel(x, norm_mix, w_in, q_norm_a, k_norm_a, lambda_q1, lambda_k1, lambda_q2, lambda_k2,
           subln_b, w_proj_a, w_proj_b, w_out, norm_ffn, w_gate_ffn, w_up_ffn, w_down_ffn,
           norm_final):
    depth = norm_mix.shape[0]
    tables = _angle_tables(x.shape[1])
    for l in range(depth):
        lambda_init = 0.8 - 0.6 * math.exp(-0.3 * l)
        x = _layer(x, lambda_init, l == depth - 1, norm_mix[l], w_in[l], q_norm_a[l], k_norm_a[l],
                   lambda_q1[l], lambda_k1[l], lambda_q2[l], lambda_k2[l], subln_b[l],
                   w_proj_a[l], w_proj_b[l], w_out[l], norm_ffn[l], w_gate_ffn[l], w_up_ffn[l],
                   w_down_ffn[l], norm_final, tables)
    return x
```

```python
import functools
import math

import jax
import jax.numpy as jnp
from jax import lax
from jax.experimental import pallas as pl
from jax.experimental.pallas import tpu as pltpu

F32 = jnp.float32
BF16 = jnp.bfloat16

GRID_W = 64
HEAD_DIM = 64
HALF = HEAD_DIM // 2
A_Q_HEADS = 8
A_KV_HEADS = 2
A_GROUP = A_Q_HEADS // A_KV_HEADS
A_WIDTH = A_Q_HEADS * HEAD_DIM
B_HEADS = 4
B_V_DIM = 2 * HEAD_DIM
B_WIDTH = B_HEADS * B_V_DIM
ROPE_THETA = 10000.0
AXIAL_THETA = 10000.0
NORM_EPS = 1e-6
QK_SCALE = math.log2(math.e) / math.sqrt(HEAD_DIM)
A_STREAMS = 4
B_STEP_HEADS = 2
SCORE_LAG = 2
SCORE_SLOTS = 3

OFF_AQ = 0
OFF_AK = OFF_AQ + A_WIDTH
OFF_AV = OFF_AK + A_KV_HEADS * HEAD_DIM
OFF_BQ = OFF_AV + A_KV_HEADS * HEAD_DIM
OFF_BK = OFF_BQ + B_HEADS * 2 * HEAD_DIM
OFF_BV = OFF_BK + B_HEADS * 2 * HEAD_DIM
OFF_GATES = OFF_BV + B_WIDTH
SLAB = 2 * HEAD_DIM
SUBLANES = 8
NORM_ROWS = 2 * SUBLANES
FAST_BOUND = 40.0

PREP_TM = 512
PREP_PARTS = 2
ATTN_TQ = PREP_TM
FAST_KC = 8192
FAST_ORDER_A = (256, 4)
FAST_ORDER_B = (4096, 2)
DENSE_TM = 512
DENSE_PARTS = 2
CAST_STEPS = 8
VMEM_LIMIT = 56 * 1024 * 1024


def _rms(x, axis):
    return x * lax.rsqrt(jnp.mean(x * x, axis=axis, keepdims=True) + NORM_EPS)


def _rope_t(xt, cos, sin):
    x1, x2 = xt[:HALF], xt[HALF:]
    return jnp.concatenate([x1 * cos - x2 * sin, x2 * cos + x1 * sin], axis=0)


def _prep_kernel(x_ref, nw_ref, w_ref, gq_ref, gk_ref, ca_ref, sa_ref, cb_ref, sb_ref,
                 qta_ref, ka_ref, vta_ref, qtb_ref, kb_ref, vtb_ref, kn_ref, qn_ref, wb_ref):
    @pl.when((pl.program_id(0) == 0) & (pl.program_id(1) == 0))
    def _():
        wb_ref[...] = w_ref[...].astype(BF16)

    tm = x_ref.shape[1] // PREP_PARTS
    gq, gk = gq_ref[...], gk_ref[...]
    pad = jnp.zeros((SLAB - HEAD_DIM - SUBLANES, tm), F32)
    one_row = (lax.broadcasted_iota(jnp.int32, (SUBLANES, tm), 0) == 0).astype(F32)

    def norm(v):
        return jnp.sqrt(jnp.sum(v * v, axis=0, keepdims=True))

    def k_slab(k):
        return jnp.concatenate([k, one_row, pad], axis=0).T.astype(BF16)

    groups = [slice(i * tm, (i + 1) * tm) for i in range(PREP_PARTS)]
    zs = []
    for t in groups:
        h = _rms(x_ref[0, t], -1) * nw_ref[...]
        zs.append(jnp.dot(h.astype(BF16), wb_ref[...], preferred_element_type=F32))

    for t, z in zip(groups, zs):
        ca, sa, cb, sb = ca_ref[:, t], sa_ref[:, t], cb_ref[:, t], sb_ref[:, t]
        q_norms, k_norms = [], []

        def q_block(q):
            nq = norm(q).astype(BF16).astype(F32)
            q_norms.append(nq)
            return jnp.concatenate([q, jnp.broadcast_to(nq, (SUBLANES, tm)), pad], axis=0).astype(BF16)

        aqt = z[:, OFF_AQ:OFF_AK].T
        for hd in range(A_Q_HEADS):
            q = aqt[hd * HEAD_DIM:(hd + 1) * HEAD_DIM]
            qta_ref[0, 0, hd, :, t] = q_block(_rope_t(_rms(q, 0) * gq, ca, sa) * QK_SCALE)
        akt = z[:, OFF_AK:OFF_AV].T
        for g in range(A_KV_HEADS):
            k = _rope_t(_rms(akt[g * HEAD_DIM:(g + 1) * HEAD_DIM], 0) * gk, ca, sa)
            ka_ref[0, g, t, :] = k_slab(k)
            k_norms.append(norm(k))
        k_norms.append(jnp.zeros((SUBLANES - A_KV_HEADS, tm), F32))
        vta_ref[0, 0, :, t] = z[:, OFF_AV:OFF_BQ].T.astype(BF16)

        bqt = z[:, OFF_BQ:OFF_BK].T
        bkt = z[:, OFF_BK:OFF_BV].T
        for hd in range(B_HEADS):
            for c in range(2):
                r0 = (hd * 2 + c) * HEAD_DIM
                qtb_ref[0, 0, hd, c, :, t] = q_block(_rope_t(bqt[r0:r0 + HEAD_DIM], cb, sb) * QK_SCALE)
                k = _rope_t(bkt[r0:r0 + HEAD_DIM], cb, sb)
                kb_ref[0, hd, c, t, :] = k_slab(k)
                k_norms.append(norm(k))
        vtb_ref[0, 0, :, t] = z[:, OFF_BV:OFF_GATES].T.astype(BF16)
        kn_ref[0, :, t] = jnp.concatenate(k_norms, axis=0)
        qn_ref[0, :, t] = jnp.concatenate(q_norms, axis=0)


def _plain_query(qt):
    qf = qt.astype(F32)
    rows = lax.broadcasted_iota(jnp.int32, qf.shape, 0)
    return jnp.where(rows < HEAD_DIM, qf, 0.0).astype(BF16)


def _shifted_query(qt, kmax):
    qf = qt.astype(F32)
    rows = lax.broadcasted_iota(jnp.int32, qf.shape, 0)
    shifted = jnp.where(rows == HEAD_DIM, -(qf * kmax), jnp.where(rows < HEAD_DIM, qf, 0.0))
    return shifted.astype(BF16)


def _flash_fast(q_list, k_of, vt_ref, v_rows, l_ref, acc_ref, sb, skew):
    n = len(q_list)
    vb = vt_ref.shape[3]
    pb = min(sb, vb)
    l_ref[...] = jnp.zeros(l_ref.shape, F32)
    acc_ref[...] = jnp.zeros(acc_ref.shape, F32)

    def body(i, carry):
        off = pl.multiple_of(i * FAST_KC, FAST_KC)
        l = [l_ref[j] for j in range(n)]
        acc = [acc_ref[j] for j in range(n)]
        items = [(g, j) for g in range(FAST_KC // sb) for j in range(n)]
        scores = {}

        def consume(g, j):
            p = jnp.exp2(scores.pop((g, j)))
            l[j] = l[j] + jnp.sum(p, axis=0, keepdims=True)
            p = p.astype(BF16)
            for u in range(sb // pb):
                k0 = g * sb + u * pb
                vblk = vt_ref[0, i * (FAST_KC // vb) + k0 // vb, v_rows(j), k0 % vb:k0 % vb + pb]
                acc[j] = acc[j] + jnp.dot(vblk, p[u * pb:(u + 1) * pb], preferred_element_type=F32)

        for t, (g, j) in enumerate(items):
            scores[(g, j)] = jnp.dot(k_of(j, off + g * sb, sb), q_list[j],
                                     preferred_element_type=F32)
            if t >= skew:
                consume(*items[t - skew])
        for g, j in items[max(len(items) - skew, 0):]:
            consume(g, j)
        for j in range(n):
            l_ref[j], acc_ref[j] = l[j], acc[j]
        return carry

    lax.fori_loop(0, vt_ref.shape[1] * vb // FAST_KC, body, 0)


def _flash_streams(q_list, k_of, vt_ref, v_rows, s_buf, mx_buf, m_ref, l_ref, acc_ref):
    n = len(q_list)
    n_chunks, kc = vt_ref.shape[1], vt_ref.shape[3]
    m_ref[...] = jnp.full(m_ref.shape, -jnp.inf, F32)
    l_ref[...] = jnp.zeros(l_ref.shape, F32)
    acc_ref[...] = jnp.zeros(acc_ref.shape, F32)

    def scores(c, slot):
        off = c * kc if isinstance(c, int) else pl.multiple_of(c * kc, kc)
        for j in range(n):
            s = jnp.dot(k_of(j, off, kc), q_list[j], preferred_element_type=F32)
            s_buf[slot, j] = s
            mx_buf[slot, j] = jnp.max(s, axis=0, keepdims=True)

    def softmax_pv(c, slot):
        for j in range(n):
            vblk = vt_ref[0, c, v_rows(j)]
            m_old = m_ref[j]
            m_new = jnp.maximum(m_old, mx_buf[slot, j])
            alpha = jnp.exp2(m_old - m_new)
            p = jnp.exp2(s_buf[slot, j] - m_new)
            l_ref[j] = alpha * l_ref[j] + jnp.sum(p, axis=0, keepdims=True)
            acc_ref[j] = alpha * acc_ref[j] + jnp.dot(vblk, p.astype(BF16),
                                                      preferred_element_type=F32)
            m_ref[j] = m_new

    n_slots = s_buf.shape[0]

    def stage(c, u):
        if not isinstance(c, int) or c + SCORE_LAG < n_chunks:
            scores(c + SCORE_LAG, (u + SCORE_LAG) % n_slots)
        softmax_pv(c, u)

    for c in range(SCORE_LAG):
        scores(c, c % n_slots)
    n_loop = (n_chunks - SCORE_LAG) // n_slots

    def body(i, carry):
        for u in range(n_slots):
            stage(i * n_slots + u, u)
        return carry

    lax.fori_loop(0, n_loop, body, 0)
    for c in range(n_loop * n_slots, n_chunks):
        stage(c, c % n_slots)


def _finish_a(o_ref, l_ref, acc_ref):
    outs = [acc_ref[j] * (1.0 / l_ref[j]) for j in range(A_STREAMS)]
    o_ref[0] = jnp.concatenate(outs, axis=0).T.astype(BF16)


def _finish_b(lambda_init, lam_refs, sub_ref, o_ref, l_ref, acc_ref):
    lq1_ref, lk1_ref, lq2_ref, lk2_ref = lam_refs
    lam = (jnp.exp(jnp.sum(lq1_ref[...] * lk1_ref[...], axis=-1, keepdims=True))
           - jnp.exp(jnp.sum(lq2_ref[...] * lk2_ref[...], axis=-1, keepdims=True))
           + lambda_init)
    outs = []
    for hd in range(B_STEP_HEADS):
        j1, j2 = 2 * hd, 2 * hd + 1
        o = acc_ref[j1] * (1.0 / l_ref[j1]) - lam * (acc_ref[j2] * (1.0 / l_ref[j2]))
        outs.append(_rms(o, 0) * sub_ref[...] * (1.0 - lambda_init))
    o_ref[0] = jnp.concatenate(outs, axis=0).T.astype(BF16)


def _all_rows(j):
    return slice(None)


def _b_rows(j):
    return slice((j // 2) * B_V_DIM, (j // 2 + 1) * B_V_DIM)


def _k_of_a(k_ref):
    return lambda j, off, size: k_ref[0, 0, pl.ds(off, size), :]


def _k_of_b(k_ref):
    return lambda j, off, size: k_ref[0, j // 2, j % 2, pl.ds(off, size), :]


def _attn_a_general(qt_ref, k_ref, vt_ref, o_ref, s_buf, mx_buf, m_ref, l_ref, acc_ref):
    q_list = [_plain_query(qt_ref[0, 0, j]) for j in range(A_STREAMS)]
    _flash_streams(q_list, _k_of_a(k_ref), vt_ref, _all_rows, s_buf, mx_buf, m_ref, l_ref, acc_ref)
    _finish_a(o_ref, l_ref, acc_ref)


def _attn_a_fast(qt_ref, k_ref, vt_ref, kmax_ref, o_ref, l_ref, acc_ref):
    q_list = [_shifted_query(qt_ref[0, 0, j], kmax_ref[0, 0]) for j in range(A_STREAMS)]
    _flash_fast(q_list, _k_of_a(k_ref), vt_ref, _all_rows, l_ref, acc_ref, *FAST_ORDER_A)
    _finish_a(o_ref, l_ref, acc_ref)


def _attn_b_general(lambda_init, qt_ref, k_ref, vt_ref, lq1_ref, lk1_ref, lq2_ref, lk2_ref,
                    sub_ref, o_ref, s_buf, mx_buf, m_ref, l_ref, acc_ref):
    q_list = [_plain_query(qt_ref[0, 0, hd, c]) for hd in range(B_STEP_HEADS) for c in range(2)]
    _flash_streams(q_list, _k_of_b(k_ref), vt_ref, _b_rows, s_buf, mx_buf, m_ref, l_ref, acc_ref)
    _finish_b(lambda_init, (lq1_ref, lk1_ref, lq2_ref, lk2_ref), sub_ref, o_ref, l_ref, acc_ref)


def _attn_b_fast(lambda_init, qt_ref, k_ref, vt_ref, kmax_ref, lq1_ref, lk1_ref, lq2_ref, lk2_ref,
                 sub_ref, o_ref, l_ref, acc_ref):
    q_list = [_shifted_query(qt_ref[0, 0, hd, c], kmax_ref[0, hd, c])
              for hd in range(B_STEP_HEADS) for c in range(2)]
    _flash_fast(q_list, _k_of_b(k_ref), vt_ref, _b_rows, l_ref, acc_ref, *FAST_ORDER_B)
    _finish_b(lambda_init, (lq1_ref, lk1_ref, lq2_ref, lk2_ref), sub_ref, o_ref, l_ref, acc_ref)


def _row_parts(ref):
    rows = ref.shape[0] // DENSE_PARTS
    return [slice(i * rows, (i + 1) * rows) for i in range(DENSE_PARTS)]


def _merge_kernel(x_ref, oa_ref, ob_ref, nw_ref, win_ref, wa_ref, wb_ref, wo_ref, x1_ref,
                  wg_s, wa_s, wb_s, wo_s):
    @pl.when(pl.program_id(0) == 0)
    def _():
        wg_s[...] = win_ref[:, OFF_GATES:].astype(BF16)
        wa_s[...] = wa_ref[...].astype(BF16)
        wb_s[...] = wb_ref[...].astype(BF16)
        wo_s[...] = wo_ref[...].astype(BF16)

    d = x_ref.shape[-1]
    parts = _row_parts(x_ref)
    stage1 = []
    for r in parts:
        x = x_ref[r]
        h = (_rms(x, -1) * nw_ref[...]).astype(BF16)
        gates = jnp.dot(h, wg_s[...], preferred_element_type=F32)
        ya = jnp.dot(oa_ref[r], wa_s[...], preferred_element_type=F32)
        yb = jnp.dot(ob_ref[r], wb_s[...], preferred_element_type=F32)
        stage1.append((x, gates, ya, yb))
    for r, (x, gates, ya, yb) in zip(parts, stage1):
        y = jax.nn.sigmoid(gates[:, :d]) * ya + jax.nn.sigmoid(gates[:, d:]) * yb
        x1_ref[r] = x + jnp.dot(y.astype(BF16), wo_s[...], preferred_element_type=F32)


def _cast_kernel(*refs):
    n = len(refs) // 2
    for src, dst in zip(refs[:n], refs[n:]):
        dst[...] = src[...].astype(dst.dtype)


def _ffn_kernel(final, x_ref, nw_ref, wg_ref, wu_ref, wd_ref, nf_ref, o_ref):
    parts = _row_parts(x_ref)
    stage1 = []
    for r in parts:
        x = x_ref[r]
        h = (_rms(x, -1) * nw_ref[...]).astype(BF16)
        gate = jnp.dot(h, wg_ref[...], preferred_element_type=F32)
        up = jnp.dot(h, wu_ref[...], preferred_element_type=F32)
        stage1.append((x, gate, up))
    for r, (x, gate, up) in zip(parts, stage1):
        act = (jax.nn.silu(gate) * up).astype(BF16)
        x2 = x + jnp.dot(act, wd_ref[...], preferred_element_type=F32)
        if final:
            x2 = _rms(x2, -1) * nf_ref[...]
        o_ref[r] = x2


def _const_spec(shape):
    nd = len(shape)
    return pl.BlockSpec(shape, lambda *_: (0,) * nd, pipeline_mode=pl.Buffered(1))


def _rope_angles_t(pos, dim, theta):
    inv_freq = theta ** (-jnp.arange(0, dim, 2, dtype=F32) / dim)
    return (pos[:, None] * inv_freq[None, :]).T


def _angle_tables(seq_len):
    rows = seq_len // GRID_W
    row = jnp.broadcast_to(jnp.arange(rows, dtype=F32)[:, None], (rows, GRID_W)).reshape(-1)
    col = jnp.broadcast_to(jnp.arange(GRID_W, dtype=F32)[None, :], (rows, GRID_W)).reshape(-1)
    ang_a = jnp.concatenate([_rope_angles_t(row, HALF, AXIAL_THETA),
                             _rope_angles_t(col, HALF, AXIAL_THETA)], axis=0)
    ang_b = _rope_angles_t(jnp.arange(seq_len, dtype=F32), HEAD_DIM, ROPE_THETA)
    return jnp.cos(ang_a), jnp.sin(ang_a), jnp.cos(ang_b), jnp.sin(ang_b)


def _params(*sem):
    return pltpu.CompilerParams(dimension_semantics=sem, vmem_limit_bytes=VMEM_LIMIT)


def _layer(x, lambda_init, final, norm_mix, w_in, q_norm_a, k_norm_a, lq1, lk1, lq2, lk2, subln,
           w_proj_a, w_proj_b, w_out, norm_ffn, w_gate, w_up, w_down, norm_final, tables):
    b, s, d = x.shape
    d_ff = w_gate.shape[-1]
    n_tok = b * s
    n_chunks = s // PREP_TM
    ca, sa, cb, sb = tables
    row = lambda v: v.reshape(1, -1).astype(F32)
    col = lambda v: v.reshape(-1, 1).astype(F32)

    tab_spec = pl.BlockSpec((HALF, PREP_TM), lambda bi, ti: (0, ti))
    qta, ka, vta, qtb, kb, vtb, kn, qn = pl.pallas_call(
        _prep_kernel,
        grid=(b, n_chunks),
        in_specs=[
            pl.BlockSpec((1, PREP_TM, d), lambda bi, ti: (bi, ti, 0)),
            _const_spec((1, d)),
            _const_spec((d, OFF_GATES)),
            _const_spec((HEAD_DIM, 1)),
            _const_spec((HEAD_DIM, 1)),
            tab_spec, tab_spec, tab_spec, tab_spec,
        ],
        out_specs=[
            pl.BlockSpec((1, 1, A_Q_HEADS, SLAB, PREP_TM), lambda bi, ti: (bi, ti, 0, 0, 0)),
            pl.BlockSpec((1, A_KV_HEADS, PREP_TM, SLAB), lambda bi, ti: (bi, 0, ti, 0)),
            pl.BlockSpec((1, 1, A_KV_HEADS * HEAD_DIM, PREP_TM), lambda bi, ti: (bi, ti, 0, 0)),
            pl.BlockSpec((1, 1, B_HEADS, 2, SLAB, PREP_TM), lambda bi, ti: (bi, ti, 0, 0, 0, 0)),
            pl.BlockSpec((1, B_HEADS, 2, PREP_TM, SLAB), lambda bi, ti: (bi, 0, 0, ti, 0)),
            pl.BlockSpec((1, 1, B_WIDTH, PREP_TM), lambda bi, ti: (bi, ti, 0, 0)),
            pl.BlockSpec((1, NORM_ROWS, PREP_TM), lambda bi, ti: (bi, 0, ti)),
            pl.BlockSpec((1, NORM_ROWS, PREP_TM), lambda bi, ti: (bi, 0, ti)),
        ],
        out_shape=[
            jax.ShapeDtypeStruct((b, n_chunks, A_Q_HEADS, SLAB, PREP_TM), BF16),
            jax.ShapeDtypeStruct((b, A_KV_HEADS, s, SLAB), BF16),
            jax.ShapeDtypeStruct((b, n_chunks, A_KV_HEADS * HEAD_DIM, PREP_TM), BF16),
            jax.ShapeDtypeStruct((b, n_chunks, B_HEADS, 2, SLAB, PREP_TM), BF16),
            jax.ShapeDtypeStruct((b, B_HEADS, 2, s, SLAB), BF16),
            jax.ShapeDtypeStruct((b, n_chunks, B_WIDTH, PREP_TM), BF16),
            jax.ShapeDtypeStruct((b, NORM_ROWS, s), F32),
            jax.ShapeDtypeStruct((b, NORM_ROWS, s), F32),
        ],
        scratch_shapes=[pltpu.VMEM((d, OFF_GATES), BF16)],
        compiler_params=_params("arbitrary", "arbitrary"),
        name="prep",
    )(x, row(norm_mix), w_in, col(q_norm_a), col(k_norm_a), ca, sa, cb, sb)

    nq = s // ATTN_TQ
    steps_per_group = A_GROUP // A_STREAMS
    kmax_a = jnp.max(kn[:, :A_KV_HEADS], axis=-1)
    kmax_b = jnp.max(kn[:, SUBLANES:], axis=-1).reshape(b, B_HEADS, 2)
    qmax_a = jnp.max(qn[:, :A_Q_HEADS], axis=-1)
    qmax_b = jnp.max(qn[:, A_Q_HEADS:], axis=-1).reshape(b, B_HEADS, 2)
    fast_a = jnp.max(qmax_a * jnp.repeat(kmax_a, A_GROUP, axis=1)) < FAST_BOUND
    fast_b = jnp.max(qmax_b * kmax_b) < FAST_BOUND

    ffn_shapes = [w_gate.shape, w_up.shape, w_down.shape]
    ffn_specs = [pl.BlockSpec((sh[0] // CAST_STEPS, sh[1]), lambda i: (i, 0)) for sh in ffn_shapes]
    ffn_w = pl.pallas_call(
        _cast_kernel,
        grid=(CAST_STEPS,),
        in_specs=ffn_specs,
        out_specs=ffn_specs,
        out_shape=[jax.ShapeDtypeStruct(sh, BF16) for sh in ffn_shapes],
        compiler_params=_params("parallel"),
        name="ffn_weights",
    )(w_gate, w_up, w_down)
    fast_a, fast_b, ffn_w = lax.optimization_barrier((fast_a, fast_b, ffn_w))

    def general_scratch(n, dv):
        return [
            pltpu.VMEM((SCORE_SLOTS, n, PREP_TM, ATTN_TQ), F32),
            pltpu.VMEM((SCORE_SLOTS, n, 1, ATTN_TQ), F32),
            pltpu.VMEM((n, 1, ATTN_TQ), F32),
            pltpu.VMEM((n, 1, ATTN_TQ), F32),
            pltpu.VMEM((n, dv, ATTN_TQ), F32),
        ]

    def fast_scratch(n, dv):
        return [pltpu.VMEM((n, 1, ATTN_TQ), F32), pltpu.VMEM((n, dv, ATTN_TQ), F32)]

    a_specs = [
        pl.BlockSpec((1, 1, A_STREAMS, SLAB, ATTN_TQ), lambda bi, hp, qi: (bi, qi, hp, 0, 0)),
        pl.BlockSpec((1, 1, s, SLAB), lambda bi, hp, qi: (bi, hp // steps_per_group, 0, 0)),
        pl.BlockSpec((1, n_chunks, HEAD_DIM, PREP_TM),
                     lambda bi, hp, qi: (bi, 0, hp // steps_per_group, 0)),
    ]
    a_kmax_spec = pl.BlockSpec((1, 1, 1, ATTN_TQ), lambda bi, hp, qi: (bi, hp // steps_per_group, 0, 0))
    a_common = dict(
        grid=(b, A_Q_HEADS // A_STREAMS, nq),
        out_specs=pl.BlockSpec((1, ATTN_TQ, A_STREAMS * HEAD_DIM), lambda bi, hp, qi: (bi, qi, hp)),
        out_shape=jax.ShapeDtypeStruct((b, s, A_WIDTH), BF16),
        compiler_params=_params("parallel", "parallel", "arbitrary"),
    )

    kmax_a_rows = jnp.broadcast_to(kmax_a[:, :, None, None], (b, A_KV_HEADS, 1, ATTN_TQ))
    kmax_b_rows = jnp.broadcast_to(kmax_b[:, :, :, None, None], (b, B_HEADS, 2, 1, ATTN_TQ))

    def attn_a_fast():
        return pl.pallas_call(_attn_a_fast, in_specs=a_specs + [a_kmax_spec],
                              scratch_shapes=fast_scratch(A_STREAMS, HEAD_DIM), name="attn_a_fast",
                              **a_common)(qta, ka, vta, kmax_a_rows)

    def attn_a_general():
        return pl.pallas_call(_attn_a_general, in_specs=a_specs,
                              scratch_shapes=general_scratch(A_STREAMS, HEAD_DIM), name="attn_a",
                              **a_common)(qta, ka, vta)

    oa = lax.cond(fast_a, attn_a_fast, attn_a_general)

    lam_spec = _const_spec((1, HEAD_DIM))
    hb = B_STEP_HEADS
    b_specs = [
        pl.BlockSpec((1, 1, hb, 2, SLAB, ATTN_TQ), lambda bi, hp, qi: (bi, qi, hp, 0, 0, 0)),
        pl.BlockSpec((1, hb, 2, s, SLAB), lambda bi, hp, qi: (bi, hp, 0, 0, 0)),
        pl.BlockSpec((1, n_chunks, hb * B_V_DIM, PREP_TM), lambda bi, hp, qi: (bi, 0, hp, 0)),
    ]
    b_kmax_spec = pl.BlockSpec((1, hb, 2, 1, ATTN_TQ), lambda bi, hp, qi: (bi, hp, 0, 0, 0))
    b_tail_specs = [lam_spec, lam_spec, lam_spec, lam_spec, _const_spec((B_V_DIM, 1))]
    b_tail = (row(lq1), row(lk1), row(lq2), row(lk2), col(subln))
    b_common = dict(
        grid=(b, B_HEADS // hb, nq),
        out_specs=pl.BlockSpec((1, ATTN_TQ, hb * B_V_DIM), lambda bi, hp, qi: (bi, qi, hp)),
        out_shape=jax.ShapeDtypeStruct((b, s, B_WIDTH), BF16),
        compiler_params=_params("parallel", "parallel", "arbitrary"),
    )

    def attn_b_fast():
        return pl.pallas_call(functools.partial(_attn_b_fast, lambda_init),
                              in_specs=b_specs + [b_kmax_spec] + b_tail_specs,
                              scratch_shapes=fast_scratch(2 * hb, B_V_DIM), name="attn_b_fast",
                              **b_common)(qtb, kb, vtb, kmax_b_rows, *b_tail)

    def attn_b_general():
        return pl.pallas_call(functools.partial(_attn_b_general, lambda_init),
                              in_specs=b_specs + b_tail_specs,
                              scratch_shapes=general_scratch(2 * hb, B_V_DIM), name="attn_b",
                              **b_common)(qtb, kb, vtb, *b_tail)

    ob = lax.cond(fast_b, attn_b_fast, attn_b_general)

    tok_spec = lambda width: pl.BlockSpec((DENSE_TM, width), lambda ti: (ti, 0))
    x_flat = x.reshape(n_tok, d)
    x1 = pl.pallas_call(
        _merge_kernel,
        grid=(n_tok // DENSE_TM,),
        in_specs=[
            tok_spec(d), tok_spec(A_WIDTH), tok_spec(B_WIDTH),
            _const_spec((1, d)),
            _const_spec(w_in.shape),
            _const_spec((A_WIDTH, d)),
            _const_spec((B_WIDTH, d)),
            _const_spec((d, d)),
        ],
        out_specs=tok_spec(d),
        out_shape=jax.ShapeDtypeStruct((n_tok, d), F32),
        scratch_shapes=[pltpu.VMEM((d, 2 * d), BF16), pltpu.VMEM((A_WIDTH, d), BF16),
                        pltpu.VMEM((B_WIDTH, d), BF16), pltpu.VMEM((d, d), BF16)],
        compiler_params=_params("arbitrary"),
        name="merge",
    )(x_flat, oa.reshape(n_tok, A_WIDTH), ob.reshape(n_tok, B_WIDTH), row(norm_mix),
      w_in, w_proj_a, w_proj_b, w_out)

    x2 = pl.pallas_call(
        functools.partial(_ffn_kernel, final),
        grid=(n_tok // DENSE_TM,),
        in_specs=[
            tok_spec(d),
            _const_spec((1, d)),
            _const_spec((d, d_ff)),
            _const_spec((d, d_ff)),
            _const_spec((d_ff, d)),
            _const_spec((1, d)),
        ],
        out_specs=tok_spec(d),
        out_shape=jax.ShapeDtypeStruct((n_tok, d), F32),
        compiler_params=_params("parallel"),
        name="ffn",
    )(x1, row(norm_ffn), *ffn_w, row(norm_final))
    return x2.reshape(b, s, d)


def kernel(x, norm_mix, w_in, q_norm_a, k_norm_a, lambda_q1, lambda_k1, lambda_q2, lambda_k2,
           subln_b, w_proj_a, w_proj_b, w_out, norm_ffn, w_gate_ffn, w_up_ffn, w_down_ffn,
           norm_final):
    depth = norm_mix.shape[0]
    tables = _angle_tables(x.shape[1])
    for l in range(depth):
        lambda_init = 0.8 - 0.6 * math.exp(-0.3 * l)
        x = _layer(x, lambda_init, l == depth - 1, norm_mix[l], w_in[l], q_norm_a[l], k_norm_a[l],
                   lambda_q1[l], lambda_k1[l], lambda_q2[l], lambda_k2[l], subln_b[l],
                   w_proj_a[l], w_proj_b[l], w_out[l], norm_ffn[l], w_gate_ffn[l], w_up_ffn[l],
                   w_down_ffn[l], norm_final, tables)
    return x
```

```python
import functools
import math

import jax
import jax.numpy as jnp
from jax import lax
from jax.experimental import pallas as pl
from jax.experimental.pallas import tpu as pltpu

F32 = jnp.float32
BF16 = jnp.bfloat16

GRID_W = 64
HEAD_DIM = 64
HALF = HEAD_DIM // 2
A_Q_HEADS = 8
A_KV_HEADS = 2
A_GROUP = A_Q_HEADS // A_KV_HEADS
A_WIDTH = A_Q_HEADS * HEAD_DIM
B_HEADS = 4
B_V_DIM = 2 * HEAD_DIM
B_WIDTH = B_HEADS * B_V_DIM
ROPE_THETA = 10000.0
AXIAL_THETA = 10000.0
NORM_EPS = 1e-6
QK_SCALE = math.log2(math.e) / math.sqrt(HEAD_DIM)
A_STREAMS = 4
B_STEP_HEADS = 2
SCORE_LAG = 2
SCORE_SLOTS = 3

OFF_AQ = 0
OFF_AK = OFF_AQ + A_WIDTH
OFF_AV = OFF_AK + A_KV_HEADS * HEAD_DIM
OFF_BQ = OFF_AV + A_KV_HEADS * HEAD_DIM
OFF_BK = OFF_BQ + B_HEADS * 2 * HEAD_DIM
OFF_BV = OFF_BK + B_HEADS * 2 * HEAD_DIM
OFF_GATES = OFF_BV + B_WIDTH
SLAB = 2 * HEAD_DIM
SUBLANES = 8
NORM_ROWS = 2 * SUBLANES
FAST_BOUND = 40.0

PREP_TM = 512
PREP_PARTS = 2
ATTN_TQ = PREP_TM
FAST_KC = 8192
FAST_ORDER_A = (256, 4)
FAST_ORDER_B = (4096, 2)
DENSE_TM = 512
DENSE_PARTS = 2
FFN_TM = 1024
FFN_PARTS = 4
CAST_STEPS = 8
VMEM_LIMIT = 56 * 1024 * 1024


def _rms(x, axis):
    return x * lax.rsqrt(jnp.mean(x * x, axis=axis, keepdims=True) + NORM_EPS)


def _rope_t(xt, cos, sin):
    x1, x2 = xt[:HALF], xt[HALF:]
    return jnp.concatenate([x1 * cos - x2 * sin, x2 * cos + x1 * sin], axis=0)


def _prep_kernel(x_ref, nw_ref, w_ref, gq_ref, gk_ref, ca_ref, sa_ref, cb_ref, sb_ref,
                 qta_ref, ka_ref, vta_ref, qtb_ref, kb_ref, vtb_ref, kn_ref, qn_ref, wb_ref):
    @pl.when((pl.program_id(0) == 0) & (pl.program_id(1) == 0))
    def _():
        wb_ref[...] = w_ref[...].astype(BF16)

    tm = x_ref.shape[1] // PREP_PARTS
    gq, gk = gq_ref[...], gk_ref[...]
    pad = jnp.zeros((SLAB - HEAD_DIM - SUBLANES, tm), F32)
    one_row = (lax.broadcasted_iota(jnp.int32, (SUBLANES, tm), 0) == 0).astype(F32)

    def norm(v):
        return jnp.sqrt(jnp.sum(v * v, axis=0, keepdims=True))

    def k_slab(k):
        return jnp.concatenate([k, one_row, pad], axis=0).T.astype(BF16)

    groups = [slice(i * tm, (i + 1) * tm) for i in range(PREP_PARTS)]
    zs = []
    for t in groups:
        h = _rms(x_ref[0, t], -1) * nw_ref[...]
        zs.append(jnp.dot(h.astype(BF16), wb_ref[...], preferred_element_type=F32))

    for t, z in zip(groups, zs):
        ca, sa, cb, sb = ca_ref[:, t], sa_ref[:, t], cb_ref[:, t], sb_ref[:, t]
        q_norms, k_norms = [], []

        def q_block(q):
            nq = norm(q).astype(BF16).astype(F32)
            q_norms.append(nq)
            return jnp.concatenate([q, jnp.broadcast_to(nq, (SUBLANES, tm)), pad], axis=0).astype(BF16)

        aqt = z[:, OFF_AQ:OFF_AK].T
        for hd in range(A_Q_HEADS):
            q = aqt[hd * HEAD_DIM:(hd + 1) * HEAD_DIM]
            qta_ref[0, 0, hd, :, t] = q_block(_rope_t(_rms(q, 0) * gq, ca, sa) * QK_SCALE)
        akt = z[:, OFF_AK:OFF_AV].T
        for g in range(A_KV_HEADS):
            k = _rope_t(_rms(akt[g * HEAD_DIM:(g + 1) * HEAD_DIM], 0) * gk, ca, sa)
            ka_ref[0, g, t, :] = k_slab(k)
            k_norms.append(norm(k))
        k_norms.append(jnp.zeros((SUBLANES - A_KV_HEADS, tm), F32))
        vta_ref[0, 0, :, t] = z[:, OFF_AV:OFF_BQ].T.astype(BF16)

        bqt = z[:, OFF_BQ:OFF_BK].T
        bkt = z[:, OFF_BK:OFF_BV].T
        for hd in range(B_HEADS):
            for c in range(2):
                r0 = (hd * 2 + c) * HEAD_DIM
                qtb_ref[0, 0, hd, c, :, t] = q_block(_rope_t(bqt[r0:r0 + HEAD_DIM], cb, sb) * QK_SCALE)
                k = _rope_t(bkt[r0:r0 + HEAD_DIM], cb, sb)
                kb_ref[0, hd, c, t, :] = k_slab(k)
                k_norms.append(norm(k))
        vtb_ref[0, 0, :, t] = z[:, OFF_BV:OFF_GATES].T.astype(BF16)
        kn_ref[0, :, t] = jnp.concatenate(k_norms, axis=0)
        qn_ref[0, :, t] = jnp.concatenate(q_norms, axis=0)


def _plain_query(qt):
    qf = qt.astype(F32)
    rows = lax.broadcasted_iota(jnp.int32, qf.shape, 0)
    return jnp.where(rows < HEAD_DIM, qf, 0.0).astype(BF16)


def _shifted_query(qt, kmax):
    qf = qt.astype(F32)
    rows = lax.broadcasted_iota(jnp.int32, qf.shape, 0)
    shifted = jnp.where(rows == HEAD_DIM, -(qf * kmax), jnp.where(rows < HEAD_DIM, qf, 0.0))
    return shifted.astype(BF16)


def _flash_fast(q_list, k_of, vt_ref, v_rows, l_ref, acc_ref, sb, skew):
    n = len(q_list)
    vb = vt_ref.shape[3]
    pb = min(sb, vb)
    l_ref[...] = jnp.zeros(l_ref.shape, F32)
    acc_ref[...] = jnp.zeros(acc_ref.shape, F32)

    def body(i, carry):
        off = pl.multiple_of(i * FAST_KC, FAST_KC)
        l = [l_ref[j] for j in range(n)]
        acc = [acc_ref[j] for j in range(n)]
        items = [(g, j) for g in range(FAST_KC // sb) for j in range(n)]
        scores = {}

        def consume(g, j):
            p = jnp.exp2(scores.pop((g, j)))
            l[j] = l[j] + jnp.sum(p, axis=0, keepdims=True)
            p = p.astype(BF16)
            for u in range(sb // pb):
                k0 = g * sb + u * pb
                vblk = vt_ref[0, i * (FAST_KC // vb) + k0 // vb, v_rows(j), k0 % vb:k0 % vb + pb]
                acc[j] = acc[j] + jnp.dot(vblk, p[u * pb:(u + 1) * pb], preferred_element_type=F32)

        for t, (g, j) in enumerate(items):
            scores[(g, j)] = jnp.dot(k_of(j, off + g * sb, sb), q_list[j],
                                     preferred_element_type=F32)
            if t >= skew:
                consume(*items[t - skew])
        for g, j in items[max(len(items) - skew, 0):]:
            consume(g, j)
        for j in range(n):
            l_ref[j], acc_ref[j] = l[j], acc[j]
        return carry

    lax.fori_loop(0, vt_ref.shape[1] * vb // FAST_KC, body, 0)


def _flash_streams(q_list, k_of, vt_ref, v_rows, s_buf, mx_buf, m_ref, l_ref, acc_ref):
    n = len(q_list)
    n_chunks, kc = vt_ref.shape[1], vt_ref.shape[3]
    m_ref[...] = jnp.full(m_ref.shape, -jnp.inf, F32)
    l_ref[...] = jnp.zeros(l_ref.shape, F32)
    acc_ref[...] = jnp.zeros(acc_ref.shape, F32)

    def scores(c, slot):
        off = c * kc if isinstance(c, int) else pl.multiple_of(c * kc, kc)
        for j in range(n):
            s = jnp.dot(k_of(j, off, kc), q_list[j], preferred_element_type=F32)
            s_buf[slot, j] = s
            mx_buf[slot, j] = jnp.max(s, axis=0, keepdims=True)

    def softmax_pv(c, slot):
        for j in range(n):
            vblk = vt_ref[0, c, v_rows(j)]
            m_old = m_ref[j]
            m_new = jnp.maximum(m_old, mx_buf[slot, j])
            alpha = jnp.exp2(m_old - m_new)
            p = jnp.exp2(s_buf[slot, j] - m_new)
            l_ref[j] = alpha * l_ref[j] + jnp.sum(p, axis=0, keepdims=True)
            acc_ref[j] = alpha * acc_ref[j] + jnp.dot(vblk, p.astype(BF16),
                                                      preferred_element_type=F32)
            m_ref[j] = m_new

    n_slots = s_buf.shape[0]

    def stage(c, u):
        if not isinstance(c, int) or c + SCORE_LAG < n_chunks:
            scores(c + SCORE_LAG, (u + SCORE_LAG) % n_slots)
        softmax_pv(c, u)

    for c in range(SCORE_LAG):
        scores(c, c % n_slots)
    n_loop = (n_chunks - SCORE_LAG) // n_slots

    def body(i, carry):
        for u in range(n_slots):
            stage(i * n_slots + u, u)
        return carry

    lax.fori_loop(0, n_loop, body, 0)
    for c in range(n_loop * n_slots, n_chunks):
        stage(c, c % n_slots)


def _finish_a(o_ref, l_ref, acc_ref):
    outs = [acc_ref[j] * (1.0 / l_ref[j]) for j in range(A_STREAMS)]
    o_ref[0] = jnp.concatenate(outs, axis=0).T.astype(BF16)


def _finish_b(lambda_init, lam_refs, sub_ref, o_ref, l_ref, acc_ref):
    lq1_ref, lk1_ref, lq2_ref, lk2_ref = lam_refs
    lam = (jnp.exp(jnp.sum(lq1_ref[...] * lk1_ref[...], axis=-1, keepdims=True))
           - jnp.exp(jnp.sum(lq2_ref[...] * lk2_ref[...], axis=-1, keepdims=True))
           + lambda_init)
    outs = []
    for hd in range(B_STEP_HEADS):
        j1, j2 = 2 * hd, 2 * hd + 1
        o = acc_ref[j1] * (1.0 / l_ref[j1]) - lam * (acc_ref[j2] * (1.0 / l_ref[j2]))
        outs.append(_rms(o, 0) * sub_ref[...] * (1.0 - lambda_init))
    o_ref[0] = jnp.concatenate(outs, axis=0).T.astype(BF16)


def _all_rows(j):
    return slice(None)


def _b_rows(j):
    return slice((j // 2) * B_V_DIM, (j // 2 + 1) * B_V_DIM)


def _k_of_a(k_ref):
    return lambda j, off, size: k_ref[0, 0, pl.ds(off, size), :]


def _k_of_b(k_ref):
    return lambda j, off, size: k_ref[0, j // 2, j % 2, pl.ds(off, size), :]


def _attn_a_general(qt_ref, k_ref, vt_ref, o_ref, s_buf, mx_buf, m_ref, l_ref, acc_ref):
    q_list = [_plain_query(qt_ref[0, 0, j]) for j in range(A_STREAMS)]
    _flash_streams(q_list, _k_of_a(k_ref), vt_ref, _all_rows, s_buf, mx_buf, m_ref, l_ref, acc_ref)
    _finish_a(o_ref, l_ref, acc_ref)


def _attn_a_fast(qt_ref, k_ref, vt_ref, kmax_ref, o_ref, l_ref, acc_ref):
    q_list = [_shifted_query(qt_ref[0, 0, j], kmax_ref[0, 0]) for j in range(A_STREAMS)]
    _flash_fast(q_list, _k_of_a(k_ref), vt_ref, _all_rows, l_ref, acc_ref, *FAST_ORDER_A)
    _finish_a(o_ref, l_ref, acc_ref)


def _attn_b_general(lambda_init, qt_ref, k_ref, vt_ref, lq1_ref, lk1_ref, lq2_ref, lk2_ref,
                    sub_ref, o_ref, s_buf, mx_buf, m_ref, l_ref, acc_ref):
    q_list = [_plain_query(qt_ref[0, 0, hd, c]) for hd in range(B_STEP_HEADS) for c in range(2)]
    _flash_streams(q_list, _k_of_b(k_ref), vt_ref, _b_rows, s_buf, mx_buf, m_ref, l_ref, acc_ref)
    _finish_b(lambda_init, (lq1_ref, lk1_ref, lq2_ref, lk2_ref), sub_ref, o_ref, l_ref, acc_ref)


def _attn_b_fast(lambda_init, qt_ref, k_ref, vt_ref, kmax_ref, lq1_ref, lk1_ref, lq2_ref, lk2_ref,
                 sub_ref, o_ref, l_ref, acc_ref):
    q_list = [_shifted_query(qt_ref[0, 0, hd, c], kmax_ref[0, hd, c])
              for hd in range(B_STEP_HEADS) for c in range(2)]
    _flash_fast(q_list, _k_of_b(k_ref), vt_ref, _b_rows, l_ref, acc_ref, *FAST_ORDER_B)
    _finish_b(lambda_init, (lq1_ref, lk1_ref, lq2_ref, lk2_ref), sub_ref, o_ref, l_ref, acc_ref)


def _row_parts(ref, n_parts):
    rows = ref.shape[0] // n_parts
    return [slice(i * rows, (i + 1) * rows) for i in range(n_parts)]


def _merge_kernel(x_ref, oa_ref, ob_ref, nw_ref, win_ref, wa_ref, wb_ref, wo_ref, x1_ref,
                  wg_s, wa_s, wb_s, wo_s):
    @pl.when(pl.program_id(0) == 0)
    def _():
        wg_s[...] = win_ref[:, OFF_GATES:].astype(BF16)
        wa_s[...] = wa_ref[...].astype(BF16)
        wb_s[...] = wb_ref[...].astype(BF16)
        wo_s[...] = wo_ref[...].astype(BF16)

    d = x_ref.shape[-1]
    parts = _row_parts(x_ref, DENSE_PARTS)
    stage1 = []
    for r in parts:
        x = x_ref[r]
        h = (_rms(x, -1) * nw_ref[...]).astype(BF16)
        gates = jnp.dot(h, wg_s[...], preferred_element_type=F32)
        ya = jnp.dot(oa_ref[r], wa_s[...], preferred_element_type=F32)
        yb = jnp.dot(ob_ref[r], wb_s[...], preferred_element_type=F32)
        stage1.append((x, gates, ya, yb))
    for r, (x, gates, ya, yb) in zip(parts, stage1):
        y = jax.nn.sigmoid(gates[:, :d]) * ya + jax.nn.sigmoid(gates[:, d:]) * yb
        x1_ref[r] = x + jnp.dot(y.astype(BF16), wo_s[...], preferred_element_type=F32)


def _cast_kernel(*refs):
    n = len(refs) // 2
    for src, dst in zip(refs[:n], refs[n:]):
        dst[...] = src[...].astype(dst.dtype)


def _ffn_kernel(final, x_ref, nw_ref, wg_ref, wu_ref, wd_ref, nf_ref, o_ref):
    def stage1(r):
        x = x_ref[r]
        h = (_rms(x, -1) * nw_ref[...]).astype(BF16)
        gate = jnp.dot(h, wg_ref[...], preferred_element_type=F32)
        up = jnp.dot(h, wu_ref[...], preferred_element_type=F32)
        return r, x, gate, up

    def stage2(r, x, gate, up):
        act = (jax.nn.silu(gate) * up).astype(BF16)
        x2 = x + jnp.dot(act, wd_ref[...], preferred_element_type=F32)
        if final:
            x2 = _rms(x2, -1) * nf_ref[...]
        o_ref[r] = x2

    pending = None
    for r in _row_parts(x_ref, FFN_PARTS):
        nxt = stage1(r)
        if pending is not None:
            stage2(*pending)
        pending = nxt
    stage2(*pending)


def _const_spec(shape):
    nd = len(shape)
    return pl.BlockSpec(shape, lambda *_: (0,) * nd, pipeline_mode=pl.Buffered(1))


def _rope_angles_t(pos, dim, theta):
    inv_freq = theta ** (-jnp.arange(0, dim, 2, dtype=F32) / dim)
    return (pos[:, None] * inv_freq[None, :]).T


def _angle_tables(seq_len):
    rows = seq_len // GRID_W
    row = jnp.broadcast_to(jnp.arange(rows, dtype=F32)[:, None], (rows, GRID_W)).reshape(-1)
    col = jnp.broadcast_to(jnp.arange(GRID_W, dtype=F32)[None, :], (rows, GRID_W)).reshape(-1)
    ang_a = jnp.concatenate([_rope_angles_t(row, HALF, AXIAL_THETA),
                             _rope_angles_t(col, HALF, AXIAL_THETA)], axis=0)
    ang_b = _rope_angles_t(jnp.arange(seq_len, dtype=F32), HEAD_DIM, ROPE_THETA)
    return jnp.cos(ang_a), jnp.sin(ang_a), jnp.cos(ang_b), jnp.sin(ang_b)


def _params(*sem):
    return pltpu.CompilerParams(dimension_semantics=sem, vmem_limit_bytes=VMEM_LIMIT)


def _layer(x, lambda_init, final, norm_mix, w_in, q_norm_a, k_norm_a, lq1, lk1, lq2, lk2, subln,
           w_proj_a, w_proj_b, w_out, norm_ffn, w_gate, w_up, w_down, norm_final, tables):
    b, s, d = x.shape
    d_ff = w_gate.shape[-1]
    n_tok = b * s
    n_chunks = s // PREP_TM
    ca, sa, cb, sb = tables
    row = lambda v: v.reshape(1, -1).astype(F32)
    col = lambda v: v.reshape(-1, 1).astype(F32)

    tab_spec = pl.BlockSpec((HALF, PREP_TM), lambda bi, ti: (0, ti))
    qta, ka, vta, qtb, kb, vtb, kn, qn = pl.pallas_call(
        _prep_kernel,
        grid=(b, n_chunks),
        in_specs=[
            pl.BlockSpec((1, PREP_TM, d), lambda bi, ti: (bi, ti, 0)),
            _const_spec((1, d)),
            _const_spec((d, OFF_GATES)),
            _const_spec((HEAD_DIM, 1)),
            _const_spec((HEAD_DIM, 1)),
            tab_spec, tab_spec, tab_spec, tab_spec,
        ],
        out_specs=[
            pl.BlockSpec((1, 1, A_Q_HEADS, SLAB, PREP_TM), lambda bi, ti: (bi, ti, 0, 0, 0)),
            pl.BlockSpec((1, A_KV_HEADS, PREP_TM, SLAB), lambda bi, ti: (bi, 0, ti, 0)),
            pl.BlockSpec((1, 1, A_KV_HEADS * HEAD_DIM, PREP_TM), lambda bi, ti: (bi, ti, 0, 0)),
            pl.BlockSpec((1, 1, B_HEADS, 2, SLAB, PREP_TM), lambda bi, ti: (bi, ti, 0, 0, 0, 0)),
            pl.BlockSpec((1, B_HEADS, 2, PREP_TM, SLAB), lambda bi, ti: (bi, 0, 0, ti, 0)),
            pl.BlockSpec((1, 1, B_WIDTH, PREP_TM), lambda bi, ti: (bi, ti, 0, 0)),
            pl.BlockSpec((1, NORM_ROWS, PREP_TM), lambda bi, ti: (bi, 0, ti)),
            pl.BlockSpec((1, NORM_ROWS, PREP_TM), lambda bi, ti: (bi, 0, ti)),
        ],
        out_shape=[
            jax.ShapeDtypeStruct((b, n_chunks, A_Q_HEADS, SLAB, PREP_TM), BF16),
            jax.ShapeDtypeStruct((b, A_KV_HEADS, s, SLAB), BF16),
            jax.ShapeDtypeStruct((b, n_chunks, A_KV_HEADS * HEAD_DIM, PREP_TM), BF16),
            jax.ShapeDtypeStruct((b, n_chunks, B_HEADS, 2, SLAB, PREP_TM), BF16),
            jax.ShapeDtypeStruct((b, B_HEADS, 2, s, SLAB), BF16),
            jax.ShapeDtypeStruct((b, n_chunks, B_WIDTH, PREP_TM), BF16),
            jax.ShapeDtypeStruct((b, NORM_ROWS, s), F32),
            jax.ShapeDtypeStruct((b, NORM_ROWS, s), F32),
        ],
        scratch_shapes=[pltpu.VMEM((d, OFF_GATES), BF16)],
        compiler_params=_params("arbitrary", "arbitrary"),
        name="prep",
    )(x, row(norm_mix), w_in, col(q_norm_a), col(k_norm_a), ca, sa, cb, sb)

    nq = s // ATTN_TQ
    steps_per_group = A_GROUP // A_STREAMS
    kmax_a = jnp.max(kn[:, :A_KV_HEADS], axis=-1)
    kmax_b = jnp.max(kn[:, SUBLANES:], axis=-1).reshape(b, B_HEADS, 2)
    qmax_a = jnp.max(qn[:, :A_Q_HEADS], axis=-1)
    qmax_b = jnp.max(qn[:, A_Q_HEADS:], axis=-1).reshape(b, B_HEADS, 2)
    fast_a = jnp.max(qmax_a * jnp.repeat(kmax_a, A_GROUP, axis=1)) < FAST_BOUND
    fast_b = jnp.max(qmax_b * kmax_b) < FAST_BOUND

    def general_scratch(n, dv):
        return [
            pltpu.VMEM((SCORE_SLOTS, n, PREP_TM, ATTN_TQ), F32),
            pltpu.VMEM((SCORE_SLOTS, n, 1, ATTN_TQ), F32),
            pltpu.VMEM((n, 1, ATTN_TQ), F32),
            pltpu.VMEM((n, 1, ATTN_TQ), F32),
            pltpu.VMEM((n, dv, ATTN_TQ), F32),
        ]

    def fast_scratch(n, dv):
        return [pltpu.VMEM((n, 1, ATTN_TQ), F32), pltpu.VMEM((n, dv, ATTN_TQ), F32)]

    a_specs = [
        pl.BlockSpec((1, 1, A_STREAMS, SLAB, ATTN_TQ), lambda bi, hp, qi: (bi, qi, hp, 0, 0)),
        pl.BlockSpec((1, 1, s, SLAB), lambda bi, hp, qi: (bi, hp // steps_per_group, 0, 0)),
        pl.BlockSpec((1, n_chunks, HEAD_DIM, PREP_TM),
                     lambda bi, hp, qi: (bi, 0, hp // steps_per_group, 0)),
    ]
    a_kmax_spec = pl.BlockSpec((1, 1, 1, ATTN_TQ), lambda bi, hp, qi: (bi, hp // steps_per_group, 0, 0))
    a_common = dict(
        grid=(b, A_Q_HEADS // A_STREAMS, nq),
        out_specs=pl.BlockSpec((1, ATTN_TQ, A_STREAMS * HEAD_DIM), lambda bi, hp, qi: (bi, qi, hp)),
        out_shape=jax.ShapeDtypeStruct((b, s, A_WIDTH), BF16),
        compiler_params=_params("parallel", "parallel", "arbitrary"),
    )

    kmax_a_rows = jnp.broadcast_to(kmax_a[:, :, None, None], (b, A_KV_HEADS, 1, ATTN_TQ))
    kmax_b_rows = jnp.broadcast_to(kmax_b[:, :, :, None, None], (b, B_HEADS, 2, 1, ATTN_TQ))

    def attn_a_fast():
        return pl.pallas_call(_attn_a_fast, in_specs=a_specs + [a_kmax_spec],
                              scratch_shapes=fast_scratch(A_STREAMS, HEAD_DIM), name="attn_a_fast",
                              **a_common)(qta, ka, vta, kmax_a_rows)

    def attn_a_general():
        return pl.pallas_call(_attn_a_general, in_specs=a_specs,
                              scratch_shapes=general_scratch(A_STREAMS, HEAD_DIM), name="attn_a",
                              **a_common)(qta, ka, vta)

    oa = lax.cond(fast_a, attn_a_fast, attn_a_general)

    lam_spec = _const_spec((1, HEAD_DIM))
    hb = B_STEP_HEADS
    b_specs = [
        pl.BlockSpec((1, 1, hb, 2, SLAB, ATTN_TQ), lambda bi, hp, qi: (bi, qi, hp, 0, 0, 0)),
        pl.BlockSpec((1, hb, 2, s, SLAB), lambda bi, hp, qi: (bi, hp, 0, 0, 0)),
        pl.BlockSpec((1, n_chunks, hb * B_V_DIM, PREP_TM), lambda bi, hp, qi: (bi, 0, hp, 0)),
    ]
    b_kmax_spec = pl.BlockSpec((1, hb, 2, 1, ATTN_TQ), lambda bi, hp, qi: (bi, hp, 0, 0, 0))
    b_tail_specs = [lam_spec, lam_spec, lam_spec, lam_spec, _const_spec((B_V_DIM, 1))]
    b_tail = (row(lq1), row(lk1), row(lq2), row(lk2), col(subln))
    b_common = dict(
        grid=(b, B_HEADS // hb, nq),
        out_specs=pl.BlockSpec((1, ATTN_TQ, hb * B_V_DIM), lambda bi, hp, qi: (bi, qi, hp)),
        out_shape=jax.ShapeDtypeStruct((b, s, B_WIDTH), BF16),
        compiler_params=_params("parallel", "parallel", "arbitrary"),
    )

    def attn_b_fast():
        return pl.pallas_call(functools.partial(_attn_b_fast, lambda_init),
                              in_specs=b_specs + [b_kmax_spec] + b_tail_specs,
                              scratch_shapes=fast_scratch(2 * hb, B_V_DIM), name="attn_b_fast",
                              **b_common)(qtb, kb, vtb, kmax_b_rows, *b_tail)

    def attn_b_general():
        return pl.pallas_call(functools.partial(_attn_b_general, lambda_init),
                              in_specs=b_specs + b_tail_specs,
                              scratch_shapes=general_scratch(2 * hb, B_V_DIM), name="attn_b",
                              **b_common)(qtb, kb, vtb, *b_tail)

    ob = lax.cond(fast_b, attn_b_fast, attn_b_general)

    tok_spec = lambda width: pl.BlockSpec((DENSE_TM, width), lambda ti: (ti, 0))
    x_flat = x.reshape(n_tok, d)
    x1 = pl.pallas_call(
        _merge_kernel,
        grid=(n_tok // DENSE_TM,),
        in_specs=[
            tok_spec(d), tok_spec(A_WIDTH), tok_spec(B_WIDTH),
            _const_spec((1, d)),
            _const_spec(w_in.shape),
            _const_spec((A_WIDTH, d)),
            _const_spec((B_WIDTH, d)),
            _const_spec((d, d)),
        ],
        out_specs=tok_spec(d),
        out_shape=jax.ShapeDtypeStruct((n_tok, d), F32),
        scratch_shapes=[pltpu.VMEM((d, 2 * d), BF16), pltpu.VMEM((A_WIDTH, d), BF16),
                        pltpu.VMEM((B_WIDTH, d), BF16), pltpu.VMEM((d, d), BF16)],
        compiler_params=_params("arbitrary"),
        name="merge",
    )(x_flat, oa.reshape(n_tok, A_WIDTH), ob.reshape(n_tok, B_WIDTH), row(norm_mix),
      w_in, w_proj_a, w_proj_b, w_out)

    ffn_shapes = [w_gate.shape, w_up.shape, w_down.shape]
    ffn_specs = [pl.BlockSpec((sh[0] // CAST_STEPS, sh[1]), lambda i: (i, 0)) for sh in ffn_shapes]
    ffn_w = pl.pallas_call(
        _cast_kernel,
        grid=(CAST_STEPS,),
        in_specs=ffn_specs,
        out_specs=ffn_specs,
        out_shape=[jax.ShapeDtypeStruct(sh, BF16) for sh in ffn_shapes],
        compiler_params=_params("parallel"),
        name="ffn_weights",
    )(w_gate, w_up, w_down)

    x2 = pl.pallas_call(
        functools.partial(_ffn_kernel, final),
        grid=(n_tok // FFN_TM,),
        in_specs=[
            pl.BlockSpec((FFN_TM, d), lambda ti: (ti, 0)),
            _const_spec((1, d)),
            _const_spec((d, d_ff)),
            _const_spec((d, d_ff)),
            _const_spec((d_ff, d)),
            _const_spec((1, d)),
        ],
        out_specs=pl.BlockSpec((FFN_TM, d), lambda ti: (ti, 0)),
        out_shape=jax.ShapeDtypeStruct((n_tok, d), F32),
        compiler_params=_params("parallel"),
        name="ffn",
    )(x1, row(norm_ffn), *ffn_w, row(norm_final))
    return x2.reshape(b, s, d)


def kernel(x, norm_mix, w_in, q_norm_a, k_norm_a, lambda_q1, lambda_k1, lambda_q2, lambda_k2,
           subln_b, w_proj_a, w_proj_b, w_out, norm_ffn, w_gate_ffn, w_up_ffn, w_down_ffn,
           norm_final):
    depth = norm_mix.shape[0]
    tables = _angle_tables(x.shape[1])
    for l in range(depth):
        lambda_init = 0.8 - 0.6 * math.exp(-0.3 * l)
        x = _layer(x, lambda_init, l == depth - 1, norm_mix[l], w_in[l], q_norm_a[l], k_norm_a[l],
                   lambda_q1[l], lambda_k1[l], lambda_q2[l], lambda_k2[l], subln_b[l],
                   w_proj_a[l], w_proj_b[l], w_out[l], norm_ffn[l], w_gate_ffn[l], w_up_ffn[l],
                   w_down_ffn[l], norm_final, tables)
    return x
```

```python
import functools
import math

import jax
import jax.numpy as jnp
from jax import lax
from jax.experimental import pallas as pl
from jax.experimental.pallas import tpu as pltpu

F32 = jnp.float32
BF16 = jnp.bfloat16

GRID_W = 64
HEAD_DIM = 64
HALF = HEAD_DIM // 2
A_Q_HEADS = 8
A_KV_HEADS = 2
A_GROUP = A_Q_HEADS // A_KV_HEADS
A_WIDTH = A_Q_HEADS * HEAD_DIM
B_HEADS = 4
B_V_DIM = 2 * HEAD_DIM
B_WIDTH = B_HEADS * B_V_DIM
ROPE_THETA = 10000.0
AXIAL_THETA = 10000.0
NORM_EPS = 1e-6
QK_SCALE = math.log2(math.e) / math.sqrt(HEAD_DIM)
A_STREAMS = 4
B_STEP_HEADS = 2
SCORE_LAG = 2
SCORE_SLOTS = 3

OFF_AQ = 0
OFF_AK = OFF_AQ + A_WIDTH
OFF_AV = OFF_AK + A_KV_HEADS * HEAD_DIM
OFF_BQ = OFF_AV + A_KV_HEADS * HEAD_DIM
OFF_BK = OFF_BQ + B_HEADS * 2 * HEAD_DIM
OFF_BV = OFF_BK + B_HEADS * 2 * HEAD_DIM
OFF_GATES = OFF_BV + B_WIDTH
SLAB = 2 * HEAD_DIM
SUBLANES = 8
NORM_ROWS = 2 * SUBLANES
FAST_BOUND = 40.0

PREP_TM = 512
PREP_PARTS = 2
ATTN_TQ = PREP_TM
FAST_KC = 8192
FAST_ORDER_A = (256, 4)
FAST_ORDER_B = (4096, 2)
DENSE_TM = 512
DENSE_PARTS = 2
FFN_TM = 1024
FFN_PARTS = 4
VMEM_LIMIT = 56 * 1024 * 1024


def _rms(x, axis):
    return x * lax.rsqrt(jnp.mean(x * x, axis=axis, keepdims=True) + NORM_EPS)


def _rope_t(xt, cos, sin):
    x1, x2 = xt[:HALF], xt[HALF:]
    return jnp.concatenate([x1 * cos - x2 * sin, x2 * cos + x1 * sin], axis=0)


def _prep_kernel(x_ref, nw_ref, w_ref, gq_ref, gk_ref, ca_ref, sa_ref, cb_ref, sb_ref,
                 qta_ref, ka_ref, vta_ref, qtb_ref, kb_ref, vtb_ref, kn_ref, qn_ref, wb_ref):
    @pl.when((pl.program_id(0) == 0) & (pl.program_id(1) == 0))
    def _():
        wb_ref[...] = w_ref[...].astype(BF16)

    tm = x_ref.shape[1] // PREP_PARTS
    gq, gk = gq_ref[...], gk_ref[...]
    pad = jnp.zeros((SLAB - HEAD_DIM - SUBLANES, tm), F32)
    one_row = (lax.broadcasted_iota(jnp.int32, (SUBLANES, tm), 0) == 0).astype(F32)

    def norm(v):
        return jnp.sqrt(jnp.sum(v * v, axis=0, keepdims=True))

    def k_slab(k):
        return jnp.concatenate([k, one_row, pad], axis=0).T.astype(BF16)

    groups = [slice(i * tm, (i + 1) * tm) for i in range(PREP_PARTS)]
    zs = []
    for t in groups:
        h = _rms(x_ref[0, t], -1) * nw_ref[...]
        zs.append(jnp.dot(h.astype(BF16), wb_ref[...], preferred_element_type=F32))

    for t, z in zip(groups, zs):
        ca, sa, cb, sb = ca_ref[:, t], sa_ref[:, t], cb_ref[:, t], sb_ref[:, t]
        q_norms, k_norms = [], []

        def q_block(q):
            nq = norm(q).astype(BF16).astype(F32)
            q_norms.append(nq)
            return jnp.concatenate([q, jnp.broadcast_to(nq, (SUBLANES, tm)), pad], axis=0).astype(BF16)

        aqt = z[:, OFF_AQ:OFF_AK].T
        for hd in range(A_Q_HEADS):
            q = aqt[hd * HEAD_DIM:(hd + 1) * HEAD_DIM]
            qta_ref[0, 0, hd, :, t] = q_block(_rope_t(_rms(q, 0) * gq, ca, sa) * QK_SCALE)
        akt = z[:, OFF_AK:OFF_AV].T
        for g in range(A_KV_HEADS):
            k = _rope_t(_rms(akt[g * HEAD_DIM:(g + 1) * HEAD_DIM], 0) * gk, ca, sa)
            ka_ref[0, g, t, :] = k_slab(k)
            k_norms.append(norm(k))
        k_norms.append(jnp.zeros((SUBLANES - A_KV_HEADS, tm), F32))
        vta_ref[0, 0, :, t] = z[:, OFF_AV:OFF_BQ].T.astype(BF16)

        bqt = z[:, OFF_BQ:OFF_BK].T
        bkt = z[:, OFF_BK:OFF_BV].T
        for hd in range(B_HEADS):
            for c in range(2):
                r0 = (hd * 2 + c) * HEAD_DIM
                qtb_ref[0, 0, hd, c, :, t] = q_block(_rope_t(bqt[r0:r0 + HEAD_DIM], cb, sb) * QK_SCALE)
                k = _rope_t(bkt[r0:r0 + HEAD_DIM], cb, sb)
                kb_ref[0, hd, c, t, :] = k_slab(k)
                k_norms.append(norm(k))
        vtb_ref[0, 0, :, t] = z[:, OFF_BV:OFF_GATES].T.astype(BF16)
        kn_ref[0, :, t] = jnp.concatenate(k_norms, axis=0)
        qn_ref[0, :, t] = jnp.concatenate(q_norms, axis=0)


def _plain_query(qt):
    qf = qt.astype(F32)
    rows = lax.broadcasted_iota(jnp.int32, qf.shape, 0)
    return jnp.where(rows < HEAD_DIM, qf, 0.0).astype(BF16)


def _shifted_query(qt, kmax):
    qf = qt.astype(F32)
    rows = lax.broadcasted_iota(jnp.int32, qf.shape, 0)
    shifted = jnp.where(rows == HEAD_DIM, -(qf * kmax), jnp.where(rows < HEAD_DIM, qf, 0.0))
    return shifted.astype(BF16)


def _flash_fast(q_list, k_of, vt_ref, v_rows, l_ref, acc_ref, sb, skew):
    n = len(q_list)
    vb = vt_ref.shape[3]
    pb = min(sb, vb)
    l_ref[...] = jnp.zeros(l_ref.shape, F32)
    acc_ref[...] = jnp.zeros(acc_ref.shape, F32)

    def body(i, carry):
        off = pl.multiple_of(i * FAST_KC, FAST_KC)
        l = [l_ref[j] for j in range(n)]
        acc = [acc_ref[j] for j in range(n)]
        items = [(g, j) for g in range(FAST_KC // sb) for j in range(n)]
        scores = {}

        def consume(g, j):
            p = jnp.exp2(scores.pop((g, j)))
            l[j] = l[j] + jnp.sum(p, axis=0, keepdims=True)
            p = p.astype(BF16)
            for u in range(sb // pb):
                k0 = g * sb + u * pb
                vblk = vt_ref[0, i * (FAST_KC // vb) + k0 // vb, v_rows(j), k0 % vb:k0 % vb + pb]
                acc[j] = acc[j] + jnp.dot(vblk, p[u * pb:(u + 1) * pb], preferred_element_type=F32)

        for t, (g, j) in enumerate(items):
            scores[(g, j)] = jnp.dot(k_of(j, off + g * sb, sb), q_list[j],
                                     preferred_element_type=F32)
            if t >= skew:
                consume(*items[t - skew])
        for g, j in items[max(len(items) - skew, 0):]:
            consume(g, j)
        for j in range(n):
            l_ref[j], acc_ref[j] = l[j], acc[j]
        return carry

    lax.fori_loop(0, vt_ref.shape[1] * vb // FAST_KC, body, 0)


def _flash_streams(q_list, k_of, vt_ref, v_rows, s_buf, mx_buf, m_ref, l_ref, acc_ref):
    n = len(q_list)
    n_chunks, kc = vt_ref.shape[1], vt_ref.shape[3]
    m_ref[...] = jnp.full(m_ref.shape, -jnp.inf, F32)
    l_ref[...] = jnp.zeros(l_ref.shape, F32)
    acc_ref[...] = jnp.zeros(acc_ref.shape, F32)

    def scores(c, slot):
        off = c * kc if isinstance(c, int) else pl.multiple_of(c * kc, kc)
        for j in range(n):
            s = jnp.dot(k_of(j, off, kc), q_list[j], preferred_element_type=F32)
            s_buf[slot, j] = s
            mx_buf[slot, j] = jnp.max(s, axis=0, keepdims=True)

    def softmax_pv(c, slot):
        for j in range(n):
            vblk = vt_ref[0, c, v_rows(j)]
            m_old = m_ref[j]
            m_new = jnp.maximum(m_old, mx_buf[slot, j])
            alpha = jnp.exp2(m_old - m_new)
            p = jnp.exp2(s_buf[slot, j] - m_new)
            l_ref[j] = alpha * l_ref[j] + jnp.sum(p, axis=0, keepdims=True)
            acc_ref[j] = alpha * acc_ref[j] + jnp.dot(vblk, p.astype(BF16),
                                                      preferred_element_type=F32)
            m_ref[j] = m_new

    n_slots = s_buf.shape[0]

    def stage(c, u):
        if not isinstance(c, int) or c + SCORE_LAG < n_chunks:
            scores(c + SCORE_LAG, (u + SCORE_LAG) % n_slots)
        softmax_pv(c, u)

    for c in range(SCORE_LAG):
        scores(c, c % n_slots)
    n_loop = (n_chunks - SCORE_LAG) // n_slots

    def body(i, carry):
        for u in range(n_slots):
            stage(i * n_slots + u, u)
        return carry

    lax.fori_loop(0, n_loop, body, 0)
    for c in range(n_loop * n_slots, n_chunks):
        stage(c, c % n_slots)


def _finish_a(o_ref, l_ref, acc_ref):
    outs = [acc_ref[j] * (1.0 / l_ref[j]) for j in range(A_STREAMS)]
    o_ref[0] = jnp.concatenate(outs, axis=0).T.astype(BF16)


def _finish_b(lambda_init, lam_refs, sub_ref, o_ref, l_ref, acc_ref):
    lq1_ref, lk1_ref, lq2_ref, lk2_ref = lam_refs
    lam = (jnp.exp(jnp.sum(lq1_ref[...] * lk1_ref[...], axis=-1, keepdims=True))
           - jnp.exp(jnp.sum(lq2_ref[...] * lk2_ref[...], axis=-1, keepdims=True))
           + lambda_init)
    outs = []
    for hd in range(B_STEP_HEADS):
        j1, j2 = 2 * hd, 2 * hd + 1
        o = acc_ref[j1] * (1.0 / l_ref[j1]) - lam * (acc_ref[j2] * (1.0 / l_ref[j2]))
        outs.append(_rms(o, 0) * sub_ref[...] * (1.0 - lambda_init))
    o_ref[0] = jnp.concatenate(outs, axis=0).T.astype(BF16)


def _all_rows(j):
    return slice(None)


def _b_rows(j):
    return slice((j // 2) * B_V_DIM, (j // 2 + 1) * B_V_DIM)


def _k_of_a(k_ref):
    return lambda j, off, size: k_ref[0, 0, pl.ds(off, size), :]


def _k_of_b(k_ref):
    return lambda j, off, size: k_ref[0, j // 2, j % 2, pl.ds(off, size), :]


def _attn_a_general(qt_ref, k_ref, vt_ref, o_ref, s_buf, mx_buf, m_ref, l_ref, acc_ref):
    q_list = [_plain_query(qt_ref[0, 0, j]) for j in range(A_STREAMS)]
    _flash_streams(q_list, _k_of_a(k_ref), vt_ref, _all_rows, s_buf, mx_buf, m_ref, l_ref, acc_ref)
    _finish_a(o_ref, l_ref, acc_ref)


def _attn_a_fast(qt_ref, k_ref, vt_ref, kmax_ref, o_ref, l_ref, acc_ref):
    q_list = [_shifted_query(qt_ref[0, 0, j], kmax_ref[0, 0]) for j in range(A_STREAMS)]
    _flash_fast(q_list, _k_of_a(k_ref), vt_ref, _all_rows, l_ref, acc_ref, *FAST_ORDER_A)
    _finish_a(o_ref, l_ref, acc_ref)


def _attn_b_general(lambda_init, qt_ref, k_ref, vt_ref, lq1_ref, lk1_ref, lq2_ref, lk2_ref,
                    sub_ref, o_ref, s_buf, mx_buf, m_ref, l_ref, acc_ref):
    q_list = [_plain_query(qt_ref[0, 0, hd, c]) for hd in range(B_STEP_HEADS) for c in range(2)]
    _flash_streams(q_list, _k_of_b(k_ref), vt_ref, _b_rows, s_buf, mx_buf, m_ref, l_ref, acc_ref)
    _finish_b(lambda_init, (lq1_ref, lk1_ref, lq2_ref, lk2_ref), sub_ref, o_ref, l_ref, acc_ref)


def _attn_b_fast(lambda_init, qt_ref, k_ref, vt_ref, kmax_ref, lq1_ref, lk1_ref, lq2_ref, lk2_ref,
                 sub_ref, o_ref, l_ref, acc_ref):
    q_list = [_shifted_query(qt_ref[0, 0, hd, c], kmax_ref[0, hd, c])
              for hd in range(B_STEP_HEADS) for c in range(2)]
    _flash_fast(q_list, _k_of_b(k_ref), vt_ref, _b_rows, l_ref, acc_ref, *FAST_ORDER_B)
    _finish_b(lambda_init, (lq1_ref, lk1_ref, lq2_ref, lk2_ref), sub_ref, o_ref, l_ref, acc_ref)


def _row_parts(ref, n_parts):
    rows = ref.shape[0] // n_parts
    return [slice(i * rows, (i + 1) * rows) for i in range(n_parts)]


def _merge_kernel(x_ref, oa_ref, ob_ref, nw_ref, win_ref, wa_ref, wb_ref, wo_ref,
                  fg_ref, fu_ref, fd_ref, x1_ref, fg_out, fu_out, fd_out, wg_s, wa_s, wb_s, wo_s):
    @pl.when(pl.program_id(0) == 0)
    def _():
        wg_s[...] = win_ref[:, OFF_GATES:].astype(BF16)
        wa_s[...] = wa_ref[...].astype(BF16)
        wb_s[...] = wb_ref[...].astype(BF16)
        wo_s[...] = wo_ref[...].astype(BF16)

    for src, dst in ((fg_ref, fg_out), (fu_ref, fu_out), (fd_ref, fd_out)):
        dst[...] = src[...].astype(BF16)

    d = x_ref.shape[-1]
    parts = _row_parts(x_ref, DENSE_PARTS)
    stage1 = []
    for r in parts:
        x = x_ref[r]
        h = (_rms(x, -1) * nw_ref[...]).astype(BF16)
        gates = jnp.dot(h, wg_s[...], preferred_element_type=F32)
        ya = jnp.dot(oa_ref[r], wa_s[...], preferred_element_type=F32)
        yb = jnp.dot(ob_ref[r], wb_s[...], preferred_element_type=F32)
        stage1.append((x, gates, ya, yb))
    for r, (x, gates, ya, yb) in zip(parts, stage1):
        y = jax.nn.sigmoid(gates[:, :d]) * ya + jax.nn.sigmoid(gates[:, d:]) * yb
        x1_ref[r] = x + jnp.dot(y.astype(BF16), wo_s[...], preferred_element_type=F32)


def _ffn_kernel(final, x_ref, nw_ref, wg_ref, wu_ref, wd_ref, nf_ref, o_ref):
    def stage1(r):
        x = x_ref[r]
        h = (_rms(x, -1) * nw_ref[...]).astype(BF16)
        gate = jnp.dot(h, wg_ref[...], preferred_element_type=F32)
        up = jnp.dot(h, wu_ref[...], preferred_element_type=F32)
        return r, x, gate, up

    def stage2(r, x, gate, up):
        act = (jax.nn.silu(gate) * up).astype(BF16)
        x2 = x + jnp.dot(act, wd_ref[...], preferred_element_type=F32)
        if final:
            x2 = _rms(x2, -1) * nf_ref[...]
        o_ref[r] = x2

    pending = None
    for r in _row_parts(x_ref, FFN_PARTS):
        nxt = stage1(r)
        if pending is not None:
            stage2(*pending)
        pending = nxt
    stage2(*pending)


def _const_spec(shape):
    nd = len(shape)
    return pl.BlockSpec(shape, lambda *_: (0,) * nd, pipeline_mode=pl.Buffered(1))


def _chunk_spec(shape, n_steps):
    n_rows = shape[0]
    n_chunks = n_steps
    while n_rows % (n_chunks * 2 * SUBLANES):
        assert n_chunks % 2 == 0, shape
        n_chunks //= 2
    repeat = n_steps // n_chunks
    return pl.BlockSpec((n_rows // n_chunks, shape[1]), lambda i: (i // repeat, 0))


def _rope_angles_t(pos, dim, theta):
    inv_freq = theta ** (-jnp.arange(0, dim, 2, dtype=F32) / dim)
    return (pos[:, None] * inv_freq[None, :]).T


def _angle_tables(seq_len):
    rows = seq_len // GRID_W
    row = jnp.broadcast_to(jnp.arange(rows, dtype=F32)[:, None], (rows, GRID_W)).reshape(-1)
    col = jnp.broadcast_to(jnp.arange(GRID_W, dtype=F32)[None, :], (rows, GRID_W)).reshape(-1)
    ang_a = jnp.concatenate([_rope_angles_t(row, HALF, AXIAL_THETA),
                             _rope_angles_t(col, HALF, AXIAL_THETA)], axis=0)
    ang_b = _rope_angles_t(jnp.arange(seq_len, dtype=F32), HEAD_DIM, ROPE_THETA)
    return jnp.cos(ang_a), jnp.sin(ang_a), jnp.cos(ang_b), jnp.sin(ang_b)


def _params(*sem):
    return pltpu.CompilerParams(dimension_semantics=sem, vmem_limit_bytes=VMEM_LIMIT)


def _layer(x, lambda_init, final, norm_mix, w_in, q_norm_a, k_norm_a, lq1, lk1, lq2, lk2, subln,
           w_proj_a, w_proj_b, w_out, norm_ffn, w_gate, w_up, w_down, norm_final, tables):
    b, s, d = x.shape
    d_ff = w_gate.shape[-1]
    n_tok = b * s
    n_chunks = s // PREP_TM
    ca, sa, cb, sb = tables
    row = lambda v: v.reshape(1, -1).astype(F32)
    col = lambda v: v.reshape(-1, 1).astype(F32)

    tab_spec = pl.BlockSpec((HALF, PREP_TM), lambda bi, ti: (0, ti))
    qta, ka, vta, qtb, kb, vtb, kn, qn = pl.pallas_call(
        _prep_kernel,
        grid=(b, n_chunks),
        in_specs=[
            pl.BlockSpec((1, PREP_TM, d), lambda bi, ti: (bi, ti, 0)),
            _const_spec((1, d)),
            _const_spec((d, OFF_GATES)),
            _const_spec((HEAD_DIM, 1)),
            _const_spec((HEAD_DIM, 1)),
            tab_spec, tab_spec, tab_spec, tab_spec,
        ],
        out_specs=[
            pl.BlockSpec((1, 1, A_Q_HEADS, SLAB, PREP_TM), lambda bi, ti: (bi, ti, 0, 0, 0)),
            pl.BlockSpec((1, A_KV_HEADS, PREP_TM, SLAB), lambda bi, ti: (bi, 0, ti, 0)),
            pl.BlockSpec((1, 1, A_KV_HEADS * HEAD_DIM, PREP_TM), lambda bi, ti: (bi, ti, 0, 0)),
            pl.BlockSpec((1, 1, B_HEADS, 2, SLAB, PREP_TM), lambda bi, ti: (bi, ti, 0, 0, 0, 0)),
            pl.BlockSpec((1, B_HEADS, 2, PREP_TM, SLAB), lambda bi, ti: (bi, 0, 0, ti, 0)),
            pl.BlockSpec((1, 1, B_WIDTH, PREP_TM), lambda bi, ti: (bi, ti, 0, 0)),
            pl.BlockSpec((1, NORM_ROWS, PREP_TM), lambda bi, ti: (bi, 0, ti)),
            pl.BlockSpec((1, NORM_ROWS, PREP_TM), lambda bi, ti: (bi, 0, ti)),
        ],
        out_shape=[
            jax.ShapeDtypeStruct((b, n_chunks, A_Q_HEADS, SLAB, PREP_TM), BF16),
            jax.ShapeDtypeStruct((b, A_KV_HEADS, s, SLAB), BF16),
            jax.ShapeDtypeStruct((b, n_chunks, A_KV_HEADS * HEAD_DIM, PREP_TM), BF16),
            jax.ShapeDtypeStruct((b, n_chunks, B_HEADS, 2, SLAB, PREP_TM), BF16),
            jax.ShapeDtypeStruct((b, B_HEADS, 2, s, SLAB), BF16),
            jax.ShapeDtypeStruct((b, n_chunks, B_WIDTH, PREP_TM), BF16),
            jax.ShapeDtypeStruct((b, NORM_ROWS, s), F32),
            jax.ShapeDtypeStruct((b, NORM_ROWS, s), F32),
        ],
        scratch_shapes=[pltpu.VMEM((d, OFF_GATES), BF16)],
        compiler_params=_params("arbitrary", "arbitrary"),
        name="prep",
    )(x, row(norm_mix), w_in, col(q_norm_a), col(k_norm_a), ca, sa, cb, sb)

    nq = s // ATTN_TQ
    steps_per_group = A_GROUP // A_STREAMS
    kmax_a = jnp.max(kn[:, :A_KV_HEADS], axis=-1)
    kmax_b = jnp.max(kn[:, SUBLANES:], axis=-1).reshape(b, B_HEADS, 2)
    qmax_a = jnp.max(qn[:, :A_Q_HEADS], axis=-1)
    qmax_b = jnp.max(qn[:, A_Q_HEADS:], axis=-1).reshape(b, B_HEADS, 2)
    fast_a = jnp.max(qmax_a * jnp.repeat(kmax_a, A_GROUP, axis=1)) < FAST_BOUND
    fast_b = jnp.max(qmax_b * kmax_b) < FAST_BOUND

    def general_scratch(n, dv):
        return [
            pltpu.VMEM((SCORE_SLOTS, n, PREP_TM, ATTN_TQ), F32),
            pltpu.VMEM((SCORE_SLOTS, n, 1, ATTN_TQ), F32),
            pltpu.VMEM((n, 1, ATTN_TQ), F32),
            pltpu.VMEM((n, 1, ATTN_TQ), F32),
            pltpu.VMEM((n, dv, ATTN_TQ), F32),
        ]

    def fast_scratch(n, dv):
        return [pltpu.VMEM((n, 1, ATTN_TQ), F32), pltpu.VMEM((n, dv, ATTN_TQ), F32)]

    a_specs = [
        pl.BlockSpec((1, 1, A_STREAMS, SLAB, ATTN_TQ), lambda bi, hp, qi: (bi, qi, hp, 0, 0)),
        pl.BlockSpec((1, 1, s, SLAB), lambda bi, hp, qi: (bi, hp // steps_per_group, 0, 0)),
        pl.BlockSpec((1, n_chunks, HEAD_DIM, PREP_TM),
                     lambda bi, hp, qi: (bi, 0, hp // steps_per_group, 0)),
    ]
    a_kmax_spec = pl.BlockSpec((1, 1, 1, ATTN_TQ), lambda bi, hp, qi: (bi, hp // steps_per_group, 0, 0))
    a_common = dict(
        grid=(b, A_Q_HEADS // A_STREAMS, nq),
        out_specs=pl.BlockSpec((1, ATTN_TQ, A_STREAMS * HEAD_DIM), lambda bi, hp, qi: (bi, qi, hp)),
        out_shape=jax.ShapeDtypeStruct((b, s, A_WIDTH), BF16),
        compiler_params=_params("parallel", "parallel", "arbitrary"),
    )

    kmax_a_rows = jnp.broadcast_to(kmax_a[:, :, None, None], (b, A_KV_HEADS, 1, ATTN_TQ))
    kmax_b_rows = jnp.broadcast_to(kmax_b[:, :, :, None, None], (b, B_HEADS, 2, 1, ATTN_TQ))

    def attn_a_fast():
        return pl.pallas_call(_attn_a_fast, in_specs=a_specs + [a_kmax_spec],
                              scratch_shapes=fast_scratch(A_STREAMS, HEAD_DIM), name="attn_a_fast",
                              **a_common)(qta, ka, vta, kmax_a_rows)

    def attn_a_general():
        return pl.pallas_call(_attn_a_general, in_specs=a_specs,
                              scratch_shapes=general_scratch(A_STREAMS, HEAD_DIM), name="attn_a",
                              **a_common)(qta, ka, vta)

    oa = lax.cond(fast_a, attn_a_fast, attn_a_general)

    lam_spec = _const_spec((1, HEAD_DIM))
    hb = B_STEP_HEADS
    b_specs = [
        pl.BlockSpec((1, 1, hb, 2, SLAB, ATTN_TQ), lambda bi, hp, qi: (bi, qi, hp, 0, 0, 0)),
        pl.BlockSpec((1, hb, 2, s, SLAB), lambda bi, hp, qi: (bi, hp, 0, 0, 0)),
        pl.BlockSpec((1, n_chunks, hb * B_V_DIM, PREP_TM), lambda bi, hp, qi: (bi, 0, hp, 0)),
    ]
    b_kmax_spec = pl.BlockSpec((1, hb, 2, 1, ATTN_TQ), lambda bi, hp, qi: (bi, hp, 0, 0, 0))
    b_tail_specs = [lam_spec, lam_spec, lam_spec, lam_spec, _const_spec((B_V_DIM, 1))]
    b_tail = (row(lq1), row(lk1), row(lq2), row(lk2), col(subln))
    b_common = dict(
        grid=(b, B_HEADS // hb, nq),
        out_specs=pl.BlockSpec((1, ATTN_TQ, hb * B_V_DIM), lambda bi, hp, qi: (bi, qi, hp)),
        out_shape=jax.ShapeDtypeStruct((b, s, B_WIDTH), BF16),
        compiler_params=_params("parallel", "parallel", "arbitrary"),
    )

    def attn_b_fast():
        return pl.pallas_call(functools.partial(_attn_b_fast, lambda_init),
                              in_specs=b_specs + [b_kmax_spec] + b_tail_specs,
                              scratch_shapes=fast_scratch(2 * hb, B_V_DIM), name="attn_b_fast",
                              **b_common)(qtb, kb, vtb, kmax_b_rows, *b_tail)

    def attn_b_general():
        return pl.pallas_call(functools.partial(_attn_b_general, lambda_init),
                              in_specs=b_specs + b_tail_specs,
                              scratch_shapes=general_scratch(2 * hb, B_V_DIM), name="attn_b",
                              **b_common)(qtb, kb, vtb, *b_tail)

    ob = lax.cond(fast_b, attn_b_fast, attn_b_general)

    tok_spec = lambda width: pl.BlockSpec((DENSE_TM, width), lambda ti: (ti, 0))
    x_flat = x.reshape(n_tok, d)
    merge_steps = n_tok // DENSE_TM
    ffn_shapes = [w_gate.shape, w_up.shape, w_down.shape]
    ffn_specs = [_chunk_spec(sh, merge_steps) for sh in ffn_shapes]
    x1, *ffn_w = pl.pallas_call(
        _merge_kernel,
        grid=(merge_steps,),
        in_specs=[
            tok_spec(d), tok_spec(A_WIDTH), tok_spec(B_WIDTH),
            _const_spec((1, d)),
            _const_spec(w_in.shape),
            _const_spec((A_WIDTH, d)),
            _const_spec((B_WIDTH, d)),
            _const_spec((d, d)),
        ] + ffn_specs,
        out_specs=[tok_spec(d)] + ffn_specs,
        out_shape=[jax.ShapeDtypeStruct((n_tok, d), F32)]
        + [jax.ShapeDtypeStruct(sh, BF16) for sh in ffn_shapes],
        scratch_shapes=[pltpu.VMEM((d, 2 * d), BF16), pltpu.VMEM((A_WIDTH, d), BF16),
                        pltpu.VMEM((B_WIDTH, d), BF16), pltpu.VMEM((d, d), BF16)],
        compiler_params=_params("arbitrary"),
        name="merge",
    )(x_flat, oa.reshape(n_tok, A_WIDTH), ob.reshape(n_tok, B_WIDTH), row(norm_mix),
      w_in, w_proj_a, w_proj_b, w_out, w_gate, w_up, w_down)

    x2 = pl.pallas_call(
        functools.partial(_ffn_kernel, final),
        grid=(n_tok // FFN_TM,),
        in_specs=[
            pl.BlockSpec((FFN_TM, d), lambda ti: (ti, 0)),
            _const_spec((1, d)),
            _const_spec((d, d_ff)),
            _const_spec((d, d_ff)),
            _const_spec((d_ff, d)),
            _const_spec((1, d)),
        ],
        out_specs=pl.BlockSpec((FFN_TM, d), lambda ti: (ti, 0)),
        out_shape=jax.ShapeDtypeStruct((n_tok, d), F32),
        compiler_params=_params("parallel"),
        name="ffn",
    )(x1, row(norm_ffn), *ffn_w, row(norm_final))
    return x2.reshape(b, s, d)


def kernel(x, norm_mix, w_in, q_norm_a, k_norm_a, lambda_q1, lambda_k1, lambda_q2, lambda_k2,
           subln_b, w_proj_a, w_proj_b, w_out, norm_ffn, w_gate_ffn, w_up_ffn, w_down_ffn,
           norm_final):
    depth = norm_mix.shape[0]
    tables = _angle_tables(x.shape[1])
    for l in range(depth):
        lambda_init = 0.8 - 0.6 * math.exp(-0.3 * l)
        x = _layer(x, lambda_init, l == depth - 1, norm_mix[l], w_in[l], q_norm_a[l], k_norm_a[l],
                   lambda_q1[l], lambda_k1[l], lambda_q2[l], lambda_k2[l], subln_b[l],
                   w_proj_a[l], w_proj_b[l], w_out[l], norm_ffn[l], w_gate_ffn[l], w_up_ffn[l],
                   w_down_ffn[l], norm_final, tables)
    return x
```

```python
import functools
import math

import jax
import jax.numpy as jnp
from jax import lax
from jax.experimental import pallas as pl
from jax.experimental.pallas import tpu as pltpu

F32 = jnp.float32
BF16 = jnp.bfloat16

GRID_W = 64
HEAD_DIM = 64
HALF = HEAD_DIM // 2
A_Q_HEADS = 8
A_KV_HEADS = 2
A_GROUP = A_Q_HEADS // A_KV_HEADS
A_WIDTH = A_Q_HEADS * HEAD_DIM
B_HEADS = 4
B_V_DIM = 2 * HEAD_DIM
B_WIDTH = B_HEADS * B_V_DIM
ROPE_THETA = 10000.0
AXIAL_THETA = 10000.0
NORM_EPS = 1e-6
QK_SCALE = math.log2(math.e) / math.sqrt(HEAD_DIM)
A_STREAMS = 4
B_STEP_HEADS = 2
SCORE_LAG = 2
SCORE_SLOTS = 3

OFF_AQ = 0
OFF_AK = OFF_AQ + A_WIDTH
OFF_AV = OFF_AK + A_KV_HEADS * HEAD_DIM
OFF_BQ = OFF_AV + A_KV_HEADS * HEAD_DIM
OFF_BK = OFF_BQ + B_HEADS * 2 * HEAD_DIM
OFF_BV = OFF_BK + B_HEADS * 2 * HEAD_DIM
OFF_GATES = OFF_BV + B_WIDTH
SLAB = 2 * HEAD_DIM
SUBLANES = 8
NORM_ROWS = 2 * SUBLANES
FAST_BOUND = 40.0

PREP_TM = 512
PREP_PARTS = 2
ATTN_TQ = PREP_TM
FAST_KC = 8192
FAST_ORDER_A = (256, 4)
FAST_ORDER_B = (4096, 2)
DENSE_TM = 512
DENSE_PARTS = 2
FFN_TM = 1024
FFN_PARTS = 4
VMEM_LIMIT = 56 * 1024 * 1024


def _rms(x, axis):
    return x * lax.rsqrt(jnp.mean(x * x, axis=axis, keepdims=True) + NORM_EPS)


def _rope_t(xt, cos, sin):
    x1, x2 = xt[:HALF], xt[HALF:]
    return jnp.concatenate([x1 * cos - x2 * sin, x2 * cos + x1 * sin], axis=0)


def _prep_kernel(x_ref, nw_ref, w_ref, gq_ref, gk_ref, ca_ref, sa_ref, cb_ref, sb_ref,
                 win_rows_ref, pa_ref, pb_ref, po_ref,
                 qta_ref, ka_ref, vta_ref, qtb_ref, kb_ref, vtb_ref, kn_ref, qn_ref,
                 wg_out, pa_out, pb_out, po_out, wb_ref):
    @pl.when((pl.program_id(0) == 0) & (pl.program_id(1) == 0))
    def _():
        wb_ref[...] = w_ref[...].astype(BF16)

    wg_out[...] = win_rows_ref[:, OFF_GATES:].astype(BF16)
    for src, dst in ((pa_ref, pa_out), (pb_ref, pb_out), (po_ref, po_out)):
        dst[...] = src[...].astype(BF16)

    tm = x_ref.shape[1] // PREP_PARTS
    gq, gk = gq_ref[...], gk_ref[...]
    pad = jnp.zeros((SLAB - HEAD_DIM - SUBLANES, tm), F32)
    one_row = (lax.broadcasted_iota(jnp.int32, (SUBLANES, tm), 0) == 0).astype(F32)

    def norm(v):
        return jnp.sqrt(jnp.sum(v * v, axis=0, keepdims=True))

    def k_slab(k):
        return jnp.concatenate([k, one_row, pad], axis=0).T.astype(BF16)

    groups = [slice(i * tm, (i + 1) * tm) for i in range(PREP_PARTS)]
    zs = []
    for t in groups:
        h = _rms(x_ref[0, t], -1) * nw_ref[...]
        zs.append(jnp.dot(h.astype(BF16), wb_ref[...], preferred_element_type=F32))

    for t, z in zip(groups, zs):
        ca, sa, cb, sb = ca_ref[:, t], sa_ref[:, t], cb_ref[:, t], sb_ref[:, t]
        q_norms, k_norms = [], []

        def q_block(q):
            nq = norm(q).astype(BF16).astype(F32)
            q_norms.append(nq)
            return jnp.concatenate([q, jnp.broadcast_to(nq, (SUBLANES, tm)), pad], axis=0).astype(BF16)

        aqt = z[:, OFF_AQ:OFF_AK].T
        for hd in range(A_Q_HEADS):
            q = aqt[hd * HEAD_DIM:(hd + 1) * HEAD_DIM]
            qta_ref[0, 0, hd, :, t] = q_block(_rope_t(_rms(q, 0) * gq, ca, sa) * QK_SCALE)
        akt = z[:, OFF_AK:OFF_AV].T
        for g in range(A_KV_HEADS):
            k = _rope_t(_rms(akt[g * HEAD_DIM:(g + 1) * HEAD_DIM], 0) * gk, ca, sa)
            ka_ref[0, g, t, :] = k_slab(k)
            k_norms.append(norm(k))
        k_norms.append(jnp.zeros((SUBLANES - A_KV_HEADS, tm), F32))
        vta_ref[0, 0, :, t] = z[:, OFF_AV:OFF_BQ].T.astype(BF16)

        bqt = z[:, OFF_BQ:OFF_BK].T
        bkt = z[:, OFF_BK:OFF_BV].T
        for hd in range(B_HEADS):
            for c in range(2):
                r0 = (hd * 2 + c) * HEAD_DIM
                qtb_ref[0, 0, hd, c, :, t] = q_block(_rope_t(bqt[r0:r0 + HEAD_DIM], cb, sb) * QK_SCALE)
                k = _rope_t(bkt[r0:r0 + HEAD_DIM], cb, sb)
                kb_ref[0, hd, c, t, :] = k_slab(k)
                k_norms.append(norm(k))
        vtb_ref[0, 0, :, t] = z[:, OFF_BV:OFF_GATES].T.astype(BF16)
        kn_ref[0, :, t] = jnp.concatenate(k_norms, axis=0)
        qn_ref[0, :, t] = jnp.concatenate(q_norms, axis=0)


def _plain_query(qt):
    qf = qt.astype(F32)
    rows = lax.broadcasted_iota(jnp.int32, qf.shape, 0)
    return jnp.where(rows < HEAD_DIM, qf, 0.0).astype(BF16)


def _shifted_query(qt, kmax):
    qf = qt.astype(F32)
    rows = lax.broadcasted_iota(jnp.int32, qf.shape, 0)
    shifted = jnp.where(rows == HEAD_DIM, -(qf * kmax), jnp.where(rows < HEAD_DIM, qf, 0.0))
    return shifted.astype(BF16)


def _flash_fast(q_list, k_of, vt_ref, v_rows, l_ref, acc_ref, sb, skew):
    n = len(q_list)
    vb = vt_ref.shape[3]
    pb = min(sb, vb)
    l_ref[...] = jnp.zeros(l_ref.shape, F32)
    acc_ref[...] = jnp.zeros(acc_ref.shape, F32)

    def body(i, carry):
        off = pl.multiple_of(i * FAST_KC, FAST_KC)
        l = [l_ref[j] for j in range(n)]
        acc = [acc_ref[j] for j in range(n)]
        items = [(g, j) for g in range(FAST_KC // sb) for j in range(n)]
        scores = {}

        def consume(g, j):
            p = jnp.exp2(scores.pop((g, j)))
            l[j] = l[j] + jnp.sum(p, axis=0, keepdims=True)
            p = p.astype(BF16)
            for u in range(sb // pb):
                k0 = g * sb + u * pb
                vblk = vt_ref[0, i * (FAST_KC // vb) + k0 // vb, v_rows(j), k0 % vb:k0 % vb + pb]
                acc[j] = acc[j] + jnp.dot(vblk, p[u * pb:(u + 1) * pb], preferred_element_type=F32)

        for t, (g, j) in enumerate(items):
            scores[(g, j)] = jnp.dot(k_of(j, off + g * sb, sb), q_list[j],
                                     preferred_element_type=F32)
            if t >= skew:
                consume(*items[t - skew])
        for g, j in items[max(len(items) - skew, 0):]:
            consume(g, j)
        for j in range(n):
            l_ref[j], acc_ref[j] = l[j], acc[j]
        return carry

    lax.fori_loop(0, vt_ref.shape[1] * vb // FAST_KC, body, 0)


def _flash_streams(q_list, k_of, vt_ref, v_rows, s_buf, mx_buf, m_ref, l_ref, acc_ref):
    n = len(q_list)
    n_chunks, kc = vt_ref.shape[1], vt_ref.shape[3]
    m_ref[...] = jnp.full(m_ref.shape, -jnp.inf, F32)
    l_ref[...] = jnp.zeros(l_ref.shape, F32)
    acc_ref[...] = jnp.zeros(acc_ref.shape, F32)

    def scores(c, slot):
        off = c * kc if isinstance(c, int) else pl.multiple_of(c * kc, kc)
        for j in range(n):
            s = jnp.dot(k_of(j, off, kc), q_list[j], preferred_element_type=F32)
            s_buf[slot, j] = s
            mx_buf[slot, j] = jnp.max(s, axis=0, keepdims=True)

    def softmax_pv(c, slot):
        for j in range(n):
            vblk = vt_ref[0, c, v_rows(j)]
            m_old = m_ref[j]
            m_new = jnp.maximum(m_old, mx_buf[slot, j])
            alpha = jnp.exp2(m_old - m_new)
            p = jnp.exp2(s_buf[slot, j] - m_new)
            l_ref[j] = alpha * l_ref[j] + jnp.sum(p, axis=0, keepdims=True)
            acc_ref[j] = alpha * acc_ref[j] + jnp.dot(vblk, p.astype(BF16),
                                                      preferred_element_type=F32)
            m_ref[j] = m_new

    n_slots = s_buf.shape[0]

    def stage(c, u):
        if not isinstance(c, int) or c + SCORE_LAG < n_chunks:
            scores(c + SCORE_LAG, (u + SCORE_LAG) % n_slots)
        softmax_pv(c, u)

    for c in range(SCORE_LAG):
        scores(c, c % n_slots)
    n_loop = (n_chunks - SCORE_LAG) // n_slots

    def body(i, carry):
        for u in range(n_slots):
            stage(i * n_slots + u, u)
        return carry

    lax.fori_loop(0, n_loop, body, 0)
    for c in range(n_loop * n_slots, n_chunks):
        stage(c, c % n_slots)


def _finish_a(o_ref, l_ref, acc_ref):
    outs = [acc_ref[j] * (1.0 / l_ref[j]) for j in range(A_STREAMS)]
    o_ref[0] = jnp.concatenate(outs, axis=0).T.astype(BF16)


def _finish_b(lambda_init, lam_refs, sub_ref, o_ref, l_ref, acc_ref):
    lq1_ref, lk1_ref, lq2_ref, lk2_ref = lam_refs
    lam = (jnp.exp(jnp.sum(lq1_ref[...] * lk1_ref[...], axis=-1, keepdims=True))
           - jnp.exp(jnp.sum(lq2_ref[...] * lk2_ref[...], axis=-1, keepdims=True))
           + lambda_init)
    outs = []
    for hd in range(B_STEP_HEADS):
        j1, j2 = 2 * hd, 2 * hd + 1
        o = acc_ref[j1] * (1.0 / l_ref[j1]) - lam * (acc_ref[j2] * (1.0 / l_ref[j2]))
        outs.append(_rms(o, 0) * sub_ref[...] * (1.0 - lambda_init))
    o_ref[0] = jnp.concatenate(outs, axis=0).T.astype(BF16)


def _all_rows(j):
    return slice(None)


def _b_rows(j):
    return slice((j // 2) * B_V_DIM, (j // 2 + 1) * B_V_DIM)


def _k_of_a(k_ref):
    return lambda j, off, size: k_ref[0, 0, pl.ds(off, size), :]


def _k_of_b(k_ref):
    return lambda j, off, size: k_ref[0, j // 2, j % 2, pl.ds(off, size), :]


def _attn_a_general(qt_ref, k_ref, vt_ref, o_ref, s_buf, mx_buf, m_ref, l_ref, acc_ref):
    q_list = [_plain_query(qt_ref[0, 0, j]) for j in range(A_STREAMS)]
    _flash_streams(q_list, _k_of_a(k_ref), vt_ref, _all_rows, s_buf, mx_buf, m_ref, l_ref, acc_ref)
    _finish_a(o_ref, l_ref, acc_ref)


def _attn_a_fast(qt_ref, k_ref, vt_ref, kmax_ref, o_ref, l_ref, acc_ref):
    q_list = [_shifted_query(qt_ref[0, 0, j], kmax_ref[0, 0]) for j in range(A_STREAMS)]
    _flash_fast(q_list, _k_of_a(k_ref), vt_ref, _all_rows, l_ref, acc_ref, *FAST_ORDER_A)
    _finish_a(o_ref, l_ref, acc_ref)


def _attn_b_general(lambda_init, qt_ref, k_ref, vt_ref, lq1_ref, lk1_ref, lq2_ref, lk2_ref,
                    sub_ref, o_ref, s_buf, mx_buf, m_ref, l_ref, acc_ref):
    q_list = [_plain_query(qt_ref[0, 0, hd, c]) for hd in range(B_STEP_HEADS) for c in range(2)]
    _flash_streams(q_list, _k_of_b(k_ref), vt_ref, _b_rows, s_buf, mx_buf, m_ref, l_ref, acc_ref)
    _finish_b(lambda_init, (lq1_ref, lk1_ref, lq2_ref, lk2_ref), sub_ref, o_ref, l_ref, acc_ref)


def _attn_b_fast(lambda_init, qt_ref, k_ref, vt_ref, kmax_ref, lq1_ref, lk1_ref, lq2_ref, lk2_ref,
                 sub_ref, o_ref, l_ref, acc_ref):
    q_list = [_shifted_query(qt_ref[0, 0, hd, c], kmax_ref[0, hd, c])
              for hd in range(B_STEP_HEADS) for c in range(2)]
    _flash_fast(q_list, _k_of_b(k_ref), vt_ref, _b_rows, l_ref, acc_ref, *FAST_ORDER_B)
    _finish_b(lambda_init, (lq1_ref, lk1_ref, lq2_ref, lk2_ref), sub_ref, o_ref, l_ref, acc_ref)


def _row_parts(ref, n_parts):
    rows = ref.shape[0] // n_parts
    return [slice(i * rows, (i + 1) * rows) for i in range(n_parts)]


def _merge_kernel(x_ref, oa_ref, ob_ref, nw_ref, wg_s, wa_s, wb_s, wo_s,
                  fg_ref, fu_ref, fd_ref, x1_ref, fg_out, fu_out, fd_out):
    for src, dst in ((fg_ref, fg_out), (fu_ref, fu_out), (fd_ref, fd_out)):
        dst[...] = src[...].astype(BF16)

    d = x_ref.shape[-1]
    parts = _row_parts(x_ref, DENSE_PARTS)
    stage1 = []
    for r in parts:
        x = x_ref[r]
        h = (_rms(x, -1) * nw_ref[...]).astype(BF16)
        gates = jnp.dot(h, wg_s[...], preferred_element_type=F32)
        ya = jnp.dot(oa_ref[r], wa_s[...], preferred_element_type=F32)
        yb = jnp.dot(ob_ref[r], wb_s[...], preferred_element_type=F32)
        stage1.append((x, gates, ya, yb))
    for r, (x, gates, ya, yb) in zip(parts, stage1):
        y = jax.nn.sigmoid(gates[:, :d]) * ya + jax.nn.sigmoid(gates[:, d:]) * yb
        x1_ref[r] = x + jnp.dot(y.astype(BF16), wo_s[...], preferred_element_type=F32)


def _ffn_kernel(final, x_ref, nw_ref, wg_ref, wu_ref, wd_ref, nf_ref, o_ref):
    def stage1(r):
        x = x_ref[r]
        h = (_rms(x, -1) * nw_ref[...]).astype(BF16)
        gate = jnp.dot(h, wg_ref[...], preferred_element_type=F32)
        up = jnp.dot(h, wu_ref[...], preferred_element_type=F32)
        return r, x, gate, up

    def stage2(r, x, gate, up):
        act = (jax.nn.silu(gate) * up).astype(BF16)
        x2 = x + jnp.dot(act, wd_ref[...], preferred_element_type=F32)
        if final:
            x2 = _rms(x2, -1) * nf_ref[...]
        o_ref[r] = x2

    pending = None
    for r in _row_parts(x_ref, FFN_PARTS):
        nxt = stage1(r)
        if pending is not None:
            stage2(*pending)
        pending = nxt
    stage2(*pending)


def _const_spec(shape):
    nd = len(shape)
    return pl.BlockSpec(shape, lambda *_: (0,) * nd, pipeline_mode=pl.Buffered(1))


def _chunk_spec(shape, n_steps, step_of=lambda i: i):
    n_rows = shape[0]
    n_chunks = n_steps
    while n_rows % (n_chunks * 2 * SUBLANES):
        assert n_chunks % 2 == 0, shape
        n_chunks //= 2
    repeat = n_steps // n_chunks
    return pl.BlockSpec((n_rows // n_chunks, shape[1]), lambda *ids: (step_of(*ids) // repeat, 0))


def _rope_angles_t(pos, dim, theta):
    inv_freq = theta ** (-jnp.arange(0, dim, 2, dtype=F32) / dim)
    return (pos[:, None] * inv_freq[None, :]).T


def _angle_tables(seq_len):
    rows = seq_len // GRID_W
    row = jnp.broadcast_to(jnp.arange(rows, dtype=F32)[:, None], (rows, GRID_W)).reshape(-1)
    col = jnp.broadcast_to(jnp.arange(GRID_W, dtype=F32)[None, :], (rows, GRID_W)).reshape(-1)
    ang_a = jnp.concatenate([_rope_angles_t(row, HALF, AXIAL_THETA),
                             _rope_angles_t(col, HALF, AXIAL_THETA)], axis=0)
    ang_b = _rope_angles_t(jnp.arange(seq_len, dtype=F32), HEAD_DIM, ROPE_THETA)
    return jnp.cos(ang_a), jnp.sin(ang_a), jnp.cos(ang_b), jnp.sin(ang_b)


def _params(*sem):
    return pltpu.CompilerParams(dimension_semantics=sem, vmem_limit_bytes=VMEM_LIMIT)


def _layer(x, lambda_init, final, norm_mix, w_in, q_norm_a, k_norm_a, lq1, lk1, lq2, lk2, subln,
           w_proj_a, w_proj_b, w_out, norm_ffn, w_gate, w_up, w_down, norm_final, tables):
    b, s, d = x.shape
    d_ff = w_gate.shape[-1]
    n_tok = b * s
    n_chunks = s // PREP_TM
    ca, sa, cb, sb = tables
    row = lambda v: v.reshape(1, -1).astype(F32)
    col = lambda v: v.reshape(-1, 1).astype(F32)

    tab_spec = pl.BlockSpec((HALF, PREP_TM), lambda bi, ti: (0, ti))
    prep_step = lambda bi, ti: bi * n_chunks + ti
    merge_w_shapes = [(d, 2 * d), w_proj_a.shape, w_proj_b.shape, w_out.shape]
    merge_w_in = [_chunk_spec(sh, b * n_chunks, prep_step)
                  for sh in [w_in.shape] + merge_w_shapes[1:]]
    merge_w_out = [_chunk_spec(sh, b * n_chunks, prep_step) for sh in merge_w_shapes]
    qta, ka, vta, qtb, kb, vtb, kn, qn, *merge_w = pl.pallas_call(
        _prep_kernel,
        grid=(b, n_chunks),
        in_specs=[
            pl.BlockSpec((1, PREP_TM, d), lambda bi, ti: (bi, ti, 0)),
            _const_spec((1, d)),
            _const_spec((d, OFF_GATES)),
            _const_spec((HEAD_DIM, 1)),
            _const_spec((HEAD_DIM, 1)),
            tab_spec, tab_spec, tab_spec, tab_spec,
        ] + merge_w_in,
        out_specs=[
            pl.BlockSpec((1, 1, A_Q_HEADS, SLAB, PREP_TM), lambda bi, ti: (bi, ti, 0, 0, 0)),
            pl.BlockSpec((1, A_KV_HEADS, PREP_TM, SLAB), lambda bi, ti: (bi, 0, ti, 0)),
            pl.BlockSpec((1, 1, A_KV_HEADS * HEAD_DIM, PREP_TM), lambda bi, ti: (bi, ti, 0, 0)),
            pl.BlockSpec((1, 1, B_HEADS, 2, SLAB, PREP_TM), lambda bi, ti: (bi, ti, 0, 0, 0, 0)),
            pl.BlockSpec((1, B_HEADS, 2, PREP_TM, SLAB), lambda bi, ti: (bi, 0, 0, ti, 0)),
            pl.BlockSpec((1, 1, B_WIDTH, PREP_TM), lambda bi, ti: (bi, ti, 0, 0)),
            pl.BlockSpec((1, NORM_ROWS, PREP_TM), lambda bi, ti: (bi, 0, ti)),
            pl.BlockSpec((1, NORM_ROWS, PREP_TM), lambda bi, ti: (bi, 0, ti)),
        ] + merge_w_out,
        out_shape=[
            jax.ShapeDtypeStruct((b, n_chunks, A_Q_HEADS, SLAB, PREP_TM), BF16),
            jax.ShapeDtypeStruct((b, A_KV_HEADS, s, SLAB), BF16),
            jax.ShapeDtypeStruct((b, n_chunks, A_KV_HEADS * HEAD_DIM, PREP_TM), BF16),
            jax.ShapeDtypeStruct((b, n_chunks, B_HEADS, 2, SLAB, PREP_TM), BF16),
            jax.ShapeDtypeStruct((b, B_HEADS, 2, s, SLAB), BF16),
            jax.ShapeDtypeStruct((b, n_chunks, B_WIDTH, PREP_TM), BF16),
            jax.ShapeDtypeStruct((b, NORM_ROWS, s), F32),
            jax.ShapeDtypeStruct((b, NORM_ROWS, s), F32),
        ] + [jax.ShapeDtypeStruct(sh, BF16) for sh in merge_w_shapes],
        scratch_shapes=[pltpu.VMEM((d, OFF_GATES), BF16)],
        compiler_params=_params("arbitrary", "arbitrary"),
        name="prep",
    )(x, row(norm_mix), w_in, col(q_norm_a), col(k_norm_a), ca, sa, cb, sb,
      w_in, w_proj_a, w_proj_b, w_out)

    nq = s // ATTN_TQ
    steps_per_group = A_GROUP // A_STREAMS
    kmax_a = jnp.max(kn[:, :A_KV_HEADS], axis=-1)
    kmax_b = jnp.max(kn[:, SUBLANES:], axis=-1).reshape(b, B_HEADS, 2)
    qmax_a = jnp.max(qn[:, :A_Q_HEADS], axis=-1)
    qmax_b = jnp.max(qn[:, A_Q_HEADS:], axis=-1).reshape(b, B_HEADS, 2)
    fast_a = jnp.max(qmax_a * jnp.repeat(kmax_a, A_GROUP, axis=1)) < FAST_BOUND
    fast_b = jnp.max(qmax_b * kmax_b) < FAST_BOUND

    def general_scratch(n, dv):
        return [
            pltpu.VMEM((SCORE_SLOTS, n, PREP_TM, ATTN_TQ), F32),
            pltpu.VMEM((SCORE_SLOTS, n, 1, ATTN_TQ), F32),
            pltpu.VMEM((n, 1, ATTN_TQ), F32),
            pltpu.VMEM((n, 1, ATTN_TQ), F32),
            pltpu.VMEM((n, dv, ATTN_TQ), F32),
        ]

    def fast_scratch(n, dv):
        return [pltpu.VMEM((n, 1, ATTN_TQ), F32), pltpu.VMEM((n, dv, ATTN_TQ), F32)]

    a_specs = [
        pl.BlockSpec((1, 1, A_STREAMS, SLAB, ATTN_TQ), lambda bi, hp, qi: (bi, qi, hp, 0, 0)),
        pl.BlockSpec((1, 1, s, SLAB), lambda bi, hp, qi: (bi, hp // steps_per_group, 0, 0)),
        pl.BlockSpec((1, n_chunks, HEAD_DIM, PREP_TM),
                     lambda bi, hp, qi: (bi, 0, hp // steps_per_group, 0)),
    ]
    a_kmax_spec = pl.BlockSpec((1, 1, 1, ATTN_TQ), lambda bi, hp, qi: (bi, hp // steps_per_group, 0, 0))
    a_common = dict(
        grid=(b, A_Q_HEADS // A_STREAMS, nq),
        out_specs=pl.BlockSpec((1, ATTN_TQ, A_STREAMS * HEAD_DIM), lambda bi, hp, qi: (bi, qi, hp)),
        out_shape=jax.ShapeDtypeStruct((b, s, A_WIDTH), BF16),
        compiler_params=_params("parallel", "parallel", "arbitrary"),
    )

    kmax_a_rows = jnp.broadcast_to(kmax_a[:, :, None, None], (b, A_KV_HEADS, 1, ATTN_TQ))
    kmax_b_rows = jnp.broadcast_to(kmax_b[:, :, :, None, None], (b, B_HEADS, 2, 1, ATTN_TQ))

    def attn_a_fast():
        return pl.pallas_call(_attn_a_fast, in_specs=a_specs + [a_kmax_spec],
                              scratch_shapes=fast_scratch(A_STREAMS, HEAD_DIM), name="attn_a_fast",
                              **a_common)(qta, ka, vta, kmax_a_rows)

    def attn_a_general():
        return pl.pallas_call(_attn_a_general, in_specs=a_specs,
                              scratch_shapes=general_scratch(A_STREAMS, HEAD_DIM), name="attn_a",
                              **a_common)(qta, ka, vta)

    oa = lax.cond(fast_a, attn_a_fast, attn_a_general)

    lam_spec = _const_spec((1, HEAD_DIM))
    hb = B_STEP_HEADS
    b_specs = [
        pl.BlockSpec((1, 1, hb, 2, SLAB, ATTN_TQ), lambda bi, hp, qi: (bi, qi, hp, 0, 0, 0)),
        pl.BlockSpec((1, hb, 2, s, SLAB), lambda bi, hp, qi: (bi, hp, 0, 0, 0)),
        pl.BlockSpec((1, n_chunks, hb * B_V_DIM, PREP_TM), lambda bi, hp, qi: (bi, 0, hp, 0)),
    ]
    b_kmax_spec = pl.BlockSpec((1, hb, 2, 1, ATTN_TQ), lambda bi, hp, qi: (bi, hp, 0, 0, 0))
    b_tail_specs = [lam_spec, lam_spec, lam_spec, lam_spec, _const_spec((B_V_DIM, 1))]
    b_tail = (row(lq1), row(lk1), row(lq2), row(lk2), col(subln))
    b_common = dict(
        grid=(b, B_HEADS // hb, nq),
        out_specs=pl.BlockSpec((1, ATTN_TQ, hb * B_V_DIM), lambda bi, hp, qi: (bi, qi, hp)),
        out_shape=jax.ShapeDtypeStruct((b, s, B_WIDTH), BF16),
        compiler_params=_params("parallel", "parallel", "arbitrary"),
    )

    def attn_b_fast():
        return pl.pallas_call(functools.partial(_attn_b_fast, lambda_init),
                              in_specs=b_specs + [b_kmax_spec] + b_tail_specs,
                              scratch_shapes=fast_scratch(2 * hb, B_V_DIM), name="attn_b_fast",
                              **b_common)(qtb, kb, vtb, kmax_b_rows, *b_tail)

    def attn_b_general():
        return pl.pallas_call(functools.partial(_attn_b_general, lambda_init),
                              in_specs=b_specs + b_tail_specs,
                              scratch_shapes=general_scratch(2 * hb, B_V_DIM), name="attn_b",
                              **b_common)(qtb, kb, vtb, *b_tail)

    ob = lax.cond(fast_b, attn_b_fast, attn_b_general)

    tok_spec = lambda width: pl.BlockSpec((DENSE_TM, width), lambda ti: (ti, 0))
    x_flat = x.reshape(n_tok, d)
    merge_steps = n_tok // DENSE_TM
    ffn_shapes = [w_gate.shape, w_up.shape, w_down.shape]
    ffn_specs = [_chunk_spec(sh, merge_steps) for sh in ffn_shapes]
    x1, *ffn_w = pl.pallas_call(
        _merge_kernel,
        grid=(merge_steps,),
        in_specs=[
            tok_spec(d), tok_spec(A_WIDTH), tok_spec(B_WIDTH),
            _const_spec((1, d)),
        ] + [_const_spec(sh) for sh in merge_w_shapes] + ffn_specs,
        out_specs=[tok_spec(d)] + ffn_specs,
        out_shape=[jax.ShapeDtypeStruct((n_tok, d), F32)]
        + [jax.ShapeDtypeStruct(sh, BF16) for sh in ffn_shapes],
        compiler_params=_params("arbitrary"),
        name="merge",
    )(x_flat, oa.reshape(n_tok, A_WIDTH), ob.reshape(n_tok, B_WIDTH), row(norm_mix),
      *merge_w, w_gate, w_up, w_down)

    x2 = pl.pallas_call(
        functools.partial(_ffn_kernel, final),
        grid=(n_tok // FFN_TM,),
        in_specs=[
            pl.BlockSpec((FFN_TM, d), lambda ti: (ti, 0)),
            _const_spec((1, d)),
            _const_spec((d, d_ff)),
            _const_spec((d, d_ff)),
            _const_spec((d_ff, d)),
            _const_spec((1, d)),
        ],
        out_specs=pl.BlockSpec((FFN_TM, d), lambda ti: (ti, 0)),
        out_shape=jax.ShapeDtypeStruct((n_tok, d), F32),
        compiler_params=_params("parallel"),
        name="ffn",
    )(x1, row(norm_ffn), *ffn_w, row(norm_final))
    return x2.reshape(b, s, d)


def kernel(x, norm_mix, w_in, q_norm_a, k_norm_a, lambda_q1, lambda_k1, lambda_q2, lambda_k2,
           subln_b, w_proj_a, w_proj_b, w_out, norm_ffn, w_gate_ffn, w_up_ffn, w_down_ffn,
           norm_final):
    depth = norm_mix.shape[0]
    tables = _angle_tables(x.shape[1])
    for l in range(depth):
        lambda_init = 0.8 - 0.6 * math.exp(-0.3 * l)
        x = _layer(x, lambda_init, l == depth - 1, norm_mix[l], w_in[l], q_norm_a[l], k_norm_a[l],
                   lambda_q1[l], lambda_k1[l], lambda_q2[l], lambda_k2[l], subln_b[l],
                   w_proj_a[l], w_proj_b[l], w_out[l], norm_ffn[l], w_gate_ffn[l], w_up_ffn[l],
                   w_down_ffn[l], norm_final, tables)
    return x
```

```python
import functools
import math

import jax
import jax.numpy as jnp
from jax import lax
from jax.experimental import pallas as pl
from jax.experimental.pallas import tpu as pltpu

F32 = jnp.float32
BF16 = jnp.bfloat16

GRID_W = 64
HEAD_DIM = 64
HALF = HEAD_DIM // 2
A_Q_HEADS = 8
A_KV_HEADS = 2
A_GROUP = A_Q_HEADS // A_KV_HEADS
A_WIDTH = A_Q_HEADS * HEAD_DIM
B_HEADS = 4
B_V_DIM = 2 * HEAD_DIM
B_WIDTH = B_HEADS * B_V_DIM
ROPE_THETA = 10000.0
AXIAL_THETA = 10000.0
NORM_EPS = 1e-6
QK_SCALE = math.log2(math.e) / math.sqrt(HEAD_DIM)
A_STREAMS = 4
B_STEP_HEADS = 2
SCORE_LAG = 2
SCORE_SLOTS = 3

OFF_AQ = 0
OFF_AK = OFF_AQ + A_WIDTH
OFF_AV = OFF_AK + A_KV_HEADS * HEAD_DIM
OFF_BQ = OFF_AV + A_KV_HEADS * HEAD_DIM
OFF_BK = OFF_BQ + B_HEADS * 2 * HEAD_DIM
OFF_BV = OFF_BK + B_HEADS * 2 * HEAD_DIM
OFF_GATES = OFF_BV + B_WIDTH
SLAB = 2 * HEAD_DIM
SUBLANES = 8
NORM_ROWS = 2 * SUBLANES
FAST_BOUND = 40.0

PREP_TM = 512
PREP_PARTS = 2
ATTN_TQ = PREP_TM
FAST_KC = 8192
FAST_ORDER_A = (256, 4)
FAST_ORDER_B = (4096, 2)
DENSE_TM = 512
DENSE_PARTS = 2
FFN_TM = 1024
FFN_PARTS = 4
VMEM_LIMIT = 56 * 1024 * 1024


def _rms(x, axis):
    return x * lax.rsqrt(jnp.mean(x * x, axis=axis, keepdims=True) + NORM_EPS)


def _rope_t(xt, cos, sin):
    x1, x2 = xt[:HALF], xt[HALF:]
    return jnp.concatenate([x1 * cos - x2 * sin, x2 * cos + x1 * sin], axis=0)


def _prep_kernel(x_ref, nw_ref, w_ref, gq_ref, gk_ref, ca_ref, sa_ref, cb_ref, sb_ref,
                 win_rows_ref, pa_ref, pb_ref, po_ref,
                 qta_ref, ka_ref, vta_ref, qtb_ref, kb_ref, vtb_ref, kn_ref, qn_ref,
                 wg_out, pa_out, pb_out, po_out, wb_ref):
    @pl.when((pl.program_id(0) == 0) & (pl.program_id(1) == 0))
    def _():
        wb_ref[...] = w_ref[...].astype(BF16)

    wg_out[...] = win_rows_ref[:, OFF_GATES:].astype(BF16)
    for src, dst in ((pa_ref, pa_out), (pb_ref, pb_out), (po_ref, po_out)):
        dst[...] = src[...].astype(BF16)

    tm = x_ref.shape[1] // PREP_PARTS
    gq, gk = gq_ref[...], gk_ref[...]
    pad = jnp.zeros((SLAB - HEAD_DIM - SUBLANES, tm), F32)
    one_row = (lax.broadcasted_iota(jnp.int32, (SUBLANES, tm), 0) == 0).astype(F32)

    def norm(v):
        return jnp.sqrt(jnp.sum(v * v, axis=0, keepdims=True))

    def k_slab(k):
        return jnp.concatenate([k, one_row, pad], axis=0).T.astype(BF16)

    groups = [slice(i * tm, (i + 1) * tm) for i in range(PREP_PARTS)]
    zs = []
    for t in groups:
        h = _rms(x_ref[0, t], -1) * nw_ref[...]
        zs.append(jnp.dot(h.astype(BF16), wb_ref[...], preferred_element_type=F32))

    for t, z in zip(groups, zs):
        ca, sa, cb, sb = ca_ref[:, t], sa_ref[:, t], cb_ref[:, t], sb_ref[:, t]
        q_norms, k_norms = [], []

        def q_block(q):
            nq = norm(q).astype(BF16).astype(F32)
            q_norms.append(nq)
            return jnp.concatenate([q, jnp.broadcast_to(nq, (SUBLANES, tm)), pad], axis=0).astype(BF16)

        aqt = z[:, OFF_AQ:OFF_AK].T
        for hd in range(A_Q_HEADS):
            q = aqt[hd * HEAD_DIM:(hd + 1) * HEAD_DIM]
            qta_ref[0, 0, hd, :, t] = q_block(_rope_t(_rms(q, 0) * gq, ca, sa) * QK_SCALE)
        akt = z[:, OFF_AK:OFF_AV].T
        for g in range(A_KV_HEADS):
            k = _rope_t(_rms(akt[g * HEAD_DIM:(g + 1) * HEAD_DIM], 0) * gk, ca, sa)
            ka_ref[0, g, t, :] = k_slab(k)
            k_norms.append(norm(k))
        k_norms.append(jnp.zeros((SUBLANES - A_KV_HEADS, tm), F32))
        vta_ref[0, 0, :, t] = z[:, OFF_AV:OFF_BQ].T.astype(BF16)

        bqt = z[:, OFF_BQ:OFF_BK].T
        bkt = z[:, OFF_BK:OFF_BV].T
        for hd in range(B_HEADS):
            for c in range(2):
                r0 = (hd * 2 + c) * HEAD_DIM
                qtb_ref[0, 0, hd, c, :, t] = q_block(_rope_t(bqt[r0:r0 + HEAD_DIM], cb, sb) * QK_SCALE)
                k = _rope_t(bkt[r0:r0 + HEAD_DIM], cb, sb)
                kb_ref[0, hd, c, t, :] = k_slab(k)
                k_norms.append(norm(k))
        vtb_ref[0, 0, :, t] = z[:, OFF_BV:OFF_GATES].T.astype(BF16)
        kn_ref[0, :, t] = jnp.concatenate(k_norms, axis=0)
        qn_ref[0, :, t] = jnp.concatenate(q_norms, axis=0)


def _plain_query(qt):
    qf = qt.astype(F32)
    rows = lax.broadcasted_iota(jnp.int32, qf.shape, 0)
    return jnp.where(rows < HEAD_DIM, qf, 0.0).astype(BF16)


def _shifted_query(qt, kmax):
    qf = qt.astype(F32)
    rows = lax.broadcasted_iota(jnp.int32, qf.shape, 0)
    shifted = jnp.where(rows == HEAD_DIM, -(qf * kmax), jnp.where(rows < HEAD_DIM, qf, 0.0))
    return shifted.astype(BF16)


def _flash_fast(q_list, k_of, vt_ref, v_rows, l_ref, acc_ref, sb, skew):
    n = len(q_list)
    vb = vt_ref.shape[3]
    pb = min(sb, vb)
    l_ref[...] = jnp.zeros(l_ref.shape, F32)
    acc_ref[...] = jnp.zeros(acc_ref.shape, F32)

    def body(i, carry):
        off = pl.multiple_of(i * FAST_KC, FAST_KC)
        l = [l_ref[j] for j in range(n)]
        acc = [acc_ref[j] for j in range(n)]
        items = [(g, j) for g in range(FAST_KC // sb) for j in range(n)]
        scores = {}

        def consume(g, j):
            p = jnp.exp2(scores.pop((g, j)))
            l[j] = l[j] + jnp.sum(p, axis=0, keepdims=True)
            p = p.astype(BF16)
            for u in range(sb // pb):
                k0 = g * sb + u * pb
                vblk = vt_ref[0, i * (FAST_KC // vb) + k0 // vb, v_rows(j), k0 % vb:k0 % vb + pb]
                acc[j] = acc[j] + jnp.dot(vblk, p[u * pb:(u + 1) * pb], preferred_element_type=F32)

        for t, (g, j) in enumerate(items):
            scores[(g, j)] = jnp.dot(k_of(j, off + g * sb, sb), q_list[j],
                                     preferred_element_type=F32)
            if t >= skew:
                consume(*items[t - skew])
        for g, j in items[max(len(items) - skew, 0):]:
            consume(g, j)
        for j in range(n):
            l_ref[j], acc_ref[j] = l[j], acc[j]
        return carry

    lax.fori_loop(0, vt_ref.shape[1] * vb // FAST_KC, body, 0)


def _flash_streams(q_list, k_of, vt_ref, v_rows, s_buf, mx_buf, m_ref, l_ref, acc_ref):
    n = len(q_list)
    n_chunks, kc = vt_ref.shape[1], vt_ref.shape[3]
    m_ref[...] = jnp.full(m_ref.shape, -jnp.inf, F32)
    l_ref[...] = jnp.zeros(l_ref.shape, F32)
    acc_ref[...] = jnp.zeros(acc_ref.shape, F32)

    def scores(c, slot):
        off = c * kc if isinstance(c, int) else pl.multiple_of(c * kc, kc)
        for j in range(n):
            s = jnp.dot(k_of(j, off, kc), q_list[j], preferred_element_type=F32)
            s_buf[slot, j] = s
            mx_buf[slot, j] = jnp.max(s, axis=0, keepdims=True)

    def softmax_pv(c, slot):
        for j in range(n):
            vblk = vt_ref[0, c, v_rows(j)]
            m_old = m_ref[j]
            m_new = jnp.maximum(m_old, mx_buf[slot, j])
            alpha = jnp.exp2(m_old - m_new)
            p = jnp.exp2(s_buf[slot, j] - m_new)
            l_ref[j] = alpha * l_ref[j] + jnp.sum(p, axis=0, keepdims=True)
            acc_ref[j] = alpha * acc_ref[j] + jnp.dot(vblk, p.astype(BF16),
                                                      preferred_element_type=F32)
            m_ref[j] = m_new

    n_slots = s_buf.shape[0]

    def stage(c, u):
        if not isinstance(c, int) or c + SCORE_LAG < n_chunks:
            scores(c + SCORE_LAG, (u + SCORE_LAG) % n_slots)
        softmax_pv(c, u)

    for c in range(SCORE_LAG):
        scores(c, c % n_slots)
    n_loop = (n_chunks - SCORE_LAG) // n_slots

    def body(i, carry):
        for u in range(n_slots):
            stage(i * n_slots + u, u)
        return carry

    lax.fori_loop(0, n_loop, body, 0)
    for c in range(n_loop * n_slots, n_chunks):
        stage(c, c % n_slots)


def _finish_a(o_ref, l_ref, acc_ref):
    outs = [acc_ref[j] * (1.0 / l_ref[j]) for j in range(A_STREAMS)]
    o_ref[0] = jnp.concatenate(outs, axis=0).T.astype(BF16)


def _finish_b(lambda_init, lam_refs, sub_ref, o_ref, l_ref, acc_ref):
    lq1_ref, lk1_ref, lq2_ref, lk2_ref = lam_refs
    lam = (jnp.exp(jnp.sum(lq1_ref[...] * lk1_ref[...], axis=-1, keepdims=True))
           - jnp.exp(jnp.sum(lq2_ref[...] * lk2_ref[...], axis=-1, keepdims=True))
           + lambda_init)
    outs = []
    for hd in range(B_STEP_HEADS):
        j1, j2 = 2 * hd, 2 * hd + 1
        o = acc_ref[j1] * (1.0 / l_ref[j1]) - lam * (acc_ref[j2] * (1.0 / l_ref[j2]))
        outs.append(_rms(o, 0) * sub_ref[...] * (1.0 - lambda_init))
    o_ref[0] = jnp.concatenate(outs, axis=0).T.astype(BF16)


def _all_rows(j):
    return slice(None)


def _b_rows(j):
    return slice((j // 2) * B_V_DIM, (j // 2 + 1) * B_V_DIM)


def _k_of_a(k_ref):
    return lambda j, off, size: k_ref[0, 0, pl.ds(off, size), :]


def _k_of_b(k_ref):
    return lambda j, off, size: k_ref[0, j // 2, j % 2, pl.ds(off, size), :]


def _attn_a_general(qt_ref, k_ref, vt_ref, o_ref, s_buf, mx_buf, m_ref, l_ref, acc_ref):
    q_list = [_plain_query(qt_ref[0, 0, j]) for j in range(A_STREAMS)]
    _flash_streams(q_list, _k_of_a(k_ref), vt_ref, _all_rows, s_buf, mx_buf, m_ref, l_ref, acc_ref)
    _finish_a(o_ref, l_ref, acc_ref)


def _attn_a_fast(qt_ref, k_ref, vt_ref, kmax_ref, o_ref, l_ref, acc_ref):
    q_list = [_shifted_query(qt_ref[0, 0, j], kmax_ref[0, 0]) for j in range(A_STREAMS)]
    _flash_fast(q_list, _k_of_a(k_ref), vt_ref, _all_rows, l_ref, acc_ref, *FAST_ORDER_A)
    _finish_a(o_ref, l_ref, acc_ref)


def _attn_b_general(lambda_init, qt_ref, k_ref, vt_ref, lq1_ref, lk1_ref, lq2_ref, lk2_ref,
                    sub_ref, o_ref, s_buf, mx_buf, m_ref, l_ref, acc_ref):
    q_list = [_plain_query(qt_ref[0, 0, hd, c]) for hd in range(B_STEP_HEADS) for c in range(2)]
    _flash_streams(q_list, _k_of_b(k_ref), vt_ref, _b_rows, s_buf, mx_buf, m_ref, l_ref, acc_ref)
    _finish_b(lambda_init, (lq1_ref, lk1_ref, lq2_ref, lk2_ref), sub_ref, o_ref, l_ref, acc_ref)


def _attn_b_fast(lambda_init, qt_ref, k_ref, vt_ref, kmax_ref, lq1_ref, lk1_ref, lq2_ref, lk2_ref,
                 sub_ref, o_ref, l_ref, acc_ref):
    q_list = [_shifted_query(qt_ref[0, 0, hd, c], kmax_ref[0, hd, c])
              for hd in range(B_STEP_HEADS) for c in range(2)]
    _flash_fast(q_list, _k_of_b(k_ref), vt_ref, _b_rows, l_ref, acc_ref, *FAST_ORDER_B)
    _finish_b(lambda_init, (lq1_ref, lk1_ref, lq2_ref, lk2_ref), sub_ref, o_ref, l_ref, acc_ref)


def _row_parts(ref, n_parts):
    rows = ref.shape[0] // n_parts
    return [slice(i * rows, (i + 1) * rows) for i in range(n_parts)]


def _merge_kernel(x_ref, oa_ref, ob_ref, nw_ref, wg_s, wa_s, wb_s, wo_s,
                  fg_ref, fu_ref, fd_ref, x1_ref, fg_out, fu_out, fd_out):
    for src, dst in ((fg_ref, fg_out), (fu_ref, fu_out), (fd_ref, fd_out)):
        dst[...] = src[...].astype(BF16)

    d = x_ref.shape[-1]
    parts = _row_parts(x_ref, DENSE_PARTS)
    stage1 = []
    for r in parts:
        x = x_ref[r]
        h = (_rms(x, -1) * nw_ref[...]).astype(BF16)
        gates = jnp.dot(h, wg_s[...], preferred_element_type=F32)
        ya = jnp.dot(oa_ref[r], wa_s[...], preferred_element_type=F32)
        yb = jnp.dot(ob_ref[r], wb_s[...], preferred_element_type=F32)
        stage1.append((x, gates, ya, yb))
    for r, (x, gates, ya, yb) in zip(parts, stage1):
        y = jax.nn.sigmoid(gates[:, :d]) * ya + jax.nn.sigmoid(gates[:, d:]) * yb
        x1_ref[r] = x + jnp.dot(y.astype(BF16), wo_s[...], preferred_element_type=F32)


def _ffn_kernel(final, x_ref, nw_ref, wg_ref, wu_ref, wd_ref, nf_ref, o_ref):
    def stage1(r):
        x = x_ref[r]
        h = (_rms(x, -1) * nw_ref[...]).astype(BF16)
        gate = jnp.dot(h, wg_ref[...], preferred_element_type=F32)
        up = jnp.dot(h, wu_ref[...], preferred_element_type=F32)
        return r, x, gate, up

    def stage2(r, x, gate, up):
        act = (jax.nn.silu(gate) * up).astype(BF16)
        x2 = x + jnp.dot(act, wd_ref[...], preferred_element_type=F32)
        if final:
            x2 = _rms(x2, -1) * nf_ref[...]
        o_ref[r] = x2

    pending = None
    for r in _row_parts(x_ref, FFN_PARTS):
        nxt = stage1(r)
        if pending is not None:
            stage2(*pending)
        pending = nxt
    stage2(*pending)


def _const_spec(shape):
    nd = len(shape)
    return pl.BlockSpec(shape, lambda *_: (0,) * nd, pipeline_mode=pl.Buffered(1))


def _chunk_spec(shape, n_steps, step_of=lambda i: i):
    n_rows = shape[0]
    n_chunks = n_steps
    while n_rows % (n_chunks * 2 * SUBLANES):
        assert n_chunks % 2 == 0, shape
        n_chunks //= 2
    repeat = n_steps // n_chunks
    return pl.BlockSpec((n_rows // n_chunks, shape[1]), lambda *ids: (step_of(*ids) // repeat, 0))


def _rope_angles_t(pos, dim, theta):
    inv_freq = theta ** (-jnp.arange(0, dim, 2, dtype=F32) / dim)
    return (pos[:, None] * inv_freq[None, :]).T


def _angle_tables(seq_len):
    rows = seq_len // GRID_W
    row = jnp.broadcast_to(jnp.arange(rows, dtype=F32)[:, None], (rows, GRID_W)).reshape(-1)
    col = jnp.broadcast_to(jnp.arange(GRID_W, dtype=F32)[None, :], (rows, GRID_W)).reshape(-1)
    ang_a = jnp.concatenate([_rope_angles_t(row, HALF, AXIAL_THETA),
                             _rope_angles_t(col, HALF, AXIAL_THETA)], axis=0)
    ang_b = _rope_angles_t(jnp.arange(seq_len, dtype=F32), HEAD_DIM, ROPE_THETA)
    return jnp.cos(ang_a), jnp.sin(ang_a), jnp.cos(ang_b), jnp.sin(ang_b)


def _params(*sem):
    return pltpu.CompilerParams(dimension_semantics=sem, vmem_limit_bytes=VMEM_LIMIT)


def _layer(x, lambda_init, final, norm_mix, w_in, q_norm_a, k_norm_a, lq1, lk1, lq2, lk2, subln,
           w_proj_a, w_proj_b, w_out, norm_ffn, w_gate, w_up, w_down, norm_final, tables):
    b, s, d = x.shape
    d_ff = w_gate.shape[-1]
    n_tok = b * s
    n_chunks = s // PREP_TM
    ca, sa, cb, sb = tables
    row = lambda v: v.reshape(1, -1).astype(F32)
    col = lambda v: v.reshape(-1, 1).astype(F32)

    tab_spec = pl.BlockSpec((HALF, PREP_TM), lambda bi, ti: (0, ti))
    prep_step = lambda bi, ti: bi * n_chunks + ti
    merge_w_shapes = [(d, 2 * d), w_proj_a.shape, w_proj_b.shape, w_out.shape]
    merge_w_in = [_chunk_spec(sh, b * n_chunks, prep_step)
                  for sh in [w_in.shape] + merge_w_shapes[1:]]
    merge_w_out = [_chunk_spec(sh, b * n_chunks, prep_step) for sh in merge_w_shapes]
    qta, ka, vta, qtb, kb, vtb, kn, qn, *merge_w = pl.pallas_call(
        _prep_kernel,
        grid=(b, n_chunks),
        in_specs=[
            pl.BlockSpec((1, PREP_TM, d), lambda bi, ti: (bi, ti, 0)),
            _const_spec((1, d)),
            _const_spec((d, OFF_GATES)),
            _const_spec((HEAD_DIM, 1)),
            _const_spec((HEAD_DIM, 1)),
            tab_spec, tab_spec, tab_spec, tab_spec,
        ] + merge_w_in,
        out_specs=[
            pl.BlockSpec((1, 1, A_Q_HEADS, SLAB, PREP_TM), lambda bi, ti: (bi, ti, 0, 0, 0)),
            pl.BlockSpec((1, A_KV_HEADS, PREP_TM, SLAB), lambda bi, ti: (bi, 0, ti, 0)),
            pl.BlockSpec((1, 1, A_KV_HEADS * HEAD_DIM, PREP_TM), lambda bi, ti: (bi, ti, 0, 0)),
            pl.BlockSpec((1, 1, B_HEADS, 2, SLAB, PREP_TM), lambda bi, ti: (bi, ti, 0, 0, 0, 0)),
            pl.BlockSpec((1, B_HEADS, 2, PREP_TM, SLAB), lambda bi, ti: (bi, 0, 0, ti, 0)),
            pl.BlockSpec((1, 1, B_WIDTH, PREP_TM), lambda bi, ti: (bi, ti, 0, 0)),
            pl.BlockSpec((1, NORM_ROWS, PREP_TM), lambda bi, ti: (bi, 0, ti)),
            pl.BlockSpec((1, NORM_ROWS, PREP_TM), lambda bi, ti: (bi, 0, ti)),
        ] + merge_w_out,
        out_shape=[
            jax.ShapeDtypeStruct((b, n_chunks, A_Q_HEADS, SLAB, PREP_TM), BF16),
            jax.ShapeDtypeStruct((b, A_KV_HEADS, s, SLAB), BF16),
            jax.ShapeDtypeStruct((b, n_chunks, A_KV_HEADS * HEAD_DIM, PREP_TM), BF16),
            jax.ShapeDtypeStruct((b, n_chunks, B_HEADS, 2, SLAB, PREP_TM), BF16),
            jax.ShapeDtypeStruct((b, B_HEADS, 2, s, SLAB), BF16),
            jax.ShapeDtypeStruct((b, n_chunks, B_WIDTH, PREP_TM), BF16),
            jax.ShapeDtypeStruct((b, NORM_ROWS, s), F32),
            jax.ShapeDtypeStruct((b, NORM_ROWS, s), F32),
        ] + [jax.ShapeDtypeStruct(sh, BF16) for sh in merge_w_shapes],
        scratch_shapes=[pltpu.VMEM((d, OFF_GATES), BF16)],
        compiler_params=_params("arbitrary", "arbitrary"),
        name="prep",
    )(x, row(norm_mix), w_in, col(q_norm_a), col(k_norm_a), ca, sa, cb, sb,
      w_in, w_proj_a, w_proj_b, w_out)

    nq = s // ATTN_TQ
    steps_per_group = A_GROUP // A_STREAMS
    kmax_a = jnp.max(kn[:, :A_KV_HEADS], axis=-1)
    kmax_b = jnp.max(kn[:, SUBLANES:], axis=-1).reshape(b, B_HEADS, 2)
    qmax_a = jnp.max(qn[:, :A_Q_HEADS], axis=-1)
    qmax_b = jnp.max(qn[:, A_Q_HEADS:], axis=-1).reshape(b, B_HEADS, 2)
    fast_a = jnp.max(qmax_a * jnp.repeat(kmax_a, A_GROUP, axis=1)) < FAST_BOUND
    fast_b = jnp.max(qmax_b * kmax_b) < FAST_BOUND

    def general_scratch(n, dv):
        return [
            pltpu.VMEM((SCORE_SLOTS, n, PREP_TM, ATTN_TQ), F32),
            pltpu.VMEM((SCORE_SLOTS, n, 1, ATTN_TQ), F32),
            pltpu.VMEM((n, 1, ATTN_TQ), F32),
            pltpu.VMEM((n, 1, ATTN_TQ), F32),
            pltpu.VMEM((n, dv, ATTN_TQ), F32),
        ]

    def fast_scratch(n, dv):
        return [pltpu.VMEM((n, 1, ATTN_TQ), F32), pltpu.VMEM((n, dv, ATTN_TQ), F32)]

    a_specs = [
        pl.BlockSpec((1, 1, A_STREAMS, SLAB, ATTN_TQ), lambda bi, hp, qi: (bi, qi, hp, 0, 0)),
        pl.BlockSpec((1, 1, s, SLAB), lambda bi, hp, qi: (bi, hp // steps_per_group, 0, 0)),
        pl.BlockSpec((1, n_chunks, HEAD_DIM, PREP_TM),
                     lambda bi, hp, qi: (bi, 0, hp // steps_per_group, 0)),
    ]
    a_kmax_spec = pl.BlockSpec((1, 1, 1, ATTN_TQ), lambda bi, hp, qi: (bi, hp // steps_per_group, 0, 0))
    a_common = dict(
        grid=(b, A_Q_HEADS // A_STREAMS, nq),
        out_specs=pl.BlockSpec((1, ATTN_TQ, A_STREAMS * HEAD_DIM), lambda bi, hp, qi: (bi, qi, hp)),
        out_shape=jax.ShapeDtypeStruct((b, s, A_WIDTH), BF16),
        compiler_params=_params("parallel", "parallel", "arbitrary"),
    )

    kmax_a_rows = jnp.broadcast_to(kmax_a[:, :, None, None], (b, A_KV_HEADS, 1, ATTN_TQ))
    kmax_b_rows = jnp.broadcast_to(kmax_b[:, :, :, None, None], (b, B_HEADS, 2, 1, ATTN_TQ))

    def attn_a_fast():
        return pl.pallas_call(_attn_a_fast, in_specs=a_specs + [a_kmax_spec],
                              scratch_shapes=fast_scratch(A_STREAMS, HEAD_DIM), name="attn_a_fast",
                              **a_common)(qta, ka, vta, kmax_a_rows)

    def attn_a_general():
        return pl.pallas_call(_attn_a_general, in_specs=a_specs,
                              scratch_shapes=general_scratch(A_STREAMS, HEAD_DIM), name="attn_a",
                              **a_common)(qta, ka, vta)

    lam_spec = _const_spec((1, HEAD_DIM))
    hb = B_STEP_HEADS
    b_specs = [
        pl.BlockSpec((1, 1, hb, 2, SLAB, ATTN_TQ), lambda bi, hp, qi: (bi, qi, hp, 0, 0, 0)),
        pl.BlockSpec((1, hb, 2, s, SLAB), lambda bi, hp, qi: (bi, hp, 0, 0, 0)),
        pl.BlockSpec((1, n_chunks, hb * B_V_DIM, PREP_TM), lambda bi, hp, qi: (bi, 0, hp, 0)),
    ]
    b_kmax_spec = pl.BlockSpec((1, hb, 2, 1, ATTN_TQ), lambda bi, hp, qi: (bi, hp, 0, 0, 0))
    b_tail_specs = [lam_spec, lam_spec, lam_spec, lam_spec, _const_spec((B_V_DIM, 1))]
    b_tail = (row(lq1), row(lk1), row(lq2), row(lk2), col(subln))
    b_common = dict(
        grid=(b, B_HEADS // hb, nq),
        out_specs=pl.BlockSpec((1, ATTN_TQ, hb * B_V_DIM), lambda bi, hp, qi: (bi, qi, hp)),
        out_shape=jax.ShapeDtypeStruct((b, s, B_WIDTH), BF16),
        compiler_params=_params("parallel", "parallel", "arbitrary"),
    )

    def attn_b_fast():
        return pl.pallas_call(functools.partial(_attn_b_fast, lambda_init),
                              in_specs=b_specs + [b_kmax_spec] + b_tail_specs,
                              scratch_shapes=fast_scratch(2 * hb, B_V_DIM), name="attn_b_fast",
                              **b_common)(qtb, kb, vtb, kmax_b_rows, *b_tail)

    def attn_b_general():
        return pl.pallas_call(functools.partial(_attn_b_general, lambda_init),
                              in_specs=b_specs + b_tail_specs,
                              scratch_shapes=general_scratch(2 * hb, B_V_DIM), name="attn_b",
                              **b_common)(qtb, kb, vtb, *b_tail)

    oa, ob = lax.cond(fast_a & fast_b,
                      lambda: (attn_a_fast(), attn_b_fast()),
                      lambda: (attn_a_general(), attn_b_general()))

    tok_spec = lambda width: pl.BlockSpec((DENSE_TM, width), lambda ti: (ti, 0))
    x_flat = x.reshape(n_tok, d)
    merge_steps = n_tok // DENSE_TM
    ffn_shapes = [w_gate.shape, w_up.shape, w_down.shape]
    ffn_specs = [_chunk_spec(sh, merge_steps) for sh in ffn_shapes]
    x1, *ffn_w = pl.pallas_call(
        _merge_kernel,
        grid=(merge_steps,),
        in_specs=[
            tok_spec(d), tok_spec(A_WIDTH), tok_spec(B_WIDTH),
            _const_spec((1, d)),
        ] + [_const_spec(sh) for sh in merge_w_shapes] + ffn_specs,
        out_specs=[tok_spec(d)] + ffn_specs,
        out_shape=[jax.ShapeDtypeStruct((n_tok, d), F32)]
        + [jax.ShapeDtypeStruct(sh, BF16) for sh in ffn_shapes],
        compiler_params=_params("arbitrary"),
        name="merge",
    )(x_flat, oa.reshape(n_tok, A_WIDTH), ob.reshape(n_tok, B_WIDTH), row(norm_mix),
      *merge_w, w_gate, w_up, w_down)

    x2 = pl.pallas_call(
        functools.partial(_ffn_kernel, final),
        grid=(n_tok // FFN_TM,),
        in_specs=[
            pl.BlockSpec((FFN_TM, d), lambda ti: (ti, 0)),
            _const_spec((1, d)),
            _const_spec((d, d_ff)),
            _const_spec((d, d_ff)),
            _const_spec((d_ff, d)),
            _const_spec((1, d)),
        ],
        out_specs=pl.BlockSpec((FFN_TM, d), lambda ti: (ti, 0)),
        out_shape=jax.ShapeDtypeStruct((n_tok, d), F32),
        compiler_params=_params("parallel"),
        name="ffn",
    )(x1, row(norm_ffn), *ffn_w, row(norm_final))
    return x2.reshape(b, s, d)


def kernel(x, norm_mix, w_in, q_norm_a, k_norm_a, lambda_q1, lambda_k1, lambda_q2, lambda_k2,
           subln_b, w_proj_a, w_proj_b, w_out, norm_ffn, w_gate_ffn, w_up_ffn, w_down_ffn,
           norm_final):
    depth = norm_mix.shape[0]
    tables = _angle_tables(x.shape[1])
    for l in range(depth):
        lambda_init = 0.8 - 0.6 * math.exp(-0.3 * l)
        x = _layer(x, lambda_init, l == depth - 1, norm_mix[l], w_in[l], q_norm_a[l], k_norm_a[l],
                   lambda_q1[l], lambda_k1[l], lambda_q2[l], lambda_k2[l], subln_b[l],
                   w_proj_a[l], w_proj_b[l], w_out[l], norm_ffn[l], w_gate_ffn[l], w_up_ffn[l],
                   w_down_ffn[l], norm_final, tables)
    return x
```

```python
import functools
import math

import jax
import jax.numpy as jnp
from jax import lax
from jax.experimental import pallas as pl
from jax.experimental.pallas import tpu as pltpu

F32 = jnp.float32
BF16 = jnp.bfloat16

GRID_W = 64
HEAD_DIM = 64
HALF = HEAD_DIM // 2
A_Q_HEADS = 8
A_KV_HEADS = 2
A_GROUP = A_Q_HEADS // A_KV_HEADS
A_WIDTH = A_Q_HEADS * HEAD_DIM
B_HEADS = 4
B_V_DIM = 2 * HEAD_DIM
B_WIDTH = B_HEADS * B_V_DIM
ROPE_THETA = 10000.0
AXIAL_THETA = 10000.0
NORM_EPS = 1e-6
QK_SCALE = math.log2(math.e) / math.sqrt(HEAD_DIM)
A_STREAMS = 4
B_STEP_HEADS = 2
SCORE_LAG = 2
SCORE_SLOTS = 3

OFF_AQ = 0
OFF_AK = OFF_AQ + A_WIDTH
OFF_AV = OFF_AK + A_KV_HEADS * HEAD_DIM
OFF_BQ = OFF_AV + A_KV_HEADS * HEAD_DIM
OFF_BK = OFF_BQ + B_HEADS * 2 * HEAD_DIM
OFF_BV = OFF_BK + B_HEADS * 2 * HEAD_DIM
OFF_GATES = OFF_BV + B_WIDTH
SLAB = 2 * HEAD_DIM
SUBLANES = 8
NORM_ROWS = 2 * SUBLANES
FAST_BOUND = 40.0

PREP_TM = 512
PREP_PARTS = 2
ATTN_TQ = PREP_TM
FAST_KC = 8192
FAST_ORDER_A = (256, 4)
FAST_ORDER_B = (4096, 2)
DENSE_TM = 1024
DENSE_PARTS = 4
FFN_TM = 1024
FFN_PARTS = 4
VMEM_LIMIT = 56 * 1024 * 1024


def _rms(x, axis):
    return x * lax.rsqrt(jnp.mean(x * x, axis=axis, keepdims=True) + NORM_EPS)


def _rope_t(xt, cos, sin):
    x1, x2 = xt[:HALF], xt[HALF:]
    return jnp.concatenate([x1 * cos - x2 * sin, x2 * cos + x1 * sin], axis=0)


def _prep_kernel(x_ref, nw_ref, w_ref, gq_ref, gk_ref, ca_ref, sa_ref, cb_ref, sb_ref,
                 win_rows_ref, pa_ref, pb_ref, po_ref,
                 qta_ref, ka_ref, vta_ref, qtb_ref, kb_ref, vtb_ref, kn_ref, qn_ref,
                 wg_out, pa_out, pb_out, po_out, wb_ref):
    @pl.when((pl.program_id(0) == 0) & (pl.program_id(1) == 0))
    def _():
        wb_ref[...] = w_ref[...].astype(BF16)

    wg_out[...] = win_rows_ref[:, OFF_GATES:].astype(BF16)
    for src, dst in ((pa_ref, pa_out), (pb_ref, pb_out), (po_ref, po_out)):
        dst[...] = src[...].astype(BF16)

    tm = x_ref.shape[1] // PREP_PARTS
    gq, gk = gq_ref[...], gk_ref[...]
    pad = jnp.zeros((SLAB - HEAD_DIM - SUBLANES, tm), F32)
    one_row = (lax.broadcasted_iota(jnp.int32, (SUBLANES, tm), 0) == 0).astype(F32)

    def norm(v):
        return jnp.sqrt(jnp.sum(v * v, axis=0, keepdims=True))

    def k_slab(k):
        return jnp.concatenate([k, one_row, pad], axis=0).T.astype(BF16)

    groups = [slice(i * tm, (i + 1) * tm) for i in range(PREP_PARTS)]
    zs = []
    for t in groups:
        h = _rms(x_ref[0, t], -1) * nw_ref[...]
        zs.append(jnp.dot(h.astype(BF16), wb_ref[...], preferred_element_type=F32))

    for t, z in zip(groups, zs):
        ca, sa, cb, sb = ca_ref[:, t], sa_ref[:, t], cb_ref[:, t], sb_ref[:, t]
        q_norms, k_norms = [], []

        def q_block(q):
            nq = norm(q).astype(BF16).astype(F32)
            q_norms.append(nq)
            return jnp.concatenate([q, jnp.broadcast_to(nq, (SUBLANES, tm)), pad], axis=0).astype(BF16)

        aqt = z[:, OFF_AQ:OFF_AK].T
        for hd in range(A_Q_HEADS):
            q = aqt[hd * HEAD_DIM:(hd + 1) * HEAD_DIM]
            qta_ref[0, 0, hd, :, t] = q_block(_rope_t(_rms(q, 0) * gq, ca, sa) * QK_SCALE)
        akt = z[:, OFF_AK:OFF_AV].T
        for g in range(A_KV_HEADS):
            k = _rope_t(_rms(akt[g * HEAD_DIM:(g + 1) * HEAD_DIM], 0) * gk, ca, sa)
            ka_ref[0, g, t, :] = k_slab(k)
            k_norms.append(norm(k))
        k_norms.append(jnp.zeros((SUBLANES - A_KV_HEADS, tm), F32))
        vta_ref[0, 0, :, t] = z[:, OFF_AV:OFF_BQ].T.astype(BF16)

        bqt = z[:, OFF_BQ:OFF_BK].T
        bkt = z[:, OFF_BK:OFF_BV].T
        for hd in range(B_HEADS):
            for c in range(2):
                r0 = (hd * 2 + c) * HEAD_DIM
                qtb_ref[0, 0, hd, c, :, t] = q_block(_rope_t(bqt[r0:r0 + HEAD_DIM], cb, sb) * QK_SCALE)
                k = _rope_t(bkt[r0:r0 + HEAD_DIM], cb, sb)
                kb_ref[0, hd, c, t, :] = k_slab(k)
                k_norms.append(norm(k))
        vtb_ref[0, 0, :, t] = z[:, OFF_BV:OFF_GATES].T.astype(BF16)
        kn_ref[0, :, t] = jnp.concatenate(k_norms, axis=0)
        qn_ref[0, :, t] = jnp.concatenate(q_norms, axis=0)


def _plain_query(qt):
    qf = qt.astype(F32)
    rows = lax.broadcasted_iota(jnp.int32, qf.shape, 0)
    return jnp.where(rows < HEAD_DIM, qf, 0.0).astype(BF16)


def _shifted_query(qt, kmax):
    qf = qt.astype(F32)
    rows = lax.broadcasted_iota(jnp.int32, qf.shape, 0)
    shifted = jnp.where(rows == HEAD_DIM, -(qf * kmax), jnp.where(rows < HEAD_DIM, qf, 0.0))
    return shifted.astype(BF16)


def _flash_fast(q_list, k_of, vt_ref, v_rows, l_ref, acc_ref, sb, skew):
    n = len(q_list)
    vb = vt_ref.shape[3]
    pb = min(sb, vb)
    l_ref[...] = jnp.zeros(l_ref.shape, F32)
    acc_ref[...] = jnp.zeros(acc_ref.shape, F32)

    def body(i, carry):
        off = pl.multiple_of(i * FAST_KC, FAST_KC)
        l = [l_ref[j] for j in range(n)]
        acc = [acc_ref[j] for j in range(n)]
        items = [(g, j) for g in range(FAST_KC // sb) for j in range(n)]
        scores = {}

        def consume(g, j):
            p = jnp.exp2(scores.pop((g, j)))
            l[j] = l[j] + jnp.sum(p, axis=0, keepdims=True)
            p = p.astype(BF16)
            for u in range(sb // pb):
                k0 = g * sb + u * pb
                vblk = vt_ref[0, i * (FAST_KC // vb) + k0 // vb, v_rows(j), k0 % vb:k0 % vb + pb]
                acc[j] = acc[j] + jnp.dot(vblk, p[u * pb:(u + 1) * pb], preferred_element_type=F32)

        for t, (g, j) in enumerate(items):
            scores[(g, j)] = jnp.dot(k_of(j, off + g * sb, sb), q_list[j],
                                     preferred_element_type=F32)
            if t >= skew:
                consume(*items[t - skew])
        for g, j in items[max(len(items) - skew, 0):]:
            consume(g, j)
        for j in range(n):
            l_ref[j], acc_ref[j] = l[j], acc[j]
        return carry

    lax.fori_loop(0, vt_ref.shape[1] * vb // FAST_KC, body, 0)


def _flash_streams(q_list, k_of, vt_ref, v_rows, s_buf, mx_buf, m_ref, l_ref, acc_ref):
    n = len(q_list)
    n_chunks, kc = vt_ref.shape[1], vt_ref.shape[3]
    m_ref[...] = jnp.full(m_ref.shape, -jnp.inf, F32)
    l_ref[...] = jnp.zeros(l_ref.shape, F32)
    acc_ref[...] = jnp.zeros(acc_ref.shape, F32)

    def scores(c, slot):
        off = c * kc if isinstance(c, int) else pl.multiple_of(c * kc, kc)
        for j in range(n):
            s = jnp.dot(k_of(j, off, kc), q_list[j], preferred_element_type=F32)
            s_buf[slot, j] = s
            mx_buf[slot, j] = jnp.max(s, axis=0, keepdims=True)

    def softmax_pv(c, slot):
        for j in range(n):
            vblk = vt_ref[0, c, v_rows(j)]
            m_old = m_ref[j]
            m_new = jnp.maximum(m_old, mx_buf[slot, j])
            alpha = jnp.exp2(m_old - m_new)
            p = jnp.exp2(s_buf[slot, j] - m_new)
            l_ref[j] = alpha * l_ref[j] + jnp.sum(p, axis=0, keepdims=True)
            acc_ref[j] = alpha * acc_ref[j] + jnp.dot(vblk, p.astype(BF16),
                                                      preferred_element_type=F32)
            m_ref[j] = m_new

    n_slots = s_buf.shape[0]

    def stage(c, u):
        if not isinstance(c, int) or c + SCORE_LAG < n_chunks:
            scores(c + SCORE_LAG, (u + SCORE_LAG) % n_slots)
        softmax_pv(c, u)

    for c in range(SCORE_LAG):
        scores(c, c % n_slots)
    n_loop = (n_chunks - SCORE_LAG) // n_slots

    def body(i, carry):
        for u in range(n_slots):
            stage(i * n_slots + u, u)
        return carry

    lax.fori_loop(0, n_loop, body, 0)
    for c in range(n_loop * n_slots, n_chunks):
        stage(c, c % n_slots)


def _finish_a(o_ref, l_ref, acc_ref):
    outs = [acc_ref[j] * (1.0 / l_ref[j]) for j in range(A_STREAMS)]
    o_ref[0] = jnp.concatenate(outs, axis=0).T.astype(BF16)


def _finish_b(lambda_init, lam_refs, sub_ref, o_ref, l_ref, acc_ref):
    lq1_ref, lk1_ref, lq2_ref, lk2_ref = lam_refs
    lam = (jnp.exp(jnp.sum(lq1_ref[...] * lk1_ref[...], axis=-1, keepdims=True))
           - jnp.exp(jnp.sum(lq2_ref[...] * lk2_ref[...], axis=-1, keepdims=True))
           + lambda_init)
    outs = []
    for hd in range(B_STEP_HEADS):
        j1, j2 = 2 * hd, 2 * hd + 1
        o = acc_ref[j1] * (1.0 / l_ref[j1]) - lam * (acc_ref[j2] * (1.0 / l_ref[j2]))
        outs.append(_rms(o, 0) * sub_ref[...] * (1.0 - lambda_init))
    o_ref[0] = jnp.concatenate(outs, axis=0).T.astype(BF16)


def _all_rows(j):
    return slice(None)


def _b_rows(j):
    return slice((j // 2) * B_V_DIM, (j // 2 + 1) * B_V_DIM)


def _k_of_a(k_ref):
    return lambda j, off, size: k_ref[0, 0, pl.ds(off, size), :]


def _k_of_b(k_ref):
    return lambda j, off, size: k_ref[0, j // 2, j % 2, pl.ds(off, size), :]


def _attn_a_general(qt_ref, k_ref, vt_ref, o_ref, s_buf, mx_buf, m_ref, l_ref, acc_ref):
    q_list = [_plain_query(qt_ref[0, 0, j]) for j in range(A_STREAMS)]
    _flash_streams(q_list, _k_of_a(k_ref), vt_ref, _all_rows, s_buf, mx_buf, m_ref, l_ref, acc_ref)
    _finish_a(o_ref, l_ref, acc_ref)


def _attn_a_fast(qt_ref, k_ref, vt_ref, kmax_ref, o_ref, l_ref, acc_ref):
    q_list = [_shifted_query(qt_ref[0, 0, j], kmax_ref[0, 0]) for j in range(A_STREAMS)]
    _flash_fast(q_list, _k_of_a(k_ref), vt_ref, _all_rows, l_ref, acc_ref, *FAST_ORDER_A)
    _finish_a(o_ref, l_ref, acc_ref)


def _attn_b_general(lambda_init, qt_ref, k_ref, vt_ref, lq1_ref, lk1_ref, lq2_ref, lk2_ref,
                    sub_ref, o_ref, s_buf, mx_buf, m_ref, l_ref, acc_ref):
    q_list = [_plain_query(qt_ref[0, 0, hd, c]) for hd in range(B_STEP_HEADS) for c in range(2)]
    _flash_streams(q_list, _k_of_b(k_ref), vt_ref, _b_rows, s_buf, mx_buf, m_ref, l_ref, acc_ref)
    _finish_b(lambda_init, (lq1_ref, lk1_ref, lq2_ref, lk2_ref), sub_ref, o_ref, l_ref, acc_ref)


def _attn_b_fast(lambda_init, qt_ref, k_ref, vt_ref, kmax_ref, lq1_ref, lk1_ref, lq2_ref, lk2_ref,
                 sub_ref, o_ref, l_ref, acc_ref):
    q_list = [_shifted_query(qt_ref[0, 0, hd, c], kmax_ref[0, hd, c])
              for hd in range(B_STEP_HEADS) for c in range(2)]
    _flash_fast(q_list, _k_of_b(k_ref), vt_ref, _b_rows, l_ref, acc_ref, *FAST_ORDER_B)
    _finish_b(lambda_init, (lq1_ref, lk1_ref, lq2_ref, lk2_ref), sub_ref, o_ref, l_ref, acc_ref)


def _row_parts(ref, n_parts):
    rows = ref.shape[0] // n_parts
    return [slice(i * rows, (i + 1) * rows) for i in range(n_parts)]


def _merge_kernel(x_ref, oa_ref, ob_ref, nw_ref, wg_s, wa_s, wb_s, wo_s,
                  fg_ref, fu_ref, fd_ref, x1_ref, fg_out, fu_out, fd_out):
    for src, dst in ((fg_ref, fg_out), (fu_ref, fu_out), (fd_ref, fd_out)):
        dst[...] = src[...].astype(BF16)

    d = x_ref.shape[-1]

    def stage1(r):
        x = x_ref[r]
        h = (_rms(x, -1) * nw_ref[...]).astype(BF16)
        gates = jnp.dot(h, wg_s[...], preferred_element_type=F32)
        ya = jnp.dot(oa_ref[r], wa_s[...], preferred_element_type=F32)
        yb = jnp.dot(ob_ref[r], wb_s[...], preferred_element_type=F32)
        return r, x, gates, ya, yb

    def stage2(r, x, gates, ya, yb):
        y = jax.nn.sigmoid(gates[:, :d]) * ya + jax.nn.sigmoid(gates[:, d:]) * yb
        x1_ref[r] = x + jnp.dot(y.astype(BF16), wo_s[...], preferred_element_type=F32)

    pending = None
    for r in _row_parts(x_ref, DENSE_PARTS):
        nxt = stage1(r)
        if pending is not None:
            stage2(*pending)
        pending = nxt
    stage2(*pending)


def _ffn_kernel(final, x_ref, nw_ref, wg_ref, wu_ref, wd_ref, nf_ref, o_ref):
    def stage1(r):
        x = x_ref[r]
        h = (_rms(x, -1) * nw_ref[...]).astype(BF16)
        gate = jnp.dot(h, wg_ref[...], preferred_element_type=F32)
        up = jnp.dot(h, wu_ref[...], preferred_element_type=F32)
        return r, x, gate, up

    def stage2(r, x, gate, up):
        act = (jax.nn.silu(gate) * up).astype(BF16)
        x2 = x + jnp.dot(act, wd_ref[...], preferred_element_type=F32)
        if final:
            x2 = _rms(x2, -1) * nf_ref[...]
        o_ref[r] = x2

    pending = None
    for r in _row_parts(x_ref, FFN_PARTS):
        nxt = stage1(r)
        if pending is not None:
            stage2(*pending)
        pending = nxt
    stage2(*pending)


def _const_spec(shape):
    nd = len(shape)
    return pl.BlockSpec(shape, lambda *_: (0,) * nd, pipeline_mode=pl.Buffered(1))


def _chunk_spec(shape, n_steps, step_of=lambda i: i):
    n_rows = shape[0]
    n_chunks = n_steps
    while n_rows % (n_chunks * 2 * SUBLANES):
        assert n_chunks % 2 == 0, shape
        n_chunks //= 2
    repeat = n_steps // n_chunks
    return pl.BlockSpec((n_rows // n_chunks, shape[1]), lambda *ids: (step_of(*ids) // repeat, 0))


def _rope_angles_t(pos, dim, theta):
    inv_freq = theta ** (-jnp.arange(0, dim, 2, dtype=F32) / dim)
    return (pos[:, None] * inv_freq[None, :]).T


def _angle_tables(seq_len):
    rows = seq_len // GRID_W
    row = jnp.broadcast_to(jnp.arange(rows, dtype=F32)[:, None], (rows, GRID_W)).reshape(-1)
    col = jnp.broadcast_to(jnp.arange(GRID_W, dtype=F32)[None, :], (rows, GRID_W)).reshape(-1)
    ang_a = jnp.concatenate([_rope_angles_t(row, HALF, AXIAL_THETA),
                             _rope_angles_t(col, HALF, AXIAL_THETA)], axis=0)
    ang_b = _rope_angles_t(jnp.arange(seq_len, dtype=F32), HEAD_DIM, ROPE_THETA)
    return jnp.cos(ang_a), jnp.sin(ang_a), jnp.cos(ang_b), jnp.sin(ang_b)


def _params(*sem):
    return pltpu.CompilerParams(dimension_semantics=sem, vmem_limit_bytes=VMEM_LIMIT)


def _layer(x, lambda_init, final, norm_mix, w_in, q_norm_a, k_norm_a, lq1, lk1, lq2, lk2, subln,
           w_proj_a, w_proj_b, w_out, norm_ffn, w_gate, w_up, w_down, norm_final, tables):
    b, s, d = x.shape
    d_ff = w_gate.shape[-1]
    n_tok = b * s
    n_chunks = s // PREP_TM
    ca, sa, cb, sb = tables
    row = lambda v: v.reshape(1, -1).astype(F32)
    col = lambda v: v.reshape(-1, 1).astype(F32)

    tab_spec = pl.BlockSpec((HALF, PREP_TM), lambda bi, ti: (0, ti))
    prep_step = lambda bi, ti: bi * n_chunks + ti
    merge_w_shapes = [(d, 2 * d), w_proj_a.shape, w_proj_b.shape, w_out.shape]
    merge_w_in = [_chunk_spec(sh, b * n_chunks, prep_step)
                  for sh in [w_in.shape] + merge_w_shapes[1:]]
    merge_w_out = [_chunk_spec(sh, b * n_chunks, prep_step) for sh in merge_w_shapes]
    qta, ka, vta, qtb, kb, vtb, kn, qn, *merge_w = pl.pallas_call(
        _prep_kernel,
        grid=(b, n_chunks),
        in_specs=[
            pl.BlockSpec((1, PREP_TM, d), lambda bi, ti: (bi, ti, 0)),
            _const_spec((1, d)),
            _const_spec((d, OFF_GATES)),
            _const_spec((HEAD_DIM, 1)),
            _const_spec((HEAD_DIM, 1)),
            tab_spec, tab_spec, tab_spec, tab_spec,
        ] + merge_w_in,
        out_specs=[
            pl.BlockSpec((1, 1, A_Q_HEADS, SLAB, PREP_TM), lambda bi, ti: (bi, ti, 0, 0, 0)),
            pl.BlockSpec((1, A_KV_HEADS, PREP_TM, SLAB), lambda bi, ti: (bi, 0, ti, 0)),
            pl.BlockSpec((1, 1, A_KV_HEADS * HEAD_DIM, PREP_TM), lambda bi, ti: (bi, ti, 0, 0)),
            pl.BlockSpec((1, 1, B_HEADS, 2, SLAB, PREP_TM), lambda bi, ti: (bi, ti, 0, 0, 0, 0)),
            pl.BlockSpec((1, B_HEADS, 2, PREP_TM, SLAB), lambda bi, ti: (bi, 0, 0, ti, 0)),
            pl.BlockSpec((1, 1, B_WIDTH, PREP_TM), lambda bi, ti: (bi, ti, 0, 0)),
            pl.BlockSpec((1, NORM_ROWS, PREP_TM), lambda bi, ti: (bi, 0, ti)),
            pl.BlockSpec((1, NORM_ROWS, PREP_TM), lambda bi, ti: (bi, 0, ti)),
        ] + merge_w_out,
        out_shape=[
            jax.ShapeDtypeStruct((b, n_chunks, A_Q_HEADS, SLAB, PREP_TM), BF16),
            jax.ShapeDtypeStruct((b, A_KV_HEADS, s, SLAB), BF16),
            jax.ShapeDtypeStruct((b, n_chunks, A_KV_HEADS * HEAD_DIM, PREP_TM), BF16),
            jax.ShapeDtypeStruct((b, n_chunks, B_HEADS, 2, SLAB, PREP_TM), BF16),
            jax.ShapeDtypeStruct((b, B_HEADS, 2, s, SLAB), BF16),
            jax.ShapeDtypeStruct((b, n_chunks, B_WIDTH, PREP_TM), BF16),
            jax.ShapeDtypeStruct((b, NORM_ROWS, s), F32),
            jax.ShapeDtypeStruct((b, NORM_ROWS, s), F32),
        ] + [jax.ShapeDtypeStruct(sh, BF16) for sh in merge_w_shapes],
        scratch_shapes=[pltpu.VMEM((d, OFF_GATES), BF16)],
        compiler_params=_params("arbitrary", "arbitrary"),
        name="prep",
    )(x, row(norm_mix), w_in, col(q_norm_a), col(k_norm_a), ca, sa, cb, sb,
      w_in, w_proj_a, w_proj_b, w_out)

    nq = s // ATTN_TQ
    steps_per_group = A_GROUP // A_STREAMS
    kmax_a = jnp.max(kn[:, :A_KV_HEADS], axis=-1)
    kmax_b = jnp.max(kn[:, SUBLANES:], axis=-1).reshape(b, B_HEADS, 2)
    qmax_a = jnp.max(qn[:, :A_Q_HEADS], axis=-1)
    qmax_b = jnp.max(qn[:, A_Q_HEADS:], axis=-1).reshape(b, B_HEADS, 2)
    fast_a = jnp.max(qmax_a * jnp.repeat(kmax_a, A_GROUP, axis=1)) < FAST_BOUND
    fast_b = jnp.max(qmax_b * kmax_b) < FAST_BOUND

    def general_scratch(n, dv):
        return [
            pltpu.VMEM((SCORE_SLOTS, n, PREP_TM, ATTN_TQ), F32),
            pltpu.VMEM((SCORE_SLOTS, n, 1, ATTN_TQ), F32),
            pltpu.VMEM((n, 1, ATTN_TQ), F32),
            pltpu.VMEM((n, 1, ATTN_TQ), F32),
            pltpu.VMEM((n, dv, ATTN_TQ), F32),
        ]

    def fast_scratch(n, dv):
        return [pltpu.VMEM((n, 1, ATTN_TQ), F32), pltpu.VMEM((n, dv, ATTN_TQ), F32)]

    a_specs = [
        pl.BlockSpec((1, 1, A_STREAMS, SLAB, ATTN_TQ), lambda bi, hp, qi: (bi, qi, hp, 0, 0)),
        pl.BlockSpec((1, 1, s, SLAB), lambda bi, hp, qi: (bi, hp // steps_per_group, 0, 0)),
        pl.BlockSpec((1, n_chunks, HEAD_DIM, PREP_TM),
                     lambda bi, hp, qi: (bi, 0, hp // steps_per_group, 0)),
    ]
    a_kmax_spec = pl.BlockSpec((1, 1, 1, ATTN_TQ), lambda bi, hp, qi: (bi, hp // steps_per_group, 0, 0))
    a_common = dict(
        grid=(b, A_Q_HEADS // A_STREAMS, nq),
        out_specs=pl.BlockSpec((1, ATTN_TQ, A_STREAMS * HEAD_DIM), lambda bi, hp, qi: (bi, qi, hp)),
        out_shape=jax.ShapeDtypeStruct((b, s, A_WIDTH), BF16),
        compiler_params=_params("parallel", "parallel", "arbitrary"),
    )

    kmax_a_rows = jnp.broadcast_to(kmax_a[:, :, None, None], (b, A_KV_HEADS, 1, ATTN_TQ))
    kmax_b_rows = jnp.broadcast_to(kmax_b[:, :, :, None, None], (b, B_HEADS, 2, 1, ATTN_TQ))

    def attn_a_fast():
        return pl.pallas_call(_attn_a_fast, in_specs=a_specs + [a_kmax_spec],
                              scratch_shapes=fast_scratch(A_STREAMS, HEAD_DIM), name="attn_a_fast",
                              **a_common)(qta, ka, vta, kmax_a_rows)

    def attn_a_general():
        return pl.pallas_call(_attn_a_general, in_specs=a_specs,
                              scratch_shapes=general_scratch(A_STREAMS, HEAD_DIM), name="attn_a",
                              **a_common)(qta, ka, vta)

    oa = lax.cond(fast_a, attn_a_fast, attn_a_general)

    lam_spec = _const_spec((1, HEAD_DIM))
    hb = B_STEP_HEADS
    b_specs = [
        pl.BlockSpec((1, 1, hb, 2, SLAB, ATTN_TQ), lambda bi, hp, qi: (bi, qi, hp, 0, 0, 0)),
        pl.BlockSpec((1, hb, 2, s, SLAB), lambda bi, hp, qi: (bi, hp, 0, 0, 0)),
        pl.BlockSpec((1, n_chunks, hb * B_V_DIM, PREP_TM), lambda bi, hp, qi: (bi, 0, hp, 0)),
    ]
    b_kmax_spec = pl.BlockSpec((1, hb, 2, 1, ATTN_TQ), lambda bi, hp, qi: (bi, hp, 0, 0, 0))
    b_tail_specs = [lam_spec, lam_spec, lam_spec, lam_spec, _const_spec((B_V_DIM, 1))]
    b_tail = (row(lq1), row(lk1), row(lq2), row(lk2), col(subln))
    b_common = dict(
        grid=(b, B_HEADS // hb, nq),
        out_specs=pl.BlockSpec((1, ATTN_TQ, hb * B_V_DIM), lambda bi, hp, qi: (bi, qi, hp)),
        out_shape=jax.ShapeDtypeStruct((b, s, B_WIDTH), BF16),
        compiler_params=_params("parallel", "parallel", "arbitrary"),
    )

    def attn_b_fast():
        return pl.pallas_call(functools.partial(_attn_b_fast, lambda_init),
                              in_specs=b_specs + [b_kmax_spec] + b_tail_specs,
                              scratch_shapes=fast_scratch(2 * hb, B_V_DIM), name="attn_b_fast",
                              **b_common)(qtb, kb, vtb, kmax_b_rows, *b_tail)

    def attn_b_general():
        return pl.pallas_call(functools.partial(_attn_b_general, lambda_init),
                              in_specs=b_specs + b_tail_specs,
                              scratch_shapes=general_scratch(2 * hb, B_V_DIM), name="attn_b",
                              **b_common)(qtb, kb, vtb, *b_tail)

    ob = lax.cond(fast_b, attn_b_fast, attn_b_general)

    tok_spec = lambda width: pl.BlockSpec((DENSE_TM, width), lambda ti: (ti, 0))
    x_flat = x.reshape(n_tok, d)
    merge_steps = n_tok // DENSE_TM
    ffn_shapes = [w_gate.shape, w_up.shape, w_down.shape]
    ffn_specs = [_chunk_spec(sh, merge_steps) for sh in ffn_shapes]
    x1, *ffn_w = pl.pallas_call(
        _merge_kernel,
        grid=(merge_steps,),
        in_specs=[
            tok_spec(d), tok_spec(A_WIDTH), tok_spec(B_WIDTH),
            _const_spec((1, d)),
        ] + [_const_spec(sh) for sh in merge_w_shapes] + ffn_specs,
        out_specs=[tok_spec(d)] + ffn_specs,
        out_shape=[jax.ShapeDtypeStruct((n_tok, d), F32)]
        + [jax.ShapeDtypeStruct(sh, BF16) for sh in ffn_shapes],
        compiler_params=_params("arbitrary"),
        name="merge",
    )(x_flat, oa.reshape(n_tok, A_WIDTH), ob.reshape(n_tok, B_WIDTH), row(norm_mix),
      *merge_w, w_gate, w_up, w_down)

    x2 = pl.pallas_call(
        functools.partial(_ffn_kernel, final),
        grid=(n_tok // FFN_TM,),
        in_specs=[
            pl.BlockSpec((FFN_TM, d), lambda ti: (ti, 0)),
            _const_spec((1, d)),
            _const_spec((d, d_ff)),
            _const_spec((d, d_ff)),
            _const_spec((d_ff, d)),
            _const_spec((1, d)),
        ],
        out_specs=pl.BlockSpec((FFN_TM, d), lambda ti: (ti, 0)),
        out_shape=jax.ShapeDtypeStruct((n_tok, d), F32),
        compiler_params=_params("parallel"),
        name="ffn",
    )(x1, row(norm_ffn), *ffn_w, row(norm_final))
    return x2.reshape(b, s, d)


def kernel(x, norm_mix, w_in, q_norm_a, k_norm_a, lambda_q1, lambda_k1, lambda_q2, lambda_k2,
           subln_b, w_proj_a, w_proj_b, w_out, norm_ffn, w_gate_ffn, w_up_ffn, w_down_ffn,
           norm_final):
    depth = norm_mix.shape[0]
    tables = _angle_tables(x.shape[1])
    for l in range(depth):
        lambda_init = 0.8 - 0.6 * math.exp(-0.3 * l)
        x = _layer(x, lambda_init, l == depth - 1, norm_mix[l], w_in[l], q_norm_a[l], k_norm_a[l],
                   lambda_q1[l], lambda_k1[l], lambda_q2[l], lambda_k2[l], subln_b[l],
                   w_proj_a[l], w_proj_b[l], w_out[l], norm_ffn[l], w_gate_ffn[l], w_up_ffn[l],
                   w_down_ffn[l], norm_final, tables)
    return x
```
